```python
import jax, jax.numpy as jnp
from jax import lax
import numpy as np

D_MODEL = 2048
BATCH = 4
SEQ = 4096
DEPTH = 2

GRID_W = 64
CTX_LEN = 256
EPS = 1e-6
ROPE_THETA = 10000.0

HGRN_DK = 128
HGRN_DV = 128
HGRN_HEADS = (D_MODEL // 2) // HGRN_DK
HGRN_CHUNK = 64
WIN_HEAD_DIM = 64
WIN_HEADS = (D_MODEL // 2) // WIN_HEAD_DIM
WIN_KV_HEADS = 2
WINDOW = 128
WIN_BLOCK = 128
GLB_HEAD_DIM = 128
GLB_HEADS = D_MODEL // GLB_HEAD_DIM
GLB_KV_HEADS = GLB_HEADS // 4
Q_BLOCK = 128
N_EXPERTS = 64
N_GROUPS = 8
TOPK_GROUPS = 4
TOP_K = 8
D_EXPERT = D_MODEL // 4
D_SHARED = D_MODEL // 4
ROUTED_SCALE = 2.5
MOE_BLOCK = 256

A_K = HGRN_HEADS * HGRN_DK
A_V = HGRN_HEADS * HGRN_DV
B_Q = WIN_HEADS * WIN_HEAD_DIM
B_KV = WIN_KV_HEADS * WIN_HEAD_DIM
EVEN_SPLITS = (A_K, A_K + A_V, 2 * A_K + A_V, 3 * A_K + A_V, 3 * A_K + 2 * A_V,
               3 * A_K + 2 * A_V + B_Q, 3 * A_K + 2 * A_V + B_Q + B_KV)
EVEN_IN = 3 * A_K + 2 * A_V + B_Q + 2 * B_KV
EVEN_OUT_IN = A_V + B_Q
C_Q = GLB_HEADS * GLB_HEAD_DIM
C_KV = GLB_KV_HEADS * GLB_HEAD_DIM
ODD_SPLITS = (C_Q, C_Q + C_KV)
ODD_IN = C_Q + 2 * C_KV

kernel_name = 'hybrid_flow_hgrn2_swa_gqa_moe'

f32 = jnp.float32


def rms_norm(x, g):
    xf = x.astype(f32)
    y = xf * lax.rsqrt(jnp.mean(xf * xf, axis=-1, keepdims=True) + EPS)
    return y.astype(x.dtype) * g


def axial_rope(x, rows, cols):
    L = x.shape[1]
    half = x.shape[-1] // 2
    xf = x.astype(f32)

    def rot(xp, pos):
        m = xp.shape[-1]
        freqs = ROPE_THETA ** (-jnp.arange(0, m, 2, dtype=f32) / m)
        ang = pos.astype(f32)[:, None] * freqs[None, :]
        shape = (1, L) + (1,) * (x.ndim - 3) + (m // 2,)
        cos = jnp.cos(ang).reshape(shape)
        sin = jnp.sin(ang).reshape(shape)
        x1, x2 = xp[..., :m // 2], xp[..., m // 2:]
        return jnp.concatenate([x1 * cos - x2 * sin, x2 * cos + x1 * sin], axis=-1)

    out = jnp.concatenate([rot(xf[..., :half], rows), rot(xf[..., half:], cols)], axis=-1)
    return out.astype(x.dtype)


def dense_attention(q, k, v, sink=None):
    B, Lq, KV, G, Dh = q.shape
    Lk = k.shape[1]
    s = jnp.einsum('bqkgd,bskd->bkgqs', q, k).astype(f32) * Dh ** -0.5
    if sink is not None:
        sk = jnp.broadcast_to(sink.astype(f32).reshape(1, KV, G, 1, 1), s.shape[:-1] + (1,))
        s = jnp.concatenate([s, sk], axis=-1)
    p = jax.nn.softmax(s, axis=-1)[..., :Lk].astype(v.dtype)
    return jnp.einsum('bkgqs,bskd->bqkgd', p, v).reshape(B, Lq, KV * G * Dh)


def window_attention(q, k, v, kc, vc, sink):
    B, L, KV, G, Dh = q.shape
    W = WIN_BLOCK
    nb = L // W
    scale = Dh ** -0.5
    pad = ((0, 0), (W, W), (0, 0), (0, 0))
    kp = jnp.pad(k, pad).reshape(B, nb + 2, W, KV, Dh)
    vp = jnp.pad(v, pad).reshape(B, nb + 2, W, KV, Dh)
    kw = jnp.concatenate([kp[:, :-2], kp[:, 1:-1], kp[:, 2:]], axis=2)
    vw = jnp.concatenate([vp[:, :-2], vp[:, 1:-1], vp[:, 2:]], axis=2)
    qb = q.reshape(B, nb, W, KV, G, Dh)
    qpos = jnp.arange(nb)[:, None, None] * W + jnp.arange(W)[None, :, None]
    kpos = (jnp.arange(nb)[:, None, None] - 1) * W + jnp.arange(3 * W)[None, None, :]
    valid = (jnp.abs(kpos - qpos) <= WINDOW) & (kpos >= 0) & (kpos < L)
    s_win = jnp.einsum('bnqkgd,bnskd->bnkgqs', qb, kw).astype(f32) * scale
    s_win = jnp.where(valid[None, :, None, None], s_win, -jnp.inf)
    s_ctx = jnp.einsum('bnqkgd,bckd->bnkgqc', qb, kc).astype(f32) * scale
    s_sink = jnp.broadcast_to(sink.astype(f32).reshape(1, 1, KV, G, 1, 1), s_win.shape[:-1] + (1,))
    p = jax.nn.softmax(jnp.concatenate([s_win, s_ctx, s_sink], axis=-1), axis=-1).astype(v.dtype)
    n_win = 3 * W
    n_ctx = kc.shape[1]
    out = (jnp.einsum('bnkgqs,bnskd->bnqkgd', p[..., :n_win], vw)
           + jnp.einsum('bnkgqc,bckd->bnqkgd', p[..., n_win:n_win + n_ctx], vc))
    return out.reshape(B, L, KV * G * Dh)


def global_attention(q, k, v):
    B, L, KV, G, Dh = q.shape
    nb = L // Q_BLOCK
    qb = q.reshape(B, nb, Q_BLOCK, KV, G, Dh).transpose(1, 0, 2, 3, 4, 5)

    def block(qi):
        s = jnp.einsum('bqkgd,bskd->bkgqs', qi, k).astype(f32) * Dh ** -0.5
        p = jax.nn.softmax(s, axis=-1).astype(v.dtype)
        return jnp.einsum('bkgqs,bskd->bqkgd', p, v)

    out = lax.map(block, qb)
    return out.transpose(1, 0, 2, 3, 4, 5).reshape(B, L, KV * G * Dh)


def gla_chunked(q, k, v, log_f, s0):
    B, L, H, DK = q.shape
    DV = v.shape[-1]
    n = L // HGRN_CHUNK

    def to_chunks(t):
        return t.reshape(B, n, HGRN_CHUNK, H, t.shape[-1]).transpose(1, 0, 3, 2, 4)

    tri = jnp.tril(jnp.ones((HGRN_CHUNK, HGRN_CHUNK), bool))[None, None, :, :, None]

    def step(S, inp):
        qb, kb, vb, gb = inp
        b = jnp.cumsum(gb, axis=2)
        o_inter = jnp.einsum('bhtk,bhkv->bhtv', qb * jnp.exp(b), S)
        decay = jnp.exp(jnp.where(tri, b[:, :, :, None, :] - b[:, :, None, :, :], -jnp.inf))
        att = jnp.einsum('bhtsk,bhsk->bhts', qb[:, :, :, None, :] * decay, kb)
        o_intra = jnp.einsum('bhts,bhsv->bhtv', att, vb)
        b_last = b[:, :, -1:, :]
        S_new = (jnp.exp(b_last[:, :, 0, :])[..., None] * S
                 + jnp.einsum('bhsk,bhsv->bhkv', kb * jnp.exp(b_last - b), vb))
        return S_new, o_inter + o_intra

    s_final, o = lax.scan(step, s0, tuple(to_chunks(t) for t in (q, k, v, log_f)))
    return o.transpose(1, 0, 3, 2, 4).reshape(B, L, H, DV), s_final


def hgrn_prep(q, v, f_fwd, f_bwd, lb):
    B, n, _ = q.shape

    def heads(t, d):
        return t.astype(f32).reshape(B, n, HGRN_HEADS, d)

    def forget(f_raw):
        f = lb + (1.0 - lb) * jax.nn.sigmoid(f_raw.astype(f32))
        return heads(1.0 - f, HGRN_DK), heads(jnp.log(f), HGRN_DK)

    return heads(jax.nn.silu(q.astype(f32)), HGRN_DK), heads(v, HGRN_DV), forget(f_fwd), forget(f_bwd)


def hgrn_out(o, g, gain):
    B, n, H, DV = o.shape
    o = o * lax.rsqrt(jnp.mean(o * o, axis=-1, keepdims=True) + EPS) * gain.astype(f32)
    o = o * jax.nn.silu(g.astype(f32).reshape(B, n, H, DV))
    return o.reshape(B, n, H * DV).astype(g.dtype)


def even_mixer(h, hc, w_in, w_out, lb, g_onorm, sink, rows, cols, need_ctx):
    B, L, _ = h.shape
    Lc = hc.shape[1]
    G = WIN_HEADS // WIN_KV_HEADS
    q_a, i_a, ff_a, fb_a, g_a, q_b, k_b, v_b = jnp.split(h @ w_in, EVEN_SPLITS, axis=-1)
    cq_a, ci_a, cff_a, cfb_a, cg_a, cq_b, ck_b, cv_b = jnp.split(hc @ w_in, EVEN_SPLITS, axis=-1)

    ql, vl, (kfl, gfl), (kbl, gbl) = hgrn_prep(q_a, i_a, ff_a, fb_a, lb)
    qc, vc, (kfc, gfc), (kbc, gbc) = hgrn_prep(cq_a, ci_a, cff_a, cfb_a, lb)
    s0 = jnp.zeros((B, HGRN_HEADS, HGRN_DK, HGRN_DV), f32)
    oc_f, sc_f = gla_chunked(qc, kfc, vc, gfc, s0)
    oc_b, sc_b = gla_chunked(qc[:, ::-1], kbc[:, ::-1], vc[:, ::-1], gbc[:, ::-1], s0)
    ol_f, _ = gla_chunked(ql, kfl, vl, gfl, sc_f)
    ol_b, _ = gla_chunked(ql[:, ::-1], kbl[:, ::-1], vl[:, ::-1], gbl[:, ::-1], sc_b)
    a_lat = hgrn_out(ol_f + ol_b[:, ::-1], g_a, g_onorm)

    qb = axial_rope(q_b.reshape(B, L, WIN_HEADS, WIN_HEAD_DIM), rows, cols).reshape(B, L, WIN_KV_HEADS, G, WIN_HEAD_DIM)
    kb = axial_rope(k_b.reshape(B, L, WIN_KV_HEADS, WIN_HEAD_DIM), rows, cols)
    vb = v_b.reshape(B, L, WIN_KV_HEADS, WIN_HEAD_DIM)
    ckb = ck_b.reshape(B, Lc, WIN_KV_HEADS, WIN_HEAD_DIM)
    cvb = cv_b.reshape(B, Lc, WIN_KV_HEADS, WIN_HEAD_DIM)
    b_lat = window_attention(qb, kb, vb, ckb, cvb, sink)

    y = jnp.concatenate([a_lat, b_lat], axis=-1) @ w_out
    yc = None
    if need_ctx:
        a_ctx = hgrn_out(oc_f + oc_b[:, ::-1], cg_a, g_onorm)
        b_ctx = dense_attention(cq_b.reshape(B, Lc, WIN_KV_HEADS, G, WIN_HEAD_DIM), ckb, cvb, sink)
        yc = jnp.concatenate([a_ctx, b_ctx], axis=-1) @ w_out
    return y, yc


def odd_mixer(h, hc, w_in, w_out, gq, gk, rows, cols, need_ctx):
    B, L, _ = h.shape
    Lc = hc.shape[1]
    G = GLB_HEADS // GLB_KV_HEADS

    def qkv(t, n):
        q, k, v = jnp.split(t @ w_in, ODD_SPLITS, axis=-1)
        q = rms_norm(q.reshape(B, n, GLB_HEADS, GLB_HEAD_DIM), gq)
        k = rms_norm(k.reshape(B, n, GLB_KV_HEADS, GLB_HEAD_DIM), gk)
        return q, k, v.reshape(B, n, GLB_KV_HEADS, GLB_HEAD_DIM)

    q, k, v = qkv(h, L)
    cq, ck, cv = qkv(hc, Lc)
    q = axial_rope(q, rows, cols).reshape(B, L, GLB_KV_HEADS, G, GLB_HEAD_DIM)
    k = axial_rope(k, rows, cols)
    k_all = jnp.concatenate([ck, k], axis=1)
    v_all = jnp.concatenate([cv, v], axis=1)
    y = global_attention(q, k_all, v_all) @ w_out
    yc = None
    if need_ctx:
        yc = dense_attention(cq.reshape(B, Lc, GLB_KV_HEADS, G, GLB_HEAD_DIM), ck, cv) @ w_out
    return y, yc


def moe_ffn(h, w_router, b_router, w_gate, w_up, w_down, ws_gate, ws_up, ws_down):
    T, D = h.shape
    per_group = N_EXPERTS // N_GROUPS
    scores = jax.nn.sigmoid((h @ w_router).astype(f32))
    biased = scores + b_router.astype(f32)
    group_score = lax.top_k(biased.reshape(T, N_GROUPS, per_group), 2)[0].sum(-1)
    _, top_groups = lax.top_k(group_score, TOPK_GROUPS)
    group_mask = jnp.any(top_groups[:, :, None] == jnp.arange(N_GROUPS)[None, None, :], axis=1)
    expert_mask = jnp.repeat(group_mask, per_group, axis=1)
    _, idx = lax.top_k(jnp.where(expert_mask, biased, -jnp.inf), TOP_K)
    wts = jnp.take_along_axis(scores, idx, axis=1)
    wts = wts / jnp.sum(wts, axis=-1, keepdims=True) * ROUTED_SCALE

    n_assign = T * TOP_K
    flat_e = idx.reshape(-1)
    flat_tok = jnp.arange(n_assign, dtype=jnp.int32) // TOP_K
    order = jnp.argsort(flat_e)
    sorted_e = flat_e[order]
    counts = jnp.bincount(flat_e, length=N_EXPERTS)
    padded = (counts + MOE_BLOCK - 1) // MOE_BLOCK * MOE_BLOCK
    padded_end = jnp.cumsum(padded)
    start = jnp.cumsum(counts) - counts
    dest = (padded_end - padded)[sorted_e] + jnp.arange(n_assign) - start[sorted_e]
    n_blocks = (n_assign + N_EXPERTS * (MOE_BLOCK - 1) + MOE_BLOCK - 1) // MOE_BLOCK
    n_slots = n_blocks * MOE_BLOCK
    slot_tok = jnp.zeros((n_slots,), jnp.int32).at[dest].set(flat_tok[order])
    slot_w = jnp.zeros((n_slots,), f32).at[dest].set(wts.reshape(-1)[order])
    block_e = jnp.minimum(jnp.searchsorted(padded_end, jnp.arange(n_blocks) * MOE_BLOCK, side='right'), N_EXPERTS - 1)

    def expert_block(args):
        tok, wb, e = args
        xb = h[tok]
        hid = jax.nn.silu(xb @ w_gate[e]) * (xb @ w_up[e])
        return (hid @ w_down[e]) * wb[:, None].astype(h.dtype)

    y_slots = lax.map(expert_block, (slot_tok.reshape(n_blocks, MOE_BLOCK), slot_w.reshape(n_blocks, MOE_BLOCK), block_e))
    routed = jax.ops.segment_sum(y_slots.reshape(n_slots, D), slot_tok, num_segments=T)
    shared = (jax.nn.silu(h @ ws_gate) * (h @ ws_up)) @ ws_down
    return routed.astype(h.dtype) + shared


def setup_inputs(seed: int = 0) -> dict:
    key = jax.random.key(seed)
    ks = jax.random.split(key, 32)
    D = D_MODEL
    n_even = (DEPTH + 1) // 2
    n_odd = DEPTH // 2

    def nrm(k, shape, scale):
        return jax.random.normal(k, shape, f32) * scale

    return {
        'x': nrm(ks[0], (BATCH, SEQ, D), 1.0),
        'c': nrm(ks[1], (BATCH, D), 1.0),
        'ctx': nrm(ks[2], (BATCH, CTX_LEN, D), 1.0),
        'c_ctx': nrm(ks[3], (D,), 1.0),
        'w_ada': nrm(ks[4], (DEPTH, D, 6 * D), 0.5 * D ** -0.5),
        'b_ada': nrm(ks[5], (DEPTH, 6 * D), 0.01),
        'g_norm_mix': 1.0 + nrm(ks[6], (DEPTH, D), 0.02),
        'g_norm_ffn': 1.0 + nrm(ks[7], (DEPTH, D), 0.02),
        'w_in_even': nrm(ks[8], (n_even, D, EVEN_IN), D ** -0.5),
        'w_out_even': nrm(ks[9], (n_even, EVEN_OUT_IN, D), EVEN_OUT_IN ** -0.5),
        'hgrn_lb_logits': nrm(ks[10], (DEPTH + 1, A_K), 0.5),
        'g_hgrn_norm': 1.0 + nrm(ks[11], (n_even, HGRN_DV), 0.02),
        'win_sink': nrm(ks[12], (n_even, WIN_HEADS), 0.5),
        'w_in_odd': nrm(ks[13], (n_odd, D, ODD_IN), D ** -0.5),
        'w_out_odd': nrm(ks[14], (n_odd, C_Q, D), C_Q ** -0.5),
        'g_q_norm': 1.0 + nrm(ks[15], (n_odd, GLB_HEAD_DIM), 0.02),
        'g_k_norm': 1.0 + nrm(ks[16], (n_odd, GLB_HEAD_DIM), 0.02),
        'w_router': nrm(ks[17], (DEPTH, D, N_EXPERTS), D ** -0.5),
        'b_router': nrm(ks[18], (DEPTH, N_EXPERTS), 0.01),
        'w_exp_gate': nrm(ks[19], (DEPTH, N_EXPERTS, D, D_EXPERT), D ** -0.5),
        'w_exp_up': nrm(ks[20], (DEPTH, N_EXPERTS, D, D_EXPERT), D ** -0.5),
        'w_exp_down': nrm(ks[21], (DEPTH, N_EXPERTS, D_EXPERT, D), D_EXPERT ** -0.5),
        'w_sh_gate': nrm(ks[22], (DEPTH, D, D_SHARED), D ** -0.5),
        'w_sh_up': nrm(ks[23], (DEPTH, D, D_SHARED), D ** -0.5),
        'w_sh_down': nrm(ks[24], (DEPTH, D_SHARED, D), D_SHARED ** -0.5),
        'g_norm_final': 1.0 + nrm(ks[25], (D,), 0.02),
    }


def reference(x, c, ctx, c_ctx, w_ada, b_ada, g_norm_mix, g_norm_ffn, w_in_even, w_out_even, hgrn_lb_logits,
              g_hgrn_norm, win_sink, w_in_odd, w_out_odd, g_q_norm, g_k_norm, w_router, b_router,
              w_exp_gate, w_exp_up, w_exp_down, w_sh_gate, w_sh_up, w_sh_down, g_norm_final):
    B, L, D = x.shape
    Lc = ctx.shape[1]
    ROWS = L // GRID_W
    rows = jnp.repeat(jnp.arange(ROWS, dtype=jnp.int32), GRID_W)
    cols = jnp.tile(jnp.arange(GRID_W, dtype=jnp.int32), ROWS)
    lb_w = jax.nn.softmax(hgrn_lb_logits.astype(f32), axis=0)
    lower_bounds = jnp.cumsum(lb_w, axis=0)[1:] - lb_w[0]

    for l in range(DEPTH):
        need_ctx = l < DEPTH - 1
        mod = (jax.nn.silu(c) @ w_ada[l] + b_ada[l])[:, None, :]
        mod_c = jax.nn.silu(c_ctx) @ w_ada[l] + b_ada[l]
        sh1, sc1, g1, sh2, sc2, g2 = jnp.split(mod, 6, axis=-1)
        csh1, csc1, cg1, csh2, csc2, cg2 = jnp.split(mod_c, 6, axis=-1)

        h = rms_norm(x, g_norm_mix[l]) * (1.0 + sc1) + sh1
        hc = rms_norm(ctx, g_norm_mix[l]) * (1.0 + csc1) + csh1
        j = l // 2
        if l % 2 == 0:
            y, yc = even_mixer(h, hc, w_in_even[j], w_out_even[j], lower_bounds[l], g_hgrn_norm[j],
                               win_sink[j], rows, cols, need_ctx)
        else:
            y, yc = odd_mixer(h, hc, w_in_odd[j], w_out_odd[j], g_q_norm[j], g_k_norm[j], rows, cols, need_ctx)
        x = x + g1 * y

        h = rms_norm(x, g_norm_ffn[l]) * (1.0 + sc2) + sh2
        moe_params = (w_router[l], b_router[l], w_exp_gate[l], w_exp_up[l], w_exp_down[l],
                      w_sh_gate[l], w_sh_up[l], w_sh_down[l])
        if need_ctx:
            ctx = ctx + cg1 * yc
            hc = rms_norm(ctx, g_norm_ffn[l]) * (1.0 + csc2) + csh2
            out = moe_ffn(jnp.concatenate([h.reshape(B * L, D), hc.reshape(B * Lc, D)], axis=0), *moe_params)
            x = x + g2 * out[:B * L].reshape(B, L, D)
            ctx = ctx + cg2 * out[B * L:].reshape(B, Lc, D)
        else:
            x = x + g2 * moe_ffn(h.reshape(B * L, D), *moe_params).reshape(B, L, D)

    return rms_norm(x, g_norm_final)
```

```python
import functools

import jax
import jax.numpy as jnp
from jax import lax
from jax.experimental import pallas as pl
from jax.experimental.pallas import tpu as pltpu

f32 = jnp.float32
bf16 = jnp.bfloat16
i32 = jnp.int32

EPS = 1e-6
ROPE_THETA = 10000.0
GRID_W = 64

HGRN_DK = 128
HGRN_HEADS = 8
HGRN_CHUNK = 64
HGRN_SUB = 16
WIN_HEAD_DIM = 64
WIN_HEADS = 16
WIN_KV_HEADS = 2
WIN_BLOCK = 128
GLB_HEAD_DIM = 128
GLB_HEADS = 16
GLB_KV_HEADS = 4
Q_BLOCK = 128
N_EXPERTS = 64
N_GROUPS = 8
TOPK_GROUPS = 4
TOP_K = 8
ROUTED_SCALE = 2.5
MOE_BLOCK = 256

LANE = 128
ROW_BLOCK = 256
VMEM_LIMIT = 56 * 1024 * 1024

A_K = HGRN_HEADS * HGRN_DK
HGRN_COLS = 5 * A_K
WIN_COLS = WIN_HEADS * WIN_HEAD_DIM + 2 * WIN_KV_HEADS * WIN_HEAD_DIM


def _params(sem, vmem=VMEM_LIMIT):
    return pltpu.CompilerParams(dimension_semantics=sem, vmem_limit_bytes=vmem)


def _silu(x):
    return x * jax.nn.sigmoid(x)


def _dot(a, b):
    return jnp.dot(a, b, preferred_element_type=f32)


def _dot_nt(a, b):
    return lax.dot_general(a, b, (((1,), (1,)), ((), ())), preferred_element_type=f32)


def _dot_tn(a, b):
    return lax.dot_general(a, b, (((0,), (0,)), ((), ())), preferred_element_type=f32)


def _ada_kernel(c_ref, w_ref, b_ref, o_ref):
    a = _silu(c_ref[...]).astype(bf16)
    o_ref[...] = _dot(a, w_ref[...].astype(bf16)) + b_ref[...]


def _ada(c8, w, b, tn=768):
    m, d = c8.shape
    n = w.shape[1]
    return pl.pallas_call(
        _ada_kernel,
        grid=(n // tn,),
        in_specs=[pl.BlockSpec((m, d), lambda j: (0, 0)),
                  pl.BlockSpec((d, tn), lambda j: (0, j)),
                  pl.BlockSpec((1, tn), lambda j: (0, j))],
        out_specs=pl.BlockSpec((m, tn), lambda j: (0, j)),
        out_shape=jax.ShapeDtypeStruct((m, n), f32),
        compiler_params=_params(("arbitrary",)),
    )(c8, w, b.reshape(1, n))


def _norm_mod(x, g, sc, sh):
    y = x * lax.rsqrt(jnp.mean(x * x, axis=-1, keepdims=True) + EPS) * g
    return y * (1.0 + sc) + sh


def _rope_slice(x, cos, sin, quarter):
    lane = lax.broadcasted_iota(i32, x.shape, 1)
    up = pltpu.roll(x, LANE - quarter, axis=1)
    dn = pltpu.roll(x, quarter, axis=1)
    partner = jnp.where(lane % (2 * quarter) < quarter, up, dn)
    return x * cos + partner * sin


def _normmm_kernel(x_ref, g_ref, sc_ref, sh_ref, w_ref, *rest, epilogue, tn):
    if epilogue is None:
        o_ref, h_scr = rest
    elif epilogue["head_norm"]:
        cos_ref, sin_ref, gq_ref, gk_ref, o_ref, h_scr = rest
    else:
        cos_ref, sin_ref, o_ref, h_scr = rest
    j = pl.program_id(1)

    @pl.when(j == 0)
    def _():
        h_scr[...] = _norm_mod(x_ref[...], g_ref[...], sc_ref[...], sh_ref[...]).astype(bf16)

    y = _dot(h_scr[...], w_ref[...])
    if epilogue is None:
        o_ref[...] = y.astype(o_ref.dtype)
        return

    n_sl = tn // LANE
    for jb in range(epilogue["n_col_blocks"]):
        n_rope = epilogue["blocks"].get(jb, 0)

        @pl.when(j == jb)
        def _(jb=jb, n_rope=n_rope):
            cos = cos_ref[...]
            sin = sin_ref[...]
            outs = []
            for s in range(n_sl):
                ys = y[:, s * LANE:(s + 1) * LANE]
                if s < n_rope:
                    col = jb * n_sl + s
                    is_q = col < epilogue["q_slices"]
                    if epilogue["head_norm"]:
                        gain = gq_ref[...] if is_q else gk_ref[...]
                        ys = ys * lax.rsqrt(jnp.mean(ys * ys, axis=-1, keepdims=True) + EPS) * gain
                    ys = _rope_slice(ys, cos, sin, epilogue["quarter"])
                    if is_q:
                        ys = ys * epilogue["q_scale"]
                outs.append(ys.astype(o_ref.dtype))
            o_ref[...] = jnp.concatenate(outs, axis=1)


def _mod_row(i, n_lat_blocks, blocks_per_batch, n_batch):
    return jnp.where(i < n_lat_blocks, i // blocks_per_batch, n_batch)


def _normmm(x, g, mod, which, w, out_dtype, dims, tn, epilogue=None, tables=None, gains=None):
    t, d = x.shape
    n = w.shape[1]
    tm = ROW_BLOCK
    nlb, bpb, nb = dims
    mrow = lambda i: _mod_row(i, nlb, bpb, nb)
    in_specs = [pl.BlockSpec((tm, d), lambda i, j: (i, 0)),
                pl.BlockSpec((1, d), lambda i, j: (0, 0)),
                pl.BlockSpec((None, 1, d), lambda i, j: (mrow(i) * 6 + which + 1, 0, 0)),
                pl.BlockSpec((None, 1, d), lambda i, j: (mrow(i) * 6 + which, 0, 0)),
                pl.BlockSpec((d, tn), lambda i, j: (0, j))]
    args = [x, g.reshape(1, d), mod, mod, w]
    if epilogue is not None:
        cos, sin = tables
        tab = lambda i, j: (jnp.where(i < nlb, i % bpb, bpb), 0)
        in_specs += [pl.BlockSpec((tm, LANE), tab), pl.BlockSpec((tm, LANE), tab)]
        args += [cos, sin]
        epilogue = dict(epilogue, n_col_blocks=n // tn)
        if epilogue["head_norm"]:
            in_specs += [pl.BlockSpec((1, LANE), lambda i, j: (0, 0))] * 2
            args += [gains[0].reshape(1, LANE), gains[1].reshape(1, LANE)]
    return pl.pallas_call(
        functools.partial(_normmm_kernel, epilogue=epilogue, tn=tn),
        grid=(t // tm, n // tn),
        in_specs=in_specs,
        out_specs=pl.BlockSpec((tm, tn), lambda i, j: (i, j)),
        out_shape=jax.ShapeDtypeStruct((t, n), out_dtype),
        scratch_shapes=[pltpu.VMEM((tm, d), bf16)],
        compiler_params=_params(("parallel", "arbitrary")),
    )(*args)


def _gla_chunk(q_raw, v, f_raw, lb, st_ref, a_scr, rev):
    c = HGRN_CHUNK
    sub = HGRN_SUB
    q = _silu(q_raw)
    f = lb + (1.0 - lb) * jax.nn.sigmoid(f_raw)
    kk = 1.0 - f
    g = jnp.log(f)
    row = lax.broadcasted_iota(i32, (c, c), 0)
    col = lax.broadcasted_iota(i32, (c, c), 1)
    tri = (col >= row) if rev else (col <= row)
    b = jnp.dot(tri.astype(f32), g, preferred_element_type=f32, precision=lax.Precision.HIGHEST)
    b_end = b[0:1] if rev else b[c - 1:c]
    st = st_ref[...]
    o = _dot_nt((q * jnp.exp(b)).astype(bf16), st.astype(bf16))

    vb = v.astype(bf16)
    nsub = c // sub
    lane16 = lax.broadcasted_iota(i32, (sub, sub), 1)
    row16 = lax.broadcasted_iota(i32, (sub, sub), 0)
    diag_ok = (row16 <= lane16) if rev else (row16 >= lane16)
    rsub = lax.broadcasted_iota(i32, (c, sub), 0) // sub
    for jb in range(nsub):
        js = slice(jb * sub, (jb + 1) * sub)
        m_j = b[jb * sub:jb * sub + 1] if rev else b[(jb + 1) * sub - 1:(jb + 1) * sub]
        kd = kk[js] * jnp.exp(m_j - b[js])
        qd = q * jnp.exp(jnp.minimum(b - m_j, 0.0))
        a_col = _dot_nt(qd.astype(bf16), kd.astype(bf16))
        off_ok = (rsub < jb) if rev else (rsub > jb)
        a_col = jnp.where(off_ok, a_col, 0.0)
        qi, bi, ki = q[js], b[js], kk[js]
        a_dd = jnp.zeros((sub, sub), f32)
        for s in range(sub):
            e = jnp.exp(jnp.minimum(bi - bi[s:s + 1], 0.0))
            col_s = jnp.sum(qi * ki[s:s + 1] * e, axis=-1, keepdims=True)
            a_dd = jnp.where(lane16 == s, col_s, a_dd)
        a_dd = jnp.where(diag_ok, a_dd, 0.0)
        pieces = [a_dd if ib == jb else a_col[ib * sub:(ib + 1) * sub] for ib in range(nsub)]
        a_scr[:, js] = jnp.concatenate(pieces, axis=0)
    o = o + _dot(a_scr[...].astype(bf16), vb)
    kdec = kk * jnp.exp(b_end - b)
    st_ref[...] = jnp.exp(b_end) * st + _dot_tn(vb, kdec.astype(bf16))
    return o


def _hgrn_kernel(qf_ref, vf_ref, ff_ref, qb_ref, vb_ref, fb_ref, lb_ref, of_ref, ob_ref,
                 stf_ref, stb_ref, af_scr, ab_scr):
    @pl.when(pl.program_id(2) == 0)
    def _():
        stf_ref[...] = jnp.zeros_like(stf_ref)
        stb_ref[...] = jnp.zeros_like(stb_ref)

    lb = lb_ref[...]
    of_ref[...] = _gla_chunk(qf_ref[...], vf_ref[...], ff_ref[...], lb, stf_ref, af_scr, False)
    ob_ref[...] = _gla_chunk(qb_ref[...], vb_ref[...], fb_ref[...], lb, stb_ref, ab_scr, True)


def _hgrn(proj, lb, n_batch, seq, ctx_len):
    t = proj.shape[0]
    c = HGRN_CHUNK
    ncc, ncl = ctx_len // c, seq // c
    lat_chunks = n_batch * ncl

    def fwd_blk(b, s):
        return jnp.where(s < ncc, lat_chunks + b * ncc + s, b * ncl + (s - ncc))

    def bwd_blk(b, s):
        return jnp.where(s < ncc, lat_chunks + b * ncc + (ncc - 1 - s), b * ncl + (ncl - 1 - (s - ncc)))

    def spec(blk, section):
        return pl.BlockSpec((c, HGRN_DK), lambda b, h, s: (blk(b, s), section * HGRN_HEADS + h))

    def ospec(blk):
        return pl.BlockSpec((c, HGRN_DK), lambda b, h, s: (blk(b, s), h))

    out = jax.ShapeDtypeStruct((t, A_K), f32)
    return pl.pallas_call(
        _hgrn_kernel,
        grid=(n_batch, HGRN_HEADS, ncc + ncl),
        in_specs=[spec(fwd_blk, 0), spec(fwd_blk, 1), spec(fwd_blk, 2),
                  spec(bwd_blk, 0), spec(bwd_blk, 1), spec(bwd_blk, 3),
                  pl.BlockSpec((1, HGRN_DK), lambda b, h, s: (0, h))],
        out_specs=[ospec(fwd_blk), ospec(bwd_blk)],
        out_shape=[out, out],
        scratch_shapes=[pltpu.VMEM((HGRN_DK, HGRN_DK), f32), pltpu.VMEM((HGRN_DK, HGRN_DK), f32),
                        pltpu.VMEM((c, c), f32), pltpu.VMEM((c, c), f32)],
        compiler_params=_params(("parallel", "parallel", "arbitrary")),
    )(proj, proj, proj, proj, proj, proj, lb.reshape(1, A_K))


def _hgrn_out_kernel(of_ref, ob_ref, g_ref, gain_ref, o_ref):
    gain = gain_ref[...]
    outs = []
    for h in range(HGRN_HEADS):
        sl = slice(h * HGRN_DK, (h + 1) * HGRN_DK)
        o = of_ref[:, sl] + ob_ref[:, sl]
        o = o * lax.rsqrt(jnp.mean(o * o, axis=-1, keepdims=True) + EPS) * gain
        outs.append((o * _silu(g_ref[:, sl])).astype(bf16))
    o_ref[...] = jnp.concatenate(outs, axis=1)


def _hgrn_out(o_f, o_b, proj, gain):
    t = o_f.shape[0]
    tm = ROW_BLOCK
    return pl.pallas_call(
        _hgrn_out_kernel,
        grid=(t // tm,),
        in_specs=[pl.BlockSpec((tm, A_K), lambda i: (i, 0)),
                  pl.BlockSpec((tm, A_K), lambda i: (i, 0)),
                  pl.BlockSpec((tm, A_K), lambda i: (i, 4)),
                  pl.BlockSpec((1, HGRN_DK), lambda i: (0, 0))],
        out_specs=pl.BlockSpec((tm, A_K), lambda i: (i, 0)),
        out_shape=jax.ShapeDtypeStruct((t, A_K), bf16),
        compiler_params=_params(("parallel",)),
    )(o_f, o_b, proj, gain.reshape(1, HGRN_DK))


def _win_kernel(sink_ref, q_ref, kp_ref, kc_ref, kn_ref, vp_ref, vc_ref, vn_ref, kx_ref, vx_ref, o_ref,
                *, n_lat_blocks):
    n = pl.program_id(1)
    w = WIN_BLOCK
    dh = WIN_HEAD_DIM
    is_lat = n < n_lat_blocks
    ri = lax.broadcasted_iota(i32, (w, w), 0)
    ci = lax.broadcasted_iota(i32, (w, w), 1)
    ok_p = (ci >= ri) & is_lat & (n > 0)
    ok_c = jnp.broadcast_to(is_lat, (w, w))
    ok_n = (ci <= ri) & is_lat & (n < n_lat_blocks - 1)
    valid = jnp.concatenate([ok_p, ok_c, ok_n], axis=1)
    kwin = jnp.concatenate([kp_ref[...], kc_ref[...], kn_ref[...]], axis=0)
    vwin = jnp.concatenate([vp_ref[...], vc_ref[...], vn_ref[...]], axis=0)
    kx = kx_ref[...]
    vx = vx_ref[...]
    g = WIN_HEADS // WIN_KV_HEADS
    outs = []
    for h in range(WIN_HEADS):
        kv = h // g
        ks = slice(kv * dh, (kv + 1) * dh)
        qh = q_ref[:, h * dh:(h + 1) * dh]
        sw = jnp.where(valid, _dot_nt(qh, kwin[:, ks]), -jnp.inf)
        sx = _dot_nt(qh, kx[:, ks])
        sk = sink_ref[h]
        m = jnp.maximum(jnp.maximum(jnp.max(sw, axis=-1, keepdims=True), jnp.max(sx, axis=-1, keepdims=True)), sk)
        pw = jnp.exp(sw - m)
        px = jnp.exp(sx - m)
        den = jnp.sum(pw, axis=-1, keepdims=True) + jnp.sum(px, axis=-1, keepdims=True) + jnp.exp(sk - m)
        o = _dot(pw.astype(bf16), vwin[:, ks]) + _dot(px.astype(bf16), vx[:, ks])
        outs.append((o / den).astype(bf16))
    o_ref[...] = jnp.concatenate(outs, axis=1)


def _win_attn(wproj, sink, n_batch, seq, ctx_len):
    t = wproj.shape[0]
    w = WIN_BLOCK
    nlb = seq // w
    ncb = ctx_len // w
    lat_blocks = n_batch * nlb
    qc = WIN_HEADS * WIN_HEAD_DIM // LANE
    kcol, vcol = qc, qc + 1

    def qrow(b, n):
        return jnp.where(n < nlb, b * nlb + n, lat_blocks + b * ncb + (n - nlb))

    def krow(off):
        def f(b, n):
            return b * nlb + jnp.clip(n + off, 0, nlb - 1)
        return f

    def kspec(off, colblk):
        return pl.BlockSpec((w, LANE), lambda b, n: (krow(off)(b, n), colblk))

    def xspec(colblk):
        return pl.BlockSpec((ctx_len, LANE), lambda b, n: ((n_batch * seq) // ctx_len + b, colblk))

    return pl.pallas_call(
        functools.partial(_win_kernel, n_lat_blocks=nlb),
        grid=(n_batch, nlb + ncb),
        in_specs=[pl.BlockSpec(memory_space=pltpu.SMEM),
                  pl.BlockSpec((w, qc * LANE), lambda b, n: (qrow(b, n), 0)),
                  kspec(-1, kcol), kspec(0, kcol), kspec(1, kcol),
                  kspec(-1, vcol), kspec(0, vcol), kspec(1, vcol),
                  xspec(kcol), xspec(vcol)],
        out_specs=pl.BlockSpec((w, qc * LANE), lambda b, n: (qrow(b, n), 0)),
        out_shape=jax.ShapeDtypeStruct((t, qc * LANE), bf16),
        compiler_params=_params(("parallel", "arbitrary")),
    )(sink.astype(f32), wproj, wproj, wproj, wproj, wproj, wproj, wproj, wproj, wproj)


def _glb_kernel(q_ref, kx_ref, kl_ref, vx_ref, vl_ref, o_ref):
    dh = GLB_HEAD_DIM
    g = GLB_HEADS // GLB_KV_HEADS
    q = jnp.concatenate([q_ref[:, i * dh:(i + 1) * dh] for i in range(g)], axis=0)
    sx = _dot_nt(q, kx_ref[...])
    sl = _dot_nt(q, kl_ref[...])
    m = jnp.maximum(jnp.max(sx, axis=-1, keepdims=True), jnp.max(sl, axis=-1, keepdims=True))
    px = jnp.exp(sx - m)
    plat = jnp.exp(sl - m)
    den = jnp.sum(px, axis=-1, keepdims=True) + jnp.sum(plat, axis=-1, keepdims=True)
    o = (_dot(px.astype(bf16), vx_ref[...]) + _dot(plat.astype(bf16), vl_ref[...])) / den
    tq = q_ref.shape[0]
    o_ref[...] = jnp.concatenate([o[i * tq:(i + 1) * tq] for i in range(g)], axis=1).astype(bf16)


def _glb_attn(qkv, n_batch, seq, ctx_len):
    tq = Q_BLOCK
    dh = GLB_HEAD_DIM
    g = GLB_HEADS // GLB_KV_HEADS
    nq = seq // tq
    kcol0 = GLB_HEADS
    vcol0 = GLB_HEADS + GLB_KV_HEADS
    ctx_blk0 = (n_batch * seq) // ctx_len
    return pl.pallas_call(
        _glb_kernel,
        grid=(n_batch, GLB_KV_HEADS, nq),
        in_specs=[pl.BlockSpec((tq, g * dh), lambda b, k, i: (b * nq + i, k)),
                  pl.BlockSpec((ctx_len, dh), lambda b, k, i: (ctx_blk0 + b, kcol0 + k)),
                  pl.BlockSpec((seq, dh), lambda b, k, i: (b, kcol0 + k)),
                  pl.BlockSpec((ctx_len, dh), lambda b, k, i: (ctx_blk0 + b, vcol0 + k)),
                  pl.BlockSpec((seq, dh), lambda b, k, i: (b, vcol0 + k))],
        out_specs=pl.BlockSpec((tq, g * dh), lambda b, k, i: (b * nq + i, k)),
        out_shape=jax.ShapeDtypeStruct((n_batch * seq, GLB_HEADS * dh), bf16),
        compiler_params=_params(("parallel", "parallel", "arbitrary")),
    )(qkv, qkv, qkv, qkv, qkv)


def _outproj_kernel(a1_ref, a2_ref, w1_ref, w2_ref, x_ref, gate_ref, o_ref):
    y = _dot(a1_ref[...], w1_ref[...]) + _dot(a2_ref[...], w2_ref[...])
    o_ref[...] = x_ref[...] + gate_ref[...] * y


def _outproj(a1, c1, a2, c2, w, x, mod, which, dims, n_rows, tn=1024):
    d = x.shape[1]
    kh = w.shape[0] // 2
    tm = ROW_BLOCK
    nlb, bpb, nb = dims
    mrow = lambda i: _mod_row(i, nlb, bpb, nb)
    return pl.pallas_call(
        _outproj_kernel,
        grid=(n_rows // tm, d // tn),
        in_specs=[pl.BlockSpec((tm, kh), lambda i, j: (i, c1)),
                  pl.BlockSpec((tm, kh), lambda i, j: (i, c2)),
                  pl.BlockSpec((kh, tn), lambda i, j: (0, j)),
                  pl.BlockSpec((kh, tn), lambda i, j: (1, j)),
                  pl.BlockSpec((tm, tn), lambda i, j: (i, j)),
                  pl.BlockSpec((None, 1, tn), lambda i, j: (mrow(i) * 6 + which, 0, j))],
        out_specs=pl.BlockSpec((tm, tn), lambda i, j: (i, j)),
        out_shape=jax.ShapeDtypeStruct((n_rows, d), f32),
        compiler_params=_params(("parallel", "arbitrary")),
    )(a1, a2, w, w, x, mod)


def _router_kernel(x_ref, g_ref, sc_ref, sh_ref, wr_ref, br_ref, u_ref,
                   h_ref, idx_ref, wt_ref, rank_ref, cnt_ref, carry_scr):
    i = pl.program_id(0)
    tm = x_ref.shape[0]
    ne, ng, pg = N_EXPERTS, N_GROUPS, N_EXPERTS // N_GROUPS

    @pl.when(i == 0)
    def _():
        carry_scr[...] = jnp.zeros_like(carry_scr)

    h = _norm_mod(x_ref[...], g_ref[...], sc_ref[...], sh_ref[...])
    h_ref[...] = h
    logits = lax.dot_general(wr_ref[...], h, (((1,), (1,)), ((), ())), preferred_element_type=f32,
                             precision=lax.Precision.HIGHEST)
    s = jax.nn.sigmoid(logits)
    s3 = s.reshape(ng, pg, tm)
    b3 = (s + br_ref[...]).reshape(ng, pg, tm)
    gi = lax.broadcasted_iota(i32, (ng, pg, tm), 0)
    pi = lax.broadcasted_iota(i32, (ng, pg, tm), 1)
    neg = -jnp.inf
    m1 = jnp.max(b3, axis=1, keepdims=True)
    first = jnp.min(jnp.where(b3 == m1, pi, pg), axis=1, keepdims=True)
    m2 = jnp.max(jnp.where(pi == first, neg, b3), axis=1, keepdims=True)
    gs = m1 + m2
    g2 = lax.broadcasted_iota(i32, (ng, 1, tm), 0)
    gmask = jnp.zeros((ng, 1, tm), jnp.bool_)
    for _ in range(TOPK_GROUPS):
        m = jnp.max(gs, axis=0, keepdims=True)
        fi = jnp.min(jnp.where(gs == m, g2, ng), axis=0, keepdims=True)
        pick = g2 == fi
        gmask = gmask | pick
        gs = jnp.where(pick, neg, gs)
    cand = jnp.where(gmask, b3, neg)
    eid = gi * pg + pi
    sel = jnp.zeros((ng, pg, tm), jnp.bool_)
    picks, idxs, wts = [], [], []
    for _ in range(TOP_K):
        m = jnp.max(jnp.max(cand, axis=0, keepdims=True), axis=1, keepdims=True)
        fi = jnp.min(jnp.min(jnp.where(cand == m, eid, ne), axis=0, keepdims=True), axis=1, keepdims=True)
        pick = eid == fi
        picks.append(pick)
        idxs.append(fi.reshape(1, tm))
        wts.append(jnp.sum(jnp.sum(jnp.where(pick, s3, 0.0), axis=0, keepdims=True), axis=1,
                           keepdims=True).reshape(1, tm))
        sel = sel | pick
        cand = jnp.where(pick, neg, cand)
    wsum = wts[0]
    for k in range(1, TOP_K):
        wsum = wsum + wts[k]
    idx_ref[...] = jnp.concatenate(idxs, axis=0)
    wt_ref[...] = jnp.concatenate([wk / wsum * ROUTED_SCALE for wk in wts], axis=0)
    sel2 = jnp.where(sel, 1.0, 0.0).reshape(ne, tm)
    prefix = (_dot(sel2.astype(bf16), u_ref[...]) + carry_scr[...]).reshape(ng, pg, tm)
    ranks = [jnp.sum(jnp.sum(jnp.where(pk, prefix, 0.0), axis=0, keepdims=True), axis=1,
                     keepdims=True).reshape(1, tm) for pk in picks]
    rank_ref[...] = jnp.concatenate(ranks, axis=0).astype(i32)
    total = carry_scr[...] + jnp.sum(sel2, axis=1, keepdims=True)
    carry_scr[...] = total
    cnt_ref[...] = total


def _router(x, g, mod, which, w_router, b_router, dims, n_rows):
    d = x.shape[1]
    tm = ROW_BLOCK
    ne = N_EXPERTS
    nlb, bpb, nb = dims
    mrow = lambda i: _mod_row(i, nlb, bpb, nb)
    tri = (jnp.arange(tm)[:, None] < jnp.arange(tm)[None, :]).astype(bf16)
    return pl.pallas_call(
        _router_kernel,
        grid=(n_rows // tm,),
        in_specs=[pl.BlockSpec((tm, d), lambda i: (i, 0)),
                  pl.BlockSpec((1, d), lambda i: (0, 0)),
                  pl.BlockSpec((None, 1, d), lambda i: (mrow(i) * 6 + which + 1, 0, 0)),
                  pl.BlockSpec((None, 1, d), lambda i: (mrow(i) * 6 + which, 0, 0)),
                  pl.BlockSpec((ne, d), lambda i: (0, 0)),
                  pl.BlockSpec((ne, 1), lambda i: (0, 0)),
                  pl.BlockSpec((tm, tm), lambda i: (0, 0))],
        out_specs=[pl.BlockSpec((tm, d), lambda i: (i, 0)),
                   pl.BlockSpec((TOP_K, tm), lambda i: (0, i)),
                   pl.BlockSpec((TOP_K, tm), lambda i: (0, i)),
                   pl.BlockSpec((TOP_K, tm), lambda i: (0, i)),
                   pl.BlockSpec((ne, 1), lambda i: (0, 0))],
        out_shape=[jax.ShapeDtypeStruct((n_rows, d), f32),
                   jax.ShapeDtypeStruct((TOP_K, n_rows), i32),
                   jax.ShapeDtypeStruct((TOP_K, n_rows), f32),
                   jax.ShapeDtypeStruct((TOP_K, n_rows), i32),
                   jax.ShapeDtypeStruct((ne, 1), f32)],
        scratch_shapes=[pltpu.VMEM((ne, 1), f32)],
        compiler_params=_params(("arbitrary",)),
    )(x, g.reshape(1, d), mod, mod, w_router.T, b_router.reshape(ne, 1).astype(f32), tri)


def _expert_kernel(be_ref, nu_ref, tok_ref, tokn_ref, h_hbm, wg_ref, wu_ref, wd_ref, y_ref, xbuf, sem):
    i = pl.program_id(0)
    n_used = nu_ref[0]
    slot = i % 2
    mb = MOE_BLOCK

    def start(idx_ref, dst_slot):
        def body(r, carry):
            pltpu.make_async_copy(h_hbm.at[pl.ds(idx_ref[0, 0, r], 1)], xbuf.at[dst_slot, pl.ds(r, 1)],
                                  sem.at[dst_slot]).start()
            return carry
        lax.fori_loop(0, mb, body, 0)

    @pl.when((i == 0) & (n_used > 0))
    def _():
        start(tok_ref, 0)

    @pl.when(i + 1 < n_used)
    def _():
        start(tokn_ref, 1 - slot)

    @pl.when(i < n_used)
    def _():
        pltpu.make_async_copy(h_hbm.at[pl.ds(0, mb)], xbuf.at[slot], sem.at[slot]).wait()
        x = xbuf[slot].astype(bf16)
        hid = _silu(_dot(x, wg_ref[...])) * _dot(x, wu_ref[...])
        y_ref[...] = _dot(hid.astype(bf16), wd_ref[...])

    @pl.when(i >= n_used)
    def _():
        y_ref[...] = jnp.zeros_like(y_ref)


def _experts(h, slot_tok, block_e, n_used, wg, wu, wd):
    d = h.shape[1]
    nblk = slot_tok.shape[0]
    de = wg.shape[2]
    mb = MOE_BLOCK
    grid_spec = pltpu.PrefetchScalarGridSpec(
        num_scalar_prefetch=2,
        grid=(nblk,),
        in_specs=[pl.BlockSpec((1, 1, mb), lambda i, be, nu: (i, 0, 0), memory_space=pltpu.SMEM),
                  pl.BlockSpec((1, 1, mb), lambda i, be, nu: (jnp.minimum(i + 1, nblk - 1), 0, 0),
                               memory_space=pltpu.SMEM),
                  pl.BlockSpec(memory_space=pl.ANY),
                  pl.BlockSpec((None, d, de), lambda i, be, nu: (be[i], 0, 0)),
                  pl.BlockSpec((None, d, de), lambda i, be, nu: (be[i], 0, 0)),
                  pl.BlockSpec((None, de, d), lambda i, be, nu: (be[i], 0, 0))],
        out_specs=pl.BlockSpec((mb, d), lambda i, be, nu: (i, 0)),
        scratch_shapes=[pltpu.VMEM((2, mb, d), f32), pltpu.SemaphoreType.DMA((2,))],
    )
    st = slot_tok.reshape(nblk, 1, mb)
    return pl.pallas_call(
        _expert_kernel,
        grid_spec=grid_spec,
        out_shape=jax.ShapeDtypeStruct((nblk * mb, d), f32),
        compiler_params=_params(("arbitrary",)),
    )(block_e, n_used, st, st, h, wg, wu, wd)


def _shared_kernel(h_ref, wg_ref, wu_ref, wd_ref, y_ref):
    x = h_ref[...].astype(bf16)
    hid = _silu(_dot(x, wg_ref[...])) * _dot(x, wu_ref[...])
    y_ref[...] = _dot(hid.astype(bf16), wd_ref[...])


def _shared(h, wg, wu, wd):
    t, d = h.shape
    de = wg.shape[1]
    tm = ROW_BLOCK
    return pl.pallas_call(
        _shared_kernel,
        grid=(t // tm,),
        in_specs=[pl.BlockSpec((tm, d), lambda i: (i, 0)),
                  pl.BlockSpec((d, de), lambda i: (0, 0)),
                  pl.BlockSpec((d, de), lambda i: (0, 0)),
                  pl.BlockSpec((de, d), lambda i: (0, 0))],
        out_specs=pl.BlockSpec((tm, d), lambda i: (i, 0)),
        out_shape=jax.ShapeDtypeStruct((t, d), f32),
        compiler_params=_params(("parallel",)),
    )(h, wg, wu, wd)


COMBINE_ROWS = 64


def _combine_kernel(dst_ref, dstn_ref, y_hbm, wt_ref, sh_ref, x_ref, gate_ref, gf_ref, o_ref, buf, sem,
                    *, final_norm):
    i = pl.program_id(0)
    nsteps = pl.num_programs(0)
    slot = i % 2
    tm = COMBINE_ROWS

    def start(idx_ref, dst_slot):
        for k in range(TOP_K):
            def body(r, carry, k=k):
                pltpu.make_async_copy(y_hbm.at[pl.ds(idx_ref[0, k, r], 1)], buf.at[dst_slot, k, pl.ds(r, 1)],
                                      sem.at[dst_slot]).start()
                return carry
            lax.fori_loop(0, tm, body, 0)

    @pl.when(i == 0)
    def _():
        start(dst_ref, 0)

    @pl.when(i + 1 < nsteps)
    def _():
        start(dstn_ref, 1 - slot)

    for k in range(TOP_K):
        pltpu.make_async_copy(y_hbm.at[pl.ds(0, tm)], buf.at[slot, k], sem.at[slot]).wait()
    acc = sh_ref[...]
    wt = wt_ref[...]
    for k in range(TOP_K):
        acc = acc + wt[:, k:k + 1] * buf[slot, k]
    out = x_ref[...] + gate_ref[...] * acc
    if final_norm:
        out = out * lax.rsqrt(jnp.mean(out * out, axis=-1, keepdims=True) + EPS) * gf_ref[...]
    o_ref[...] = out


def _combine(y_slots, dest, wts, shared, x, mod, which, g_final, dims, n_rows, final_norm):
    d = x.shape[1]
    tm = COMBINE_ROWS
    nsteps = n_rows // tm
    per = ROW_BLOCK // tm
    nlb, bpb, nb = dims
    mrow = lambda i: _mod_row(i // per, nlb, bpb, nb)
    dst = dest.reshape(TOP_K, nsteps, tm).transpose(1, 0, 2)
    wt = wts.T
    return pl.pallas_call(
        functools.partial(_combine_kernel, final_norm=final_norm),
        grid=(nsteps,),
        in_specs=[pl.BlockSpec((1, TOP_K, tm), lambda i: (i, 0, 0), memory_space=pltpu.SMEM),
                  pl.BlockSpec((1, TOP_K, tm), lambda i: (jnp.minimum(i + 1, nsteps - 1), 0, 0),
                               memory_space=pltpu.SMEM),
                  pl.BlockSpec(memory_space=pl.ANY),
                  pl.BlockSpec((tm, TOP_K), lambda i: (i, 0)),
                  pl.BlockSpec((tm, d), lambda i: (i, 0)),
                  pl.BlockSpec((tm, d), lambda i: (i, 0)),
                  pl.BlockSpec((None, 1, d), lambda i: (mrow(i) * 6 + which, 0, 0)),
                  pl.BlockSpec((1, d), lambda i: (0, 0))],
        out_specs=pl.BlockSpec((tm, d), lambda i: (i, 0)),
        out_shape=jax.ShapeDtypeStruct((n_rows, d), f32),
        scratch_shapes=[pltpu.VMEM((2, TOP_K, tm, d), f32), pltpu.SemaphoreType.DMA((2,))],
        compiler_params=_params(("arbitrary",)),
    )(dst, dst, y_slots, wt, shared, x, mod, g_final.reshape(1, d))


def _moe(x, g, mod, which_shift, w_router, b_router, wg, wu, wd, sg, su, sd, g_final, dims, n_rows, final_norm):
    ne = N_EXPERTS
    mb = MOE_BLOCK
    h, idx, wts, rank, cnt = _router(x, g, mod, which_shift, w_router, b_router, dims, n_rows)
    counts = cnt[:, 0].astype(i32)
    padded = (counts + mb - 1) // mb * mb
    pend = jnp.cumsum(padded)
    dest = (pend - padded)[idx] + rank
    n_assign = n_rows * TOP_K
    nblk = (n_assign + ne * (mb - 1) + mb - 1) // mb
    tok = jnp.broadcast_to(jnp.arange(n_rows, dtype=i32)[None, :], (TOP_K, n_rows))
    slot_tok = jnp.zeros((nblk * mb,), i32).at[dest.reshape(-1)].set(tok.reshape(-1))
    blk_start = jnp.arange(nblk, dtype=i32) * mb
    block_e = jnp.minimum(jnp.sum((pend[None, :] <= blk_start[:, None]).astype(i32), axis=1), ne - 1)
    n_used = (pend[-1:] // mb).astype(i32)
    y_slots = _experts(h, slot_tok.reshape(nblk, mb), block_e, n_used, wg, wu, wd)
    shared = _shared(h, sg, su, sd)
    return _combine(y_slots, dest, wts, shared, x, mod, which_shift + 2, g_final, dims, n_rows, final_norm)


def _rope_tables(seq, head_dim):
    half = head_dim // 2
    quarter = half // 2
    t = jnp.arange(seq, dtype=i32)
    rows = (t // GRID_W).astype(f32)
    cols = (t % GRID_W).astype(f32)
    freqs = ROPE_THETA ** (-jnp.arange(0, half, 2, dtype=f32) / half)
    ar = rows[:, None] * freqs[None, :]
    ac = cols[:, None] * freqs[None, :]
    cos = jnp.concatenate([jnp.cos(ar), jnp.cos(ar), jnp.cos(ac), jnp.cos(ac)], axis=1)
    sin = jnp.concatenate([-jnp.sin(ar), jnp.sin(ar), -jnp.sin(ac), jnp.sin(ac)], axis=1)
    reps = LANE // head_dim
    cos = jnp.tile(cos, (1, reps))
    sin = jnp.tile(sin, (1, reps))
    cos = jnp.concatenate([cos, jnp.ones((ROW_BLOCK, LANE), f32)], axis=0)
    sin = jnp.concatenate([sin, jnp.zeros((ROW_BLOCK, LANE), f32)], axis=0)
    return cos, sin, quarter


def kernel(x, c, ctx, c_ctx, w_ada, b_ada, g_norm_mix, g_norm_ffn, w_in_even, w_out_even, hgrn_lb_logits,
           g_hgrn_norm, win_sink, w_in_odd, w_out_odd, g_q_norm, g_k_norm, w_router, b_router,
           w_exp_gate, w_exp_up, w_exp_down, w_sh_gate, w_sh_up, w_sh_down, g_norm_final):
    nb, seq, d = x.shape
    lc = ctx.shape[1]
    assert seq % ROW_BLOCK == 0 and lc % ROW_BLOCK == 0 and seq % lc == 0
    t_lat = nb * seq
    t_all = t_lat + nb * lc
    bpb = seq // ROW_BLOCK
    nlb = t_lat // ROW_BLOCK
    dims = (nlb, bpb, nb)

    xs = jnp.concatenate([x.reshape(t_lat, d), ctx.reshape(nb * lc, d)], axis=0)
    cc = jnp.concatenate([c, c_ctx[None, :], jnp.zeros((8 - nb - 1, d), f32)], axis=0)

    lb_w = jax.nn.softmax(hgrn_lb_logits.astype(f32), axis=0)
    lower_bounds = jnp.cumsum(lb_w, axis=0)[1:] - lb_w[0]

    mod = _ada(cc, w_ada[0], b_ada[0]).reshape(8 * 6, 1, d)
    w_in = w_in_even[0].astype(bf16)
    proj = _normmm(xs, g_norm_mix[0], mod, 0, w_in[:, :HGRN_COLS], f32, dims, tn=1280)
    cos, sin, quarter = _rope_tables(seq, WIN_HEAD_DIM)
    epi = dict(blocks={0: (WIN_HEADS + WIN_KV_HEADS) * WIN_HEAD_DIM // LANE}, quarter=quarter, head_norm=False,
               q_slices=WIN_HEADS * WIN_HEAD_DIM // LANE, q_scale=WIN_HEAD_DIM ** -0.5)
    wproj = _normmm(xs, g_norm_mix[0], mod, 0, w_in[:, HGRN_COLS:], bf16, dims, tn=WIN_COLS,
                    epilogue=epi, tables=(cos, sin))
    o_f, o_b = _hgrn(proj, lower_bounds[0], nb, seq, lc)
    a_mix = _hgrn_out(o_f, o_b, proj, g_hgrn_norm[0])
    b_mix = _win_attn(wproj, win_sink[0], nb, seq, lc)
    xs = _outproj(a_mix, 0, b_mix, 0, w_out_even[0].astype(bf16), xs, mod, 2, dims, t_all)
    xs = _moe(xs, g_norm_ffn[0], mod, 3, w_router[0], b_router[0],
              w_exp_gate[0].astype(bf16), w_exp_up[0].astype(bf16), w_exp_down[0].astype(bf16),
              w_sh_gate[0].astype(bf16), w_sh_up[0].astype(bf16), w_sh_down[0].astype(bf16),
              g_norm_final, dims, t_all, False)

    mod = _ada(cc, w_ada[1], b_ada[1]).reshape(8 * 6, 1, d)
    cos, sin, quarter = _rope_tables(seq, GLB_HEAD_DIM)
    tn = 1536
    n_sl = tn // LANE
    rope_slices = GLB_HEADS + GLB_KV_HEADS
    epi = dict(blocks={0: min(rope_slices, n_sl), 1: max(rope_slices - n_sl, 0)}, quarter=quarter, head_norm=True,
               q_slices=GLB_HEADS, q_scale=GLB_HEAD_DIM ** -0.5)
    qkv = _normmm(xs, g_norm_mix[1], mod, 0, w_in_odd[0].astype(bf16), bf16, dims, tn=tn,
                  epilogue=epi, tables=(cos, sin), gains=(g_q_norm[0], g_k_norm[0]))
    att = _glb_attn(qkv, nb, seq, lc)
    xl = _outproj(att, 0, att, 1, w_out_odd[0].astype(bf16), xs, mod, 2, dims, t_lat)
    out = _moe(xl, g_norm_ffn[1], mod, 3, w_router[1], b_router[1],
               w_exp_gate[1].astype(bf16), w_exp_up[1].astype(bf16), w_exp_down[1].astype(bf16),
               w_sh_gate[1].astype(bf16), w_sh_up[1].astype(bf16), w_sh_down[1].astype(bf16),
               g_norm_final, dims, t_lat, True)
    return out.reshape(nb, seq, d)
```

```python
import functools

import jax
import jax.numpy as jnp
from jax import lax
from jax.experimental import pallas as pl
from jax.experimental.pallas import tpu as pltpu

f32 = jnp.float32
bf16 = jnp.bfloat16
i32 = jnp.int32

EPS = 1e-6
ROPE_THETA = 10000.0
GRID_W = 64

HGRN_DK = 128
HGRN_HEADS = 8
HGRN_CHUNK = 64
HGRN_SUB = 16
HGRN_HEADS_PER_STEP = 2
WIN_HEAD_DIM = 64
WIN_HEADS = 16
WIN_KV_HEADS = 2
WIN_BLOCK = 128
GLB_HEAD_DIM = 128
GLB_HEADS = 16
GLB_KV_HEADS = 4
Q_BLOCK = 128
N_EXPERTS = 64
N_GROUPS = 8
TOPK_GROUPS = 4
TOP_K = 8
ROUTED_SCALE = 2.5
MOE_BLOCK = 256

LANE = 128
ROW_BLOCK = 256
VMEM_LIMIT = 56 * 1024 * 1024

A_K = HGRN_HEADS * HGRN_DK
HGRN_COLS = 5 * A_K
WIN_COLS = WIN_HEADS * WIN_HEAD_DIM + 2 * WIN_KV_HEADS * WIN_HEAD_DIM


def _params(sem, vmem=VMEM_LIMIT):
    return pltpu.CompilerParams(dimension_semantics=sem, vmem_limit_bytes=vmem)


def _silu(x):
    return x * jax.nn.sigmoid(x)


def _dot(a, b):
    return jnp.dot(a, b, preferred_element_type=f32)


def _dot_nt(a, b):
    return lax.dot_general(a, b, (((1,), (1,)), ((), ())), preferred_element_type=f32)


def _dot_tn(a, b):
    return lax.dot_general(a, b, (((0,), (0,)), ((), ())), preferred_element_type=f32)


def _ada_kernel(c_ref, w_ref, b_ref, o_ref):
    a = _silu(c_ref[...]).astype(bf16)
    o_ref[...] = _dot(a, w_ref[...].astype(bf16)) + b_ref[...]


def _ada(c8, w, b, layer, tn=768):
    m, d = c8.shape
    n = w.shape[2]
    return pl.pallas_call(
        _ada_kernel,
        grid=(n // tn,),
        in_specs=[pl.BlockSpec((m, d), lambda j: (0, 0)),
                  pl.BlockSpec((None, d, tn), lambda j: (layer, 0, j)),
                  pl.BlockSpec((None, 1, tn), lambda j: (layer, 0, j))],
        out_specs=pl.BlockSpec((m, tn), lambda j: (0, j)),
        out_shape=jax.ShapeDtypeStruct((m, n), f32),
        compiler_params=_params(("arbitrary",)),
    )(c8, w, b.reshape(b.shape[0], 1, n))


def _norm_mod(x, g, sc, sh):
    y = x * lax.rsqrt(jnp.mean(x * x, axis=-1, keepdims=True) + EPS) * g
    return y * (1.0 + sc) + sh


def _rope_slice(x, cos, sin, quarter):
    lane = lax.broadcasted_iota(i32, x.shape, 1)
    up = pltpu.roll(x, LANE - quarter, axis=1)
    dn = pltpu.roll(x, quarter, axis=1)
    partner = jnp.where(lane % (2 * quarter) < quarter, up, dn)
    return x * cos + partner * sin


def _normmm_kernel(x_ref, g_ref, sc_ref, sh_ref, w_ref, *rest, epilogue, tn):
    if epilogue is None:
        o_ref, h_scr = rest
    elif epilogue["head_norm"]:
        cos_ref, sin_ref, gq_ref, gk_ref, o_ref, h_scr = rest
    else:
        cos_ref, sin_ref, o_ref, h_scr = rest
    j = pl.program_id(1)

    @pl.when(j == 0)
    def _():
        h_scr[...] = _norm_mod(x_ref[...], g_ref[...], sc_ref[...], sh_ref[...]).astype(bf16)

    y = _dot(h_scr[...], w_ref[...])
    if epilogue is None:
        o_ref[...] = y.astype(o_ref.dtype)
        return

    n_sl = tn // LANE
    for jb in range(epilogue["n_col_blocks"]):
        n_rope = epilogue["blocks"].get(jb, 0)

        @pl.when(j == jb)
        def _(jb=jb, n_rope=n_rope):
            cos = cos_ref[...]
            sin = sin_ref[...]
            outs = []
            for s in range(n_sl):
                ys = y[:, s * LANE:(s + 1) * LANE]
                if s < n_rope:
                    col = jb * n_sl + s
                    is_q = col < epilogue["q_slices"]
                    if epilogue["head_norm"]:
                        gain = gq_ref[...] if is_q else gk_ref[...]
                        ys = ys * lax.rsqrt(jnp.mean(ys * ys, axis=-1, keepdims=True) + EPS) * gain
                    ys = _rope_slice(ys, cos, sin, epilogue["quarter"])
                    if is_q:
                        ys = ys * epilogue["q_scale"]
                outs.append(ys.astype(o_ref.dtype))
            o_ref[...] = jnp.concatenate(outs, axis=1)


def _mod_row(i, n_lat_blocks, blocks_per_batch, n_batch):
    return jnp.where(i < n_lat_blocks, i // blocks_per_batch, n_batch)


def _normmm(x, g, mod, which, w, out_dtype, dims, tn, epilogue=None, tables=None, gains=None):
    t, d = x.shape
    n = w.shape[1]
    tm = ROW_BLOCK
    nlb, bpb, nb = dims
    mrow = lambda i: _mod_row(i, nlb, bpb, nb)
    in_specs = [pl.BlockSpec((tm, d), lambda i, j: (i, 0)),
                pl.BlockSpec((1, d), lambda i, j: (0, 0)),
                pl.BlockSpec((None, 1, d), lambda i, j: (mrow(i) * 6 + which + 1, 0, 0)),
                pl.BlockSpec((None, 1, d), lambda i, j: (mrow(i) * 6 + which, 0, 0)),
                pl.BlockSpec((d, tn), lambda i, j: (0, j))]
    args = [x, g.reshape(1, d), mod, mod, w]
    if epilogue is not None:
        cos, sin = tables
        tab = lambda i, j: (jnp.where(i < nlb, i % bpb, bpb), 0)
        in_specs += [pl.BlockSpec((tm, LANE), tab), pl.BlockSpec((tm, LANE), tab)]
        args += [cos, sin]
        epilogue = dict(epilogue, n_col_blocks=n // tn)
        if epilogue["head_norm"]:
            in_specs += [pl.BlockSpec((1, LANE), lambda i, j: (0, 0))] * 2
            args += [gains[0].reshape(1, LANE), gains[1].reshape(1, LANE)]
    return pl.pallas_call(
        functools.partial(_normmm_kernel, epilogue=epilogue, tn=tn),
        grid=(t // tm, n // tn),
        in_specs=in_specs,
        out_specs=pl.BlockSpec((tm, tn), lambda i, j: (i, j)),
        out_shape=jax.ShapeDtypeStruct((t, n), out_dtype),
        scratch_shapes=[pltpu.VMEM((tm, d), bf16)],
        compiler_params=_params(("parallel", "arbitrary")),
    )(*args)


def _gla_chunk(q_raw, v, f_raw, lb, st_ref, a_scr, rev):
    c = HGRN_CHUNK
    sub = HGRN_SUB
    q = _silu(q_raw)
    f = lb + (1.0 - lb) * jax.nn.sigmoid(f_raw)
    kk = 1.0 - f
    g = jnp.log(f)
    row = lax.broadcasted_iota(i32, (c, c), 0)
    col = lax.broadcasted_iota(i32, (c, c), 1)
    tri = (col >= row) if rev else (col <= row)
    b = jnp.dot(tri.astype(f32), g, preferred_element_type=f32, precision=lax.Precision.HIGHEST)
    b_end = b[0:1] if rev else b[c - 1:c]
    st = st_ref[...]
    o = _dot_nt((q * jnp.exp(b)).astype(bf16), st.astype(bf16))

    vb = v.astype(bf16)
    nsub = c // sub
    lane16 = lax.broadcasted_iota(i32, (sub, sub), 1)
    row16 = lax.broadcasted_iota(i32, (sub, sub), 0)
    diag_ok = (row16 <= lane16) if rev else (row16 >= lane16)
    rsub = lax.broadcasted_iota(i32, (c, sub), 0) // sub
    for jb in range(nsub):
        js = slice(jb * sub, (jb + 1) * sub)
        m_j = b[jb * sub:jb * sub + 1] if rev else b[(jb + 1) * sub - 1:(jb + 1) * sub]
        kd = kk[js] * jnp.exp(m_j - b[js])
        qd = q * jnp.exp(jnp.minimum(b - m_j, 0.0))
        a_col = _dot_nt(qd.astype(bf16), kd.astype(bf16))
        off_ok = (rsub < jb) if rev else (rsub > jb)
        a_col = jnp.where(off_ok, a_col, 0.0)
        qi, bi, ki = q[js], b[js], kk[js]
        a_dd = jnp.zeros((sub, sub), f32)
        for s in range(sub):
            e = jnp.exp(jnp.minimum(bi - bi[s:s + 1], 0.0))
            col_s = jnp.sum(qi * ki[s:s + 1] * e, axis=-1, keepdims=True)
            a_dd = jnp.where(lane16 == s, col_s, a_dd)
        a_dd = jnp.where(diag_ok, a_dd, 0.0)
        pieces = [a_dd if ib == jb else a_col[ib * sub:(ib + 1) * sub] for ib in range(nsub)]
        a_scr[:, js] = jnp.concatenate(pieces, axis=0)
    o = o + _dot(a_scr[...].astype(bf16), vb)
    kdec = kk * jnp.exp(b_end - b)
    st_ref[...] = jnp.exp(b_end) * st + _dot_tn(vb, kdec.astype(bf16))
    return o


def _hgrn_kernel(qf_ref, vf_ref, ff_ref, qb_ref, vb_ref, fb_ref, lb_ref, of_ref, ob_ref,
                 stf_ref, stb_ref, af_scr, ab_scr):
    @pl.when(pl.program_id(2) == 0)
    def _():
        stf_ref[...] = jnp.zeros_like(stf_ref)
        stb_ref[...] = jnp.zeros_like(stb_ref)

    for j in range(HGRN_HEADS_PER_STEP):
        sl = slice(j * HGRN_DK, (j + 1) * HGRN_DK)
        lb = lb_ref[:, sl]
        of_ref[:, sl] = _gla_chunk(qf_ref[:, sl], vf_ref[:, sl], ff_ref[:, sl], lb, stf_ref.at[j], af_scr.at[j], False)
        ob_ref[:, sl] = _gla_chunk(qb_ref[:, sl], vb_ref[:, sl], fb_ref[:, sl], lb, stb_ref.at[j], ab_scr.at[j], True)


def _hgrn(proj, lb, n_batch, seq, ctx_len):
    t = proj.shape[0]
    c = HGRN_CHUNK
    hp = HGRN_HEADS_PER_STEP
    wblk = hp * HGRN_DK
    ngrp = HGRN_HEADS // hp
    ncc, ncl = ctx_len // c, seq // c
    lat_chunks = n_batch * ncl

    def fwd_blk(b, s):
        return jnp.where(s < ncc, lat_chunks + b * ncc + s, b * ncl + (s - ncc))

    def bwd_blk(b, s):
        return jnp.where(s < ncc, lat_chunks + b * ncc + (ncc - 1 - s), b * ncl + (ncl - 1 - (s - ncc)))

    def spec(blk, section):
        return pl.BlockSpec((c, wblk), lambda b, h, s: (blk(b, s), section * ngrp + h))

    def ospec(blk):
        return pl.BlockSpec((c, wblk), lambda b, h, s: (blk(b, s), h))

    out = jax.ShapeDtypeStruct((t, A_K), f32)
    return pl.pallas_call(
        _hgrn_kernel,
        grid=(n_batch, ngrp, ncc + ncl),
        in_specs=[spec(fwd_blk, 0), spec(fwd_blk, 1), spec(fwd_blk, 2),
                  spec(bwd_blk, 0), spec(bwd_blk, 1), spec(bwd_blk, 3),
                  pl.BlockSpec((1, wblk), lambda b, h, s: (0, h))],
        out_specs=[ospec(fwd_blk), ospec(bwd_blk)],
        out_shape=[out, out],
        scratch_shapes=[pltpu.VMEM((hp, HGRN_DK, HGRN_DK), f32), pltpu.VMEM((hp, HGRN_DK, HGRN_DK), f32),
                        pltpu.VMEM((hp, c, c), f32), pltpu.VMEM((hp, c, c), f32)],
        compiler_params=_params(("parallel", "parallel", "arbitrary")),
    )(proj, proj, proj, proj, proj, proj, lb.reshape(1, A_K))


def _hgrn_out_kernel(of_ref, ob_ref, g_ref, gain_ref, o_ref):
    gain = gain_ref[...]
    outs = []
    for h in range(HGRN_HEADS):
        sl = slice(h * HGRN_DK, (h + 1) * HGRN_DK)
        o = of_ref[:, sl] + ob_ref[:, sl]
        o = o * lax.rsqrt(jnp.mean(o * o, axis=-1, keepdims=True) + EPS) * gain
        outs.append((o * _silu(g_ref[:, sl])).astype(bf16))
    o_ref[...] = jnp.concatenate(outs, axis=1)


def _hgrn_out(o_f, o_b, proj, gain):
    t = o_f.shape[0]
    tm = ROW_BLOCK
    return pl.pallas_call(
        _hgrn_out_kernel,
        grid=(t // tm,),
        in_specs=[pl.BlockSpec((tm, A_K), lambda i: (i, 0)),
                  pl.BlockSpec((tm, A_K), lambda i: (i, 0)),
                  pl.BlockSpec((tm, A_K), lambda i: (i, 4)),
                  pl.BlockSpec((1, HGRN_DK), lambda i: (0, 0))],
        out_specs=pl.BlockSpec((tm, A_K), lambda i: (i, 0)),
        out_shape=jax.ShapeDtypeStruct((t, A_K), bf16),
        compiler_params=_params(("parallel",)),
    )(o_f, o_b, proj, gain.reshape(1, HGRN_DK))


def _win_kernel(sink_ref, q_ref, kp_ref, kc_ref, kn_ref, vp_ref, vc_ref, vn_ref, kx_ref, vx_ref, o_ref,
                *, n_lat_blocks):
    n = pl.program_id(1)
    w = WIN_BLOCK
    dh = WIN_HEAD_DIM
    is_lat = n < n_lat_blocks
    ri = lax.broadcasted_iota(i32, (w, w), 0)
    ci = lax.broadcasted_iota(i32, (w, w), 1)
    ok_p = (ci >= ri) & is_lat & (n > 0)
    ok_c = jnp.broadcast_to(is_lat, (w, w))
    ok_n = (ci <= ri) & is_lat & (n < n_lat_blocks - 1)
    valid = jnp.concatenate([ok_p, ok_c, ok_n], axis=1)
    kwin = jnp.concatenate([kp_ref[...], kc_ref[...], kn_ref[...]], axis=0)
    vwin = jnp.concatenate([vp_ref[...], vc_ref[...], vn_ref[...]], axis=0)
    kx = kx_ref[...]
    vx = vx_ref[...]
    g = WIN_HEADS // WIN_KV_HEADS
    outs = []
    for h in range(WIN_HEADS):
        kv = h // g
        ks = slice(kv * dh, (kv + 1) * dh)
        qh = q_ref[:, h * dh:(h + 1) * dh]
        sw = jnp.where(valid, _dot_nt(qh, kwin[:, ks]), -jnp.inf)
        sx = _dot_nt(qh, kx[:, ks])
        sk = sink_ref[h]
        m = jnp.maximum(jnp.maximum(jnp.max(sw, axis=-1, keepdims=True), jnp.max(sx, axis=-1, keepdims=True)), sk)
        pw = jnp.exp(sw - m)
        px = jnp.exp(sx - m)
        den = jnp.sum(pw, axis=-1, keepdims=True) + jnp.sum(px, axis=-1, keepdims=True) + jnp.exp(sk - m)
        o = _dot(pw.astype(bf16), vwin[:, ks]) + _dot(px.astype(bf16), vx[:, ks])
        outs.append((o / den).astype(bf16))
    o_ref[...] = jnp.concatenate(outs, axis=1)


def _win_attn(wproj, sink, n_batch, seq, ctx_len):
    t = wproj.shape[0]
    w = WIN_BLOCK
    nlb = seq // w
    ncb = ctx_len // w
    lat_blocks = n_batch * nlb
    qc = WIN_HEADS * WIN_HEAD_DIM // LANE
    kcol, vcol = qc, qc + 1

    def qrow(b, n):
        return jnp.where(n < nlb, b * nlb + n, lat_blocks + b * ncb + (n - nlb))

    def krow(off):
        def f(b, n):
            return b * nlb + jnp.clip(n + off, 0, nlb - 1)
        return f

    def kspec(off, colblk):
        return pl.BlockSpec((w, LANE), lambda b, n: (krow(off)(b, n), colblk))

    def xspec(colblk):
        return pl.BlockSpec((ctx_len, LANE), lambda b, n: ((n_batch * seq) // ctx_len + b, colblk))

    return pl.pallas_call(
        functools.partial(_win_kernel, n_lat_blocks=nlb),
        grid=(n_batch, nlb + ncb),
        in_specs=[pl.BlockSpec(memory_space=pltpu.SMEM),
                  pl.BlockSpec((w, qc * LANE), lambda b, n: (qrow(b, n), 0)),
                  kspec(-1, kcol), kspec(0, kcol), kspec(1, kcol),
                  kspec(-1, vcol), kspec(0, vcol), kspec(1, vcol),
                  xspec(kcol), xspec(vcol)],
        out_specs=pl.BlockSpec((w, qc * LANE), lambda b, n: (qrow(b, n), 0)),
        out_shape=jax.ShapeDtypeStruct((t, qc * LANE), bf16),
        compiler_params=_params(("parallel", "arbitrary")),
    )(sink.astype(f32), wproj, wproj, wproj, wproj, wproj, wproj, wproj, wproj, wproj)


GLB_KEY_CHUNK = 512


def _glb_kernel(q_ref, kx_ref, kl_ref, vx_ref, vl_ref, o_ref, m_scr, l_scr, acc_scr):
    dh = GLB_HEAD_DIM
    g = GLB_HEADS // GLB_KV_HEADS
    ck = GLB_KEY_CHUNK
    tq = q_ref.shape[0]
    q = jnp.concatenate([q_ref[:, i * dh:(i + 1) * dh] for i in range(g)], axis=0)

    def update(k, v, first):
        s = _dot_nt(q, k)
        reps = s.shape[1] // LANE
        mx = jnp.broadcast_to(jnp.max(s, axis=-1, keepdims=True), (g * tq, LANE))
        if first:
            m_new = mx
            p = jnp.exp(s - jnp.concatenate([m_new] * reps, axis=1))
            l_scr[...] = jnp.broadcast_to(jnp.sum(p, axis=-1, keepdims=True), (g * tq, LANE))
            acc_scr[...] = _dot(p.astype(bf16), v)
        else:
            m_old = m_scr[...]
            m_new = jnp.maximum(m_old, mx)
            alpha = jnp.exp(m_old - m_new)
            p = jnp.exp(s - jnp.concatenate([m_new] * reps, axis=1))
            l_scr[...] = alpha * l_scr[...] + jnp.broadcast_to(jnp.sum(p, axis=-1, keepdims=True), (g * tq, LANE))
            acc_scr[...] = alpha * acc_scr[...] + _dot(p.astype(bf16), v)
        m_scr[...] = m_new

    chunks = [(kx_ref, vx_ref, c, min(ck, kx_ref.shape[0] - c)) for c in range(0, kx_ref.shape[0], ck)]
    chunks += [(kl_ref, vl_ref, c, min(ck, kl_ref.shape[0] - c)) for c in range(0, kl_ref.shape[0], ck)]
    for n, (kr, vr, c0, sz) in enumerate(chunks):
        update(kr[c0:c0 + sz, :], vr[c0:c0 + sz, :], n == 0)
    o = acc_scr[...] / l_scr[...]
    o_ref[...] = jnp.concatenate([o[i * tq:(i + 1) * tq] for i in range(g)], axis=1).astype(bf16)


def _glb_attn(qkv, n_batch, seq, ctx_len):
    tq = Q_BLOCK
    dh = GLB_HEAD_DIM
    g = GLB_HEADS // GLB_KV_HEADS
    nq = seq // tq
    kcol0 = GLB_HEADS
    vcol0 = GLB_HEADS + GLB_KV_HEADS
    ctx_blk0 = (n_batch * seq) // ctx_len
    return pl.pallas_call(
        _glb_kernel,
        grid=(n_batch, GLB_KV_HEADS, nq),
        in_specs=[pl.BlockSpec((tq, g * dh), lambda b, k, i: (b * nq + i, k)),
                  pl.BlockSpec((ctx_len, dh), lambda b, k, i: (ctx_blk0 + b, kcol0 + k)),
                  pl.BlockSpec((seq, dh), lambda b, k, i: (b, kcol0 + k)),
                  pl.BlockSpec((ctx_len, dh), lambda b, k, i: (ctx_blk0 + b, vcol0 + k)),
                  pl.BlockSpec((seq, dh), lambda b, k, i: (b, vcol0 + k))],
        out_specs=pl.BlockSpec((tq, g * dh), lambda b, k, i: (b * nq + i, k)),
        out_shape=jax.ShapeDtypeStruct((n_batch * seq, GLB_HEADS * dh), bf16),
        scratch_shapes=[pltpu.VMEM((g * tq, LANE), f32), pltpu.VMEM((g * tq, LANE), f32), pltpu.VMEM((g * tq, dh), f32)],
        compiler_params=_params(("parallel", "parallel", "arbitrary")),
    )(qkv, qkv, qkv, qkv, qkv)


def _outproj_kernel(a1_ref, a2_ref, w1_ref, w2_ref, x_ref, gate_ref, o_ref):
    y = _dot(a1_ref[...], w1_ref[...]) + _dot(a2_ref[...], w2_ref[...])
    o_ref[...] = x_ref[...] + gate_ref[...] * y


def _outproj(a1, c1, a2, c2, w, x, mod, which, dims, n_rows, tn=1024):
    d = x.shape[1]
    kh = w.shape[0] // 2
    tm = ROW_BLOCK
    nlb, bpb, nb = dims
    mrow = lambda i: _mod_row(i, nlb, bpb, nb)
    return pl.pallas_call(
        _outproj_kernel,
        grid=(n_rows // tm, d // tn),
        in_specs=[pl.BlockSpec((tm, kh), lambda i, j: (i, c1)),
                  pl.BlockSpec((tm, kh), lambda i, j: (i, c2)),
                  pl.BlockSpec((kh, tn), lambda i, j: (0, j)),
                  pl.BlockSpec((kh, tn), lambda i, j: (1, j)),
                  pl.BlockSpec((tm, tn), lambda i, j: (i, j)),
                  pl.BlockSpec((None, 1, tn), lambda i, j: (mrow(i) * 6 + which, 0, j))],
        out_specs=pl.BlockSpec((tm, tn), lambda i, j: (i, j)),
        out_shape=jax.ShapeDtypeStruct((n_rows, d), f32),
        compiler_params=_params(("parallel", "arbitrary")),
    )(a1, a2, w, w, x, mod)


def _router_kernel(x_ref, g_ref, sc_ref, sh_ref, wr_ref, br_ref, u_ref,
                   h_ref, idx_ref, wt_ref, rank_ref, cnt_ref, carry_scr):
    i = pl.program_id(0)
    tm = x_ref.shape[0]
    ne, ng, pg = N_EXPERTS, N_GROUPS, N_EXPERTS // N_GROUPS

    @pl.when(i == 0)
    def _():
        carry_scr[...] = jnp.zeros_like(carry_scr)

    h = _norm_mod(x_ref[...], g_ref[...], sc_ref[...], sh_ref[...])
    h_ref[...] = h
    logits = lax.dot_general(wr_ref[...], h, (((1,), (1,)), ((), ())), preferred_element_type=f32,
                             precision=lax.Precision.HIGHEST)
    s = jax.nn.sigmoid(logits)
    s3 = s.reshape(ng, pg, tm)
    b3 = (s + br_ref[...]).reshape(ng, pg, tm)
    gi = lax.broadcasted_iota(i32, (ng, pg, tm), 0)
    pi = lax.broadcasted_iota(i32, (ng, pg, tm), 1)
    neg = -jnp.inf
    m1 = jnp.max(b3, axis=1, keepdims=True)
    first = jnp.min(jnp.where(b3 == m1, pi, pg), axis=1, keepdims=True)
    m2 = jnp.max(jnp.where(pi == first, neg, b3), axis=1, keepdims=True)
    gs = m1 + m2
    g2 = lax.broadcasted_iota(i32, (ng, 1, tm), 0)
    gmask = jnp.zeros((ng, 1, tm), jnp.bool_)
    for _ in range(TOPK_GROUPS):
        m = jnp.max(gs, axis=0, keepdims=True)
        fi = jnp.min(jnp.where(gs == m, g2, ng), axis=0, keepdims=True)
        pick = g2 == fi
        gmask = gmask | pick
        gs = jnp.where(pick, neg, gs)
    cand = jnp.where(gmask, b3, neg)
    eid = gi * pg + pi
    sel = jnp.zeros((ng, pg, tm), jnp.bool_)
    picks, idxs, wts = [], [], []
    for _ in range(TOP_K):
        m = jnp.max(jnp.max(cand, axis=0, keepdims=True), axis=1, keepdims=True)
        fi = jnp.min(jnp.min(jnp.where(cand == m, eid, ne), axis=0, keepdims=True), axis=1, keepdims=True)
        pick = eid == fi
        picks.append(pick)
        idxs.append(fi.reshape(1, tm))
        wts.append(jnp.sum(jnp.sum(jnp.where(pick, s3, 0.0), axis=0, keepdims=True), axis=1,
                           keepdims=True).reshape(1, tm))
        sel = sel | pick
        cand = jnp.where(pick, neg, cand)
    wsum = wts[0]
    for k in range(1, TOP_K):
        wsum = wsum + wts[k]
    idx_ref[...] = jnp.concatenate(idxs, axis=0)
    wt_ref[...] = jnp.concatenate([wk / wsum * ROUTED_SCALE for wk in wts], axis=0)
    sel2 = jnp.where(sel, 1.0, 0.0).reshape(ne, tm)
    prefix = (_dot(sel2.astype(bf16), u_ref[...]) + carry_scr[...]).reshape(ng, pg, tm)
    ranks = [jnp.sum(jnp.sum(jnp.where(pk, prefix, 0.0), axis=0, keepdims=True), axis=1,
                     keepdims=True).reshape(1, tm) for pk in picks]
    rank_ref[...] = jnp.concatenate(ranks, axis=0).astype(i32)
    total = carry_scr[...] + jnp.sum(sel2, axis=1, keepdims=True)
    carry_scr[...] = total
    cnt_ref[...] = total


def _router(x, g, mod, which, w_router, b_router, dims, n_rows):
    d = x.shape[1]
    tm = ROW_BLOCK
    ne = N_EXPERTS
    nlb, bpb, nb = dims
    mrow = lambda i: _mod_row(i, nlb, bpb, nb)
    tri = (jnp.arange(tm)[:, None] < jnp.arange(tm)[None, :]).astype(bf16)
    return pl.pallas_call(
        _router_kernel,
        grid=(n_rows // tm,),
        in_specs=[pl.BlockSpec((tm, d), lambda i: (i, 0)),
                  pl.BlockSpec((1, d), lambda i: (0, 0)),
                  pl.BlockSpec((None, 1, d), lambda i: (mrow(i) * 6 + which + 1, 0, 0)),
                  pl.BlockSpec((None, 1, d), lambda i: (mrow(i) * 6 + which, 0, 0)),
                  pl.BlockSpec((ne, d), lambda i: (0, 0)),
                  pl.BlockSpec((ne, 1), lambda i: (0, 0)),
                  pl.BlockSpec((tm, tm), lambda i: (0, 0))],
        out_specs=[pl.BlockSpec((tm, d), lambda i: (i, 0)),
                   pl.BlockSpec((TOP_K, tm), lambda i: (0, i)),
                   pl.BlockSpec((TOP_K, tm), lambda i: (0, i)),
                   pl.BlockSpec((TOP_K, tm), lambda i: (0, i)),
                   pl.BlockSpec((ne, 1), lambda i: (0, 0))],
        out_shape=[jax.ShapeDtypeStruct((n_rows, d), f32),
                   jax.ShapeDtypeStruct((TOP_K, n_rows), i32),
                   jax.ShapeDtypeStruct((TOP_K, n_rows), f32),
                   jax.ShapeDtypeStruct((TOP_K, n_rows), i32),
                   jax.ShapeDtypeStruct((ne, 1), f32)],
        scratch_shapes=[pltpu.VMEM((ne, 1), f32)],
        compiler_params=_params(("arbitrary",)),
    )(x, g.reshape(1, d), mod, mod, w_router.T, b_router.reshape(ne, 1).astype(f32), tri)


EXPERT_BURSTS = 8


def _expert_kernel(be_ref, nu_ref, tok0_ref, tokn_ref, posp_ref, h_hbm, wg_ref, wu_ref, wd_ref, y_hbm,
                   xbuf, ybuf, wgc, wuc, wdc, sem_g, sem_s):
    i = pl.program_id(0)
    nblk = pl.num_programs(0) - 1
    n_used = nu_ref[0]
    slot = i % 2
    mb = MOE_BLOCK
    de = wgc.shape[1]
    trash0 = y_hbm.shape[0] - mb

    def gather_row(idx_ref, r, dst_slot):
        pltpu.make_async_copy(h_hbm.at[pl.ds(idx_ref[0, 0, r], 1)], xbuf.at[dst_slot, pl.ds(r, 1)],
                              sem_g.at[dst_slot]).start()

    def scatter_row(r):
        p = jnp.where(i == 0, trash0 + r, posp_ref[0, 0, r])
        pltpu.make_async_copy(ybuf.at[1 - slot, pl.ds(r, 1)], y_hbm.at[pl.ds(p, 1)], sem_s.at[0]).start()

    def burst(b):
        per = mb // EXPERT_BURSTS
        for r in range(b * per, (b + 1) * per):
            gather_row(tokn_ref, r, 1 - slot)
            scatter_row(r)

    @pl.when(i == 0)
    def _():
        ybuf[1] = jnp.zeros((mb, ybuf.shape[2]), f32)

        def body(r, carry):
            gather_row(tok0_ref, r, 0)
            return carry
        lax.fori_loop(0, mb, body, 0)

    pltpu.make_async_copy(h_hbm.at[pl.ds(0, mb)], xbuf.at[slot], sem_g.at[slot]).wait()
    ib = jnp.minimum(i, nblk - 1)
    changed = (i == 0) | (be_ref[ib] != be_ref[jnp.maximum(ib - 1, 0)])

    @pl.when(changed & (i < n_used))
    def _():
        wgc[...] = wg_ref[...].astype(bf16)
        wuc[...] = wu_ref[...].astype(bf16)
        wdc[...] = wd_ref[...].astype(bf16)

    @pl.when(i < n_used)
    def _():
        x = xbuf[slot].astype(bf16)
        nh = 2
        hw = de // nh
        hids = []
        for c in range(nh):
            gate = _dot(x, wgc[:, c * hw:(c + 1) * hw])
            burst(2 * c)
            up = _dot(x, wuc[:, c * hw:(c + 1) * hw])
            burst(2 * c + 1)
            hids.append((_silu(gate) * up).astype(bf16))
        hid = jnp.concatenate(hids, axis=1)
        nd = EXPERT_BURSTS - 2 * nh
        dw = ybuf.shape[2] // nd
        for c in range(nd):
            ybuf[slot, :, c * dw:(c + 1) * dw] = _dot(hid, wdc[:, c * dw:(c + 1) * dw])
            burst(2 * nh + c)

    @pl.when(i >= n_used)
    def _():
        for b in range(EXPERT_BURSTS):
            burst(b)

    pltpu.make_async_copy(ybuf.at[0], y_hbm.at[pl.ds(0, mb)], sem_s.at[0]).wait()

    @pl.when(i == nblk)
    def _():
        pltpu.make_async_copy(h_hbm.at[pl.ds(0, mb)], xbuf.at[1 - slot], sem_g.at[1 - slot]).wait()


def _experts(h, slot_tok, slot_pos, block_e, n_used, wg, wu, wd, layer, n_rows):
    d = h.shape[1]
    nblk = slot_tok.shape[0]
    de = wg.shape[3]
    mb = MOE_BLOCK
    last = nblk - 1
    smem = lambda f: pl.BlockSpec((1, 1, mb), f, memory_space=pltpu.SMEM)
    wspec = lambda shape: pl.BlockSpec((None,) + shape, lambda i, be, nu: (layer, be[jnp.minimum(i, last)], 0, 0))
    grid_spec = pltpu.PrefetchScalarGridSpec(
        num_scalar_prefetch=2,
        grid=(nblk + 1,),
        in_specs=[smem(lambda i, be, nu: (0, 0, 0)),
                  smem(lambda i, be, nu: (jnp.minimum(i + 1, last), 0, 0)),
                  smem(lambda i, be, nu: (jnp.clip(i - 1, 0, last), 0, 0)),
                  pl.BlockSpec(memory_space=pl.ANY),
                  wspec((None, d, de)), wspec((None, d, de)), wspec((None, de, d))],
        out_specs=pl.BlockSpec(memory_space=pl.ANY),
        scratch_shapes=[pltpu.VMEM((2, mb, d), f32), pltpu.VMEM((2, mb, d), f32),
                        pltpu.VMEM((d, de), bf16), pltpu.VMEM((d, de), bf16), pltpu.VMEM((de, d), bf16),
                        pltpu.SemaphoreType.DMA((2,)), pltpu.SemaphoreType.DMA((1,))],
    )
    st = slot_tok.reshape(nblk, 1, mb)
    sp = slot_pos.reshape(nblk, 1, mb)
    return pl.pallas_call(
        _expert_kernel,
        grid_spec=grid_spec,
        out_shape=jax.ShapeDtypeStruct((n_rows * TOP_K + mb, d), f32),
        compiler_params=_params(("arbitrary",)),
    )(block_e, n_used, st, st, sp, h, wg, wu, wd)


def _shared_kernel(h_ref, wg_ref, wu_ref, wd_ref, y_ref):
    x = h_ref[...].astype(bf16)
    hid = _silu(_dot(x, wg_ref[...])) * _dot(x, wu_ref[...])
    y_ref[...] = _dot(hid.astype(bf16), wd_ref[...])


def _shared(h, wg, wu, wd):
    t, d = h.shape
    de = wg.shape[1]
    tm = ROW_BLOCK
    return pl.pallas_call(
        _shared_kernel,
        grid=(t // tm,),
        in_specs=[pl.BlockSpec((tm, d), lambda i: (i, 0)),
                  pl.BlockSpec((d, de), lambda i: (0, 0)),
                  pl.BlockSpec((d, de), lambda i: (0, 0)),
                  pl.BlockSpec((de, d), lambda i: (0, 0))],
        out_specs=pl.BlockSpec((tm, d), lambda i: (i, 0)),
        out_shape=jax.ShapeDtypeStruct((t, d), f32),
        compiler_params=_params(("parallel",)),
    )(h, wg, wu, wd)


COMBINE_ROWS = 64


def _combine_kernel(y_ref, w_ref, sh_ref, x_ref, gate_ref, gf_ref, o_ref, *, final_norm):
    tm, d = x_ref.shape
    y = (y_ref[...] * w_ref[...]).reshape(tm, TOP_K, d)
    acc = sh_ref[...] + jnp.sum(y, axis=1)
    out = x_ref[...] + gate_ref[...] * acc
    if final_norm:
        out = out * lax.rsqrt(jnp.mean(out * out, axis=-1, keepdims=True) + EPS) * gf_ref[...]
    o_ref[...] = out


def _combine(y_rows, wts, shared, x, mod, which, g_final, dims, n_rows, final_norm):
    d = x.shape[1]
    tm = COMBINE_ROWS
    per = ROW_BLOCK // tm
    nlb, bpb, nb = dims
    mrow = lambda i: _mod_row(i // per, nlb, bpb, nb)
    wcol = wts.T.reshape(n_rows * TOP_K, 1)
    return pl.pallas_call(
        functools.partial(_combine_kernel, final_norm=final_norm),
        grid=(n_rows // tm,),
        in_specs=[pl.BlockSpec((tm * TOP_K, d), lambda i: (i, 0)),
                  pl.BlockSpec((tm * TOP_K, 1), lambda i: (i, 0)),
                  pl.BlockSpec((tm, d), lambda i: (i, 0)),
                  pl.BlockSpec((tm, d), lambda i: (i, 0)),
                  pl.BlockSpec((None, 1, d), lambda i: (mrow(i) * 6 + which, 0, 0)),
                  pl.BlockSpec((1, d), lambda i: (0, 0))],
        out_specs=pl.BlockSpec((tm, d), lambda i: (i, 0)),
        out_shape=jax.ShapeDtypeStruct((n_rows, d), f32),
        compiler_params=_params(("parallel",)),
    )(y_rows, wcol, shared, x, mod, g_final.reshape(1, d))


def _moe(x, g, mod, which_shift, w_router, b_router, wg, wu, wd, layer, sg, su, sd, g_final, dims, n_rows,
         final_norm):
    ne = N_EXPERTS
    mb = MOE_BLOCK
    h, idx, wts, rank, cnt = _router(x, g, mod, which_shift, w_router, b_router, dims, n_rows)
    counts = cnt[:, 0].astype(i32)
    padded = (counts + mb - 1) // mb * mb
    pend = jnp.cumsum(padded)
    start = pend - padded
    onehot = idx[:, :, None] == jnp.arange(ne, dtype=i32)[None, None, :]
    dest = jnp.sum(jnp.where(onehot, start[None, None, :], 0), axis=-1) + rank
    n_assign = n_rows * TOP_K
    nblk = (n_assign + ne * (mb - 1) + mb - 1) // mb
    pos = jnp.arange(n_rows, dtype=i32)[None, :] * TOP_K + jnp.arange(TOP_K, dtype=i32)[:, None]
    spare = n_assign + jnp.arange(nblk * mb, dtype=i32) % mb
    slot_pos = spare.at[dest.reshape(-1)].set(pos.reshape(-1), unique_indices=True)
    slot_tok = jnp.where(slot_pos < n_assign, slot_pos // TOP_K, 0)
    blk_start = jnp.arange(nblk, dtype=i32) * mb
    block_e = jnp.minimum(jnp.sum((pend[None, :] <= blk_start[:, None]).astype(i32), axis=1), ne - 1)
    n_used = (pend[-1:] // mb).astype(i32)
    y_rows = _experts(h, slot_tok.reshape(nblk, mb), slot_pos.reshape(nblk, mb), block_e, n_used, wg, wu, wd,
                      layer, n_rows)
    shared = _shared(h, sg, su, sd)
    return _combine(y_rows, wts, shared, x, mod, which_shift + 2, g_final, dims, n_rows, final_norm)


def _rope_tables(seq, head_dim):
    half = head_dim // 2
    quarter = half // 2
    t = jnp.arange(seq, dtype=i32)
    rows = (t // GRID_W).astype(f32)
    cols = (t % GRID_W).astype(f32)
    freqs = ROPE_THETA ** (-jnp.arange(0, half, 2, dtype=f32) / half)
    ar = rows[:, None] * freqs[None, :]
    ac = cols[:, None] * freqs[None, :]
    cos = jnp.concatenate([jnp.cos(ar), jnp.cos(ar), jnp.cos(ac), jnp.cos(ac)], axis=1)
    sin = jnp.concatenate([-jnp.sin(ar), jnp.sin(ar), -jnp.sin(ac), jnp.sin(ac)], axis=1)
    reps = LANE // head_dim
    cos = jnp.tile(cos, (1, reps))
    sin = jnp.tile(sin, (1, reps))
    cos = jnp.concatenate([cos, jnp.ones((ROW_BLOCK, LANE), f32)], axis=0)
    sin = jnp.concatenate([sin, jnp.zeros((ROW_BLOCK, LANE), f32)], axis=0)
    return cos, sin, quarter


def kernel(x, c, ctx, c_ctx, w_ada, b_ada, g_norm_mix, g_norm_ffn, w_in_even, w_out_even, hgrn_lb_logits,
           g_hgrn_norm, win_sink, w_in_odd, w_out_odd, g_q_norm, g_k_norm, w_router, b_router,
           w_exp_gate, w_exp_up, w_exp_down, w_sh_gate, w_sh_up, w_sh_down, g_norm_final):
    nb, seq, d = x.shape
    lc = ctx.shape[1]
    assert seq % ROW_BLOCK == 0 and lc % ROW_BLOCK == 0 and seq % lc == 0
    t_lat = nb * seq
    t_all = t_lat + nb * lc
    bpb = seq // ROW_BLOCK
    nlb = t_lat // ROW_BLOCK
    dims = (nlb, bpb, nb)

    xs = jnp.concatenate([x.reshape(t_lat, d), ctx.reshape(nb * lc, d)], axis=0)
    cc = jnp.concatenate([c, c_ctx[None, :], jnp.zeros((8 - nb - 1, d), f32)], axis=0)

    lb_w = jax.nn.softmax(hgrn_lb_logits.astype(f32), axis=0)
    lower_bounds = jnp.cumsum(lb_w, axis=0)[1:] - lb_w[0]

    mod = _ada(cc, w_ada, b_ada, 0).reshape(8 * 6, 1, d)
    w_in = w_in_even[0].astype(bf16)
    proj = _normmm(xs, g_norm_mix[0], mod, 0, w_in[:, :HGRN_COLS], f32, dims, tn=1280)
    cos, sin, quarter = _rope_tables(seq, WIN_HEAD_DIM)
    epi = dict(blocks={0: (WIN_HEADS + WIN_KV_HEADS) * WIN_HEAD_DIM // LANE}, quarter=quarter, head_norm=False,
               q_slices=WIN_HEADS * WIN_HEAD_DIM // LANE, q_scale=WIN_HEAD_DIM ** -0.5)
    wproj = _normmm(xs, g_norm_mix[0], mod, 0, w_in[:, HGRN_COLS:], bf16, dims, tn=WIN_COLS,
                    epilogue=epi, tables=(cos, sin))
    o_f, o_b = _hgrn(proj, lower_bounds[0], nb, seq, lc)
    a_mix = _hgrn_out(o_f, o_b, proj, g_hgrn_norm[0])
    b_mix = _win_attn(wproj, win_sink[0], nb, seq, lc)
    xs = _outproj(a_mix, 0, b_mix, 0, w_out_even[0].astype(bf16), xs, mod, 2, dims, t_all)
    xs = _moe(xs, g_norm_ffn[0], mod, 3, w_router[0], b_router[0],
              w_exp_gate, w_exp_up, w_exp_down, 0,
              w_sh_gate[0].astype(bf16), w_sh_up[0].astype(bf16), w_sh_down[0].astype(bf16),
              g_norm_final, dims, t_all, False)

    mod = _ada(cc, w_ada, b_ada, 1).reshape(8 * 6, 1, d)
    cos, sin, quarter = _rope_tables(seq, GLB_HEAD_DIM)
    tn = 1536
    n_sl = tn // LANE
    rope_slices = GLB_HEADS + GLB_KV_HEADS
    epi = dict(blocks={0: min(rope_slices, n_sl), 1: max(rope_slices - n_sl, 0)}, quarter=quarter, head_norm=True,
               q_slices=GLB_HEADS, q_scale=GLB_HEAD_DIM ** -0.5)
    qkv = _normmm(xs, g_norm_mix[1], mod, 0, w_in_odd[0].astype(bf16), bf16, dims, tn=tn,
                  epilogue=epi, tables=(cos, sin), gains=(g_q_norm[0], g_k_norm[0]))
    att = _glb_attn(qkv, nb, seq, lc)
    xl = _outproj(att, 0, att, 1, w_out_odd[0].astype(bf16), xs, mod, 2, dims, t_lat)
    out = _moe(xl, g_norm_ffn[1], mod, 3, w_router[1], b_router[1],
               w_exp_gate, w_exp_up, w_exp_down, 1,
               w_sh_gate[1].astype(bf16), w_sh_up[1].astype(bf16), w_sh_down[1].astype(bf16),
               g_norm_final, dims, t_lat, True)
    return out.reshape(nb, seq, d)
```

```python
import functools

import jax
import jax.numpy as jnp
from jax import lax
from jax.experimental import pallas as pl
from jax.experimental.pallas import tpu as pltpu

f32 = jnp.float32
bf16 = jnp.bfloat16
i32 = jnp.int32

EPS = 1e-6
ROPE_THETA = 10000.0
GRID_W = 64

HGRN_DK = 128
HGRN_HEADS = 8
HGRN_CHUNK = 64
HGRN_SUB = 16
HGRN_HEADS_PER_STEP = 4
WIN_HEAD_DIM = 64
WIN_HEADS = 16
WIN_KV_HEADS = 2
WIN_BLOCK = 128
GLB_HEAD_DIM = 128
GLB_HEADS = 16
GLB_KV_HEADS = 4
Q_BLOCK = 128
N_EXPERTS = 64
N_GROUPS = 8
TOPK_GROUPS = 4
TOP_K = 8
ROUTED_SCALE = 2.5
MOE_BLOCK = 256

LANE = 128
ROW_BLOCK = 256
VMEM_LIMIT = 56 * 1024 * 1024

A_K = HGRN_HEADS * HGRN_DK
HGRN_COLS = 5 * A_K
WIN_COLS = WIN_HEADS * WIN_HEAD_DIM + 2 * WIN_KV_HEADS * WIN_HEAD_DIM


def _params(sem, vmem=VMEM_LIMIT):
    return pltpu.CompilerParams(dimension_semantics=sem, vmem_limit_bytes=vmem)


def _silu(x):
    return x * jax.nn.sigmoid(x)


def _dot(a, b):
    return jnp.dot(a, b, preferred_element_type=f32)


def _dot_nt(a, b):
    return lax.dot_general(a, b, (((1,), (1,)), ((), ())), preferred_element_type=f32)


def _dot_tn(a, b):
    return lax.dot_general(a, b, (((0,), (0,)), ((), ())), preferred_element_type=f32)


def _ada_kernel(c_ref, w_ref, b_ref, o_ref):
    a = _silu(c_ref[...]).astype(bf16)
    o_ref[...] = _dot(a, w_ref[...].astype(bf16)) + b_ref[...]


def _ada(c8, w, b, layer, tn=768):
    m, d = c8.shape
    n = w.shape[2]
    return pl.pallas_call(
        _ada_kernel,
        grid=(n // tn,),
        in_specs=[pl.BlockSpec((m, d), lambda j: (0, 0)),
                  pl.BlockSpec((None, d, tn), lambda j: (layer, 0, j)),
                  pl.BlockSpec((None, 1, tn), lambda j: (layer, 0, j))],
        out_specs=pl.BlockSpec((m, tn), lambda j: (0, j)),
        out_shape=jax.ShapeDtypeStruct((m, n), f32),
        compiler_params=_params(("arbitrary",)),
    )(c8, w, b.reshape(b.shape[0], 1, n))


def _norm_mod(x, g, sc, sh):
    y = x * lax.rsqrt(jnp.mean(x * x, axis=-1, keepdims=True) + EPS) * g
    return y * (1.0 + sc) + sh


def _rope_slice(x, cos, sin, quarter):
    lane = lax.broadcasted_iota(i32, x.shape, 1)
    up = pltpu.roll(x, LANE - quarter, axis=1)
    dn = pltpu.roll(x, quarter, axis=1)
    partner = jnp.where(lane % (2 * quarter) < quarter, up, dn)
    return x * cos + partner * sin


def _normmm_kernel(x_ref, g_ref, sc_ref, sh_ref, w_ref, *rest, epilogue, tn):
    if epilogue is None:
        (o_ref,) = rest
    elif epilogue["head_norm"]:
        cos_ref, sin_ref, gq_ref, gk_ref, o_ref = rest
    else:
        cos_ref, sin_ref, o_ref = rest
    j = pl.program_id(0)
    h = _norm_mod(x_ref[...], g_ref[...], sc_ref[...], sh_ref[...]).astype(bf16)
    y = _dot(h, w_ref[...])
    if epilogue is None:
        o_ref[...] = y.astype(o_ref.dtype)
        return

    n_sl = tn // LANE
    for jb in range(epilogue["n_col_blocks"]):
        n_rope = epilogue["blocks"].get(jb, 0)

        @pl.when(j == jb)
        def _(jb=jb, n_rope=n_rope):
            cos = cos_ref[...]
            sin = sin_ref[...]
            outs = []
            for s in range(n_sl):
                ys = y[:, s * LANE:(s + 1) * LANE]
                if s < n_rope:
                    col = jb * n_sl + s
                    is_q = col < epilogue["q_slices"]
                    if epilogue["head_norm"]:
                        gain = gq_ref[...] if is_q else gk_ref[...]
                        ys = ys * lax.rsqrt(jnp.mean(ys * ys, axis=-1, keepdims=True) + EPS) * gain
                    ys = _rope_slice(ys, cos, sin, epilogue["quarter"])
                    if is_q:
                        ys = ys * epilogue["q_scale"]
                outs.append(ys.astype(o_ref.dtype))
            o_ref[...] = jnp.concatenate(outs, axis=1)


def _mod_row(i, n_lat_blocks, blocks_per_batch, n_batch):
    return jnp.where(i < n_lat_blocks, i // blocks_per_batch, n_batch)


def _normmm(x, g, mod, which, w, out_dtype, dims, tn, epilogue=None, tables=None, gains=None):
    t, d = x.shape
    n = w.shape[1]
    tm = ROW_BLOCK
    nlb, bpb, nb = dims
    mrow = lambda i: _mod_row(i, nlb, bpb, nb)
    in_specs = [pl.BlockSpec((tm, d), lambda j, i: (i, 0)),
                pl.BlockSpec((1, d), lambda j, i: (0, 0)),
                pl.BlockSpec((None, 1, d), lambda j, i: (mrow(i) * 6 + which + 1, 0, 0)),
                pl.BlockSpec((None, 1, d), lambda j, i: (mrow(i) * 6 + which, 0, 0)),
                pl.BlockSpec((d, tn), lambda j, i: (0, j))]
    args = [x, g.reshape(1, d), mod, mod, w]
    if epilogue is not None:
        cos, sin = tables
        tab = lambda j, i: (jnp.where(i < nlb, i % bpb, bpb), 0)
        in_specs += [pl.BlockSpec((tm, LANE), tab), pl.BlockSpec((tm, LANE), tab)]
        args += [cos, sin]
        epilogue = dict(epilogue, n_col_blocks=n // tn)
        if epilogue["head_norm"]:
            in_specs += [pl.BlockSpec((1, LANE), lambda j, i: (0, 0))] * 2
            args += [gains[0].reshape(1, LANE), gains[1].reshape(1, LANE)]
    return pl.pallas_call(
        functools.partial(_normmm_kernel, epilogue=epilogue, tn=tn),
        grid=(n // tn, t // tm),
        in_specs=in_specs,
        out_specs=pl.BlockSpec((tm, tn), lambda j, i: (i, j)),
        out_shape=jax.ShapeDtypeStruct((t, n), out_dtype),
        compiler_params=_params(("parallel", "parallel")),
    )(*args)


def _gla_chunk(q_raw, v, f_raw, lb, st_ref, a_scr, rev):
    c = HGRN_CHUNK
    sub = HGRN_SUB
    q = _silu(q_raw)
    f = lb + (1.0 - lb) * jax.nn.sigmoid(f_raw)
    kk = 1.0 - f
    g = jnp.log(f)
    row = lax.broadcasted_iota(i32, (c, c), 0)
    col = lax.broadcasted_iota(i32, (c, c), 1)
    tri = (col >= row) if rev else (col <= row)
    b = jnp.dot(tri.astype(f32), g, preferred_element_type=f32, precision=lax.Precision.HIGHEST)
    b_end = b[0:1] if rev else b[c - 1:c]
    st = st_ref[...]
    o = _dot_nt((q * jnp.exp(b)).astype(bf16), st.astype(bf16))

    vb = v.astype(bf16)
    nsub = c // sub
    lane16 = lax.broadcasted_iota(i32, (sub, sub), 1)
    row16 = lax.broadcasted_iota(i32, (sub, sub), 0)
    diag_ok = (row16 <= lane16) if rev else (row16 >= lane16)
    rsub = lax.broadcasted_iota(i32, (c, sub), 0) // sub
    for jb in range(nsub):
        js = slice(jb * sub, (jb + 1) * sub)
        m_j = b[jb * sub:jb * sub + 1] if rev else b[(jb + 1) * sub - 1:(jb + 1) * sub]
        kd = kk[js] * jnp.exp(m_j - b[js])
        qd = q * jnp.exp(jnp.minimum(b - m_j, 0.0))
        a_col = _dot_nt(qd.astype(bf16), kd.astype(bf16))
        off_ok = (rsub < jb) if rev else (rsub > jb)
        a_col = jnp.where(off_ok, a_col, 0.0)
        qi, bi, ki = q[js], b[js], kk[js]
        a_dd = jnp.zeros((sub, sub), f32)
        for s in range(sub):
            e = jnp.exp(jnp.minimum(bi - bi[s:s + 1], 0.0))
            col_s = jnp.sum(qi * ki[s:s + 1] * e, axis=-1, keepdims=True)
            a_dd = jnp.where(lane16 == s, col_s, a_dd)
        a_dd = jnp.where(diag_ok, a_dd, 0.0)
        pieces = [a_dd if ib == jb else a_col[ib * sub:(ib + 1) * sub] for ib in range(nsub)]
        a_scr[:, js] = jnp.concatenate(pieces, axis=0)
    o = o + _dot(a_scr[...].astype(bf16), vb)
    kdec = kk * jnp.exp(b_end - b)
    st_ref[...] = jnp.exp(b_end) * st + _dot_tn(vb, kdec.astype(bf16))
    return o


def _hgrn_kernel(qf_ref, vf_ref, ff_ref, qb_ref, vb_ref, fb_ref, lb_ref, of_ref, ob_ref,
                 stf_ref, stb_ref, af_scr, ab_scr):
    @pl.when(pl.program_id(2) == 0)
    def _():
        stf_ref[...] = jnp.zeros_like(stf_ref)
        stb_ref[...] = jnp.zeros_like(stb_ref)

    for j in range(HGRN_HEADS_PER_STEP):
        sl = slice(j * HGRN_DK, (j + 1) * HGRN_DK)
        lb = lb_ref[:, sl]
        of_ref[:, sl] = _gla_chunk(qf_ref[:, sl], vf_ref[:, sl], ff_ref[:, sl], lb, stf_ref.at[j], af_scr.at[j], False)
        ob_ref[:, sl] = _gla_chunk(qb_ref[:, sl], vb_ref[:, sl], fb_ref[:, sl], lb, stb_ref.at[j], ab_scr.at[j], True)


def _hgrn(proj, lb, n_batch, seq, ctx_len):
    t = proj.shape[0]
    c = HGRN_CHUNK
    hp = HGRN_HEADS_PER_STEP
    wblk = hp * HGRN_DK
    ngrp = HGRN_HEADS // hp
    ncc, ncl = ctx_len // c, seq // c
    lat_chunks = n_batch * ncl

    def fwd_blk(b, s):
        return jnp.where(s < ncc, lat_chunks + b * ncc + s, b * ncl + (s - ncc))

    def bwd_blk(b, s):
        return jnp.where(s < ncc, lat_chunks + b * ncc + (ncc - 1 - s), b * ncl + (ncl - 1 - (s - ncc)))

    def spec(blk, section):
        return pl.BlockSpec((c, wblk), lambda b, h, s: (blk(b, s), section * ngrp + h))

    def ospec(blk):
        return pl.BlockSpec((c, wblk), lambda b, h, s: (blk(b, s), h))

    out = jax.ShapeDtypeStruct((t, A_K), f32)
    return pl.pallas_call(
        _hgrn_kernel,
        grid=(n_batch, ngrp, ncc + ncl),
        in_specs=[spec(fwd_blk, 0), spec(fwd_blk, 1), spec(fwd_blk, 2),
                  spec(bwd_blk, 0), spec(bwd_blk, 1), spec(bwd_blk, 3),
                  pl.BlockSpec((1, wblk), lambda b, h, s: (0, h))],
        out_specs=[ospec(fwd_blk), ospec(bwd_blk)],
        out_shape=[out, out],
        scratch_shapes=[pltpu.VMEM((hp, HGRN_DK, HGRN_DK), f32), pltpu.VMEM((hp, HGRN_DK, HGRN_DK), f32),
                        pltpu.VMEM((hp, c, c), f32), pltpu.VMEM((hp, c, c), f32)],
        compiler_params=_params(("parallel", "parallel", "arbitrary")),
    )(proj, proj, proj, proj, proj, proj, lb.reshape(1, A_K))


def _hgrn_out_kernel(of_ref, ob_ref, g_ref, gain_ref, o_ref):
    gain = gain_ref[...]
    outs = []
    for h in range(HGRN_HEADS):
        sl = slice(h * HGRN_DK, (h + 1) * HGRN_DK)
        o = of_ref[:, sl] + ob_ref[:, sl]
        o = o * lax.rsqrt(jnp.mean(o * o, axis=-1, keepdims=True) + EPS) * gain
        outs.append((o * _silu(g_ref[:, sl])).astype(bf16))
    o_ref[...] = jnp.concatenate(outs, axis=1)


def _hgrn_out(o_f, o_b, proj, gain):
    t = o_f.shape[0]
    tm = ROW_BLOCK
    return pl.pallas_call(
        _hgrn_out_kernel,
        grid=(t // tm,),
        in_specs=[pl.BlockSpec((tm, A_K), lambda i: (i, 0)),
                  pl.BlockSpec((tm, A_K), lambda i: (i, 0)),
                  pl.BlockSpec((tm, A_K), lambda i: (i, 4)),
                  pl.BlockSpec((1, HGRN_DK), lambda i: (0, 0))],
        out_specs=pl.BlockSpec((tm, A_K), lambda i: (i, 0)),
        out_shape=jax.ShapeDtypeStruct((t, A_K), bf16),
        compiler_params=_params(("parallel",)),
    )(o_f, o_b, proj, gain.reshape(1, HGRN_DK))


def _win_kernel(sink_ref, q_ref, kp_ref, kc_ref, kn_ref, vp_ref, vc_ref, vn_ref, kx_ref, vx_ref, o_ref,
                *, n_lat_blocks):
    n = pl.program_id(1)
    w = WIN_BLOCK
    dh = WIN_HEAD_DIM
    is_lat = n < n_lat_blocks
    ri = lax.broadcasted_iota(i32, (w, w), 0)
    ci = lax.broadcasted_iota(i32, (w, w), 1)
    ok_p = (ci >= ri) & is_lat & (n > 0)
    ok_c = jnp.broadcast_to(is_lat, (w, w))
    ok_n = (ci <= ri) & is_lat & (n < n_lat_blocks - 1)
    valid = jnp.concatenate([ok_p, ok_c, ok_n], axis=1)
    kwin = jnp.concatenate([kp_ref[...], kc_ref[...], kn_ref[...]], axis=0)
    vwin = jnp.concatenate([vp_ref[...], vc_ref[...], vn_ref[...]], axis=0)
    kx = kx_ref[...]
    vx = vx_ref[...]
    g = WIN_HEADS // WIN_KV_HEADS
    outs = []
    for h in range(WIN_HEADS):
        kv = h // g
        ks = slice(kv * dh, (kv + 1) * dh)
        qh = q_ref[:, h * dh:(h + 1) * dh]
        sw = jnp.where(valid, _dot_nt(qh, kwin[:, ks]), -jnp.inf)
        sx = _dot_nt(qh, kx[:, ks])
        sk = sink_ref[h]
        m = jnp.maximum(jnp.maximum(jnp.max(sw, axis=-1, keepdims=True), jnp.max(sx, axis=-1, keepdims=True)), sk)
        pw = jnp.exp(sw - m)
        px = jnp.exp(sx - m)
        den = jnp.sum(pw, axis=-1, keepdims=True) + jnp.sum(px, axis=-1, keepdims=True) + jnp.exp(sk - m)
        o = _dot(pw.astype(bf16), vwin[:, ks]) + _dot(px.astype(bf16), vx[:, ks])
        outs.append((o / den).astype(bf16))
    o_ref[...] = jnp.concatenate(outs, axis=1)


def _win_attn(wproj, sink, n_batch, seq, ctx_len):
    t = wproj.shape[0]
    w = WIN_BLOCK
    nlb = seq // w
    ncb = ctx_len // w
    lat_blocks = n_batch * nlb
    qc = WIN_HEADS * WIN_HEAD_DIM // LANE
    kcol, vcol = qc, qc + 1

    def qrow(b, n):
        return jnp.where(n < nlb, b * nlb + n, lat_blocks + b * ncb + (n - nlb))

    def krow(off):
        def f(b, n):
            return b * nlb + jnp.clip(n + off, 0, nlb - 1)
        return f

    def kspec(off, colblk):
        return pl.BlockSpec((w, LANE), lambda b, n: (krow(off)(b, n), colblk))

    def xspec(colblk):
        return pl.BlockSpec((ctx_len, LANE), lambda b, n: ((n_batch * seq) // ctx_len + b, colblk))

    return pl.pallas_call(
        functools.partial(_win_kernel, n_lat_blocks=nlb),
        grid=(n_batch, nlb + ncb),
        in_specs=[pl.BlockSpec(memory_space=pltpu.SMEM),
                  pl.BlockSpec((w, qc * LANE), lambda b, n: (qrow(b, n), 0)),
                  kspec(-1, kcol), kspec(0, kcol), kspec(1, kcol),
                  kspec(-1, vcol), kspec(0, vcol), kspec(1, vcol),
                  xspec(kcol), xspec(vcol)],
        out_specs=pl.BlockSpec((w, qc * LANE), lambda b, n: (qrow(b, n), 0)),
        out_shape=jax.ShapeDtypeStruct((t, qc * LANE), bf16),
        compiler_params=_params(("parallel", "arbitrary")),
    )(sink.astype(f32), wproj, wproj, wproj, wproj, wproj, wproj, wproj, wproj, wproj)


GLB_KEY_CHUNK = 512


def _glb_kernel(q_ref, kx_ref, kl_ref, vx_ref, vl_ref, o_ref, m_scr, l_scr, acc_scr):
    dh = GLB_HEAD_DIM
    g = GLB_HEADS // GLB_KV_HEADS
    ck = GLB_KEY_CHUNK
    tq = q_ref.shape[0]
    q = jnp.concatenate([q_ref[:, i * dh:(i + 1) * dh] for i in range(g)], axis=0)

    def update(k, v, first):
        s = _dot_nt(q, k)
        reps = s.shape[1] // LANE
        mx = jnp.broadcast_to(jnp.max(s, axis=-1, keepdims=True), (g * tq, LANE))
        if first:
            m_new = mx
            p = jnp.exp(s - jnp.concatenate([m_new] * reps, axis=1))
            l_scr[...] = jnp.broadcast_to(jnp.sum(p, axis=-1, keepdims=True), (g * tq, LANE))
            acc_scr[...] = _dot(p.astype(bf16), v)
        else:
            m_old = m_scr[...]
            m_new = jnp.maximum(m_old, mx)
            alpha = jnp.exp(m_old - m_new)
            p = jnp.exp(s - jnp.concatenate([m_new] * reps, axis=1))
            l_scr[...] = alpha * l_scr[...] + jnp.broadcast_to(jnp.sum(p, axis=-1, keepdims=True), (g * tq, LANE))
            acc_scr[...] = alpha * acc_scr[...] + _dot(p.astype(bf16), v)
        m_scr[...] = m_new

    chunks = [(kx_ref, vx_ref, c, min(ck, kx_ref.shape[0] - c)) for c in range(0, kx_ref.shape[0], ck)]
    chunks += [(kl_ref, vl_ref, c, min(ck, kl_ref.shape[0] - c)) for c in range(0, kl_ref.shape[0], ck)]
    for n, (kr, vr, c0, sz) in enumerate(chunks):
        update(kr[c0:c0 + sz, :], vr[c0:c0 + sz, :], n == 0)
    o = acc_scr[...] / l_scr[...]
    o_ref[...] = jnp.concatenate([o[i * tq:(i + 1) * tq] for i in range(g)], axis=1).astype(bf16)


def _glb_attn(qkv, n_batch, seq, ctx_len):
    tq = Q_BLOCK
    dh = GLB_HEAD_DIM
    g = GLB_HEADS // GLB_KV_HEADS
    nq = seq // tq
    kcol0 = GLB_HEADS
    vcol0 = GLB_HEADS + GLB_KV_HEADS
    ctx_blk0 = (n_batch * seq) // ctx_len
    return pl.pallas_call(
        _glb_kernel,
        grid=(n_batch, GLB_KV_HEADS, nq),
        in_specs=[pl.BlockSpec((tq, g * dh), lambda b, k, i: (b * nq + i, k)),
                  pl.BlockSpec((ctx_len, dh), lambda b, k, i: (ctx_blk0 + b, kcol0 + k)),
                  pl.BlockSpec((seq, dh), lambda b, k, i: (b, kcol0 + k)),
                  pl.BlockSpec((ctx_len, dh), lambda b, k, i: (ctx_blk0 + b, vcol0 + k)),
                  pl.BlockSpec((seq, dh), lambda b, k, i: (b, vcol0 + k))],
        out_specs=pl.BlockSpec((tq, g * dh), lambda b, k, i: (b * nq + i, k)),
        out_shape=jax.ShapeDtypeStruct((n_batch * seq, GLB_HEADS * dh), bf16),
        scratch_shapes=[pltpu.VMEM((g * tq, LANE), f32), pltpu.VMEM((g * tq, LANE), f32), pltpu.VMEM((g * tq, dh), f32)],
        compiler_params=_params(("parallel", "parallel", "arbitrary")),
    )(qkv, qkv, qkv, qkv, qkv)


def _outproj_kernel(a1_ref, a2_ref, w1_ref, w2_ref, x_ref, gate_ref, o_ref):
    y = _dot(a1_ref[...], w1_ref[...]) + _dot(a2_ref[...], w2_ref[...])
    o_ref[...] = x_ref[...] + gate_ref[...] * y


def _outproj(a1, c1, a2, c2, w, x, mod, which, dims, n_rows, tn=1024):
    d = x.shape[1]
    kh = w.shape[0] // 2
    tm = ROW_BLOCK
    nlb, bpb, nb = dims
    mrow = lambda i: _mod_row(i, nlb, bpb, nb)
    return pl.pallas_call(
        _outproj_kernel,
        grid=(d // tn, n_rows // tm),
        in_specs=[pl.BlockSpec((tm, kh), lambda j, i: (i, c1)),
                  pl.BlockSpec((tm, kh), lambda j, i: (i, c2)),
                  pl.BlockSpec((kh, tn), lambda j, i: (0, j)),
                  pl.BlockSpec((kh, tn), lambda j, i: (1, j)),
                  pl.BlockSpec((tm, tn), lambda j, i: (i, j)),
                  pl.BlockSpec((None, 1, tn), lambda j, i: (mrow(i) * 6 + which, 0, j))],
        out_specs=pl.BlockSpec((tm, tn), lambda j, i: (i, j)),
        out_shape=jax.ShapeDtypeStruct((n_rows, d), f32),
        compiler_params=_params(("parallel", "parallel")),
    )(a1, a2, w, w, x, mod)


def _router_kernel(x_ref, g_ref, sc_ref, sh_ref, wr_ref, br_ref, u_ref,
                   h_ref, idx_ref, wt_ref, rank_ref, cnt_ref, carry_scr):
    i = pl.program_id(0)
    tm = x_ref.shape[0]
    ne, ng, pg = N_EXPERTS, N_GROUPS, N_EXPERTS // N_GROUPS

    @pl.when(i == 0)
    def _():
        carry_scr[...] = jnp.zeros_like(carry_scr)

    h = _norm_mod(x_ref[...], g_ref[...], sc_ref[...], sh_ref[...])
    h_ref[...] = h
    logits = lax.dot_general(wr_ref[...], h, (((1,), (1,)), ((), ())), preferred_element_type=f32,
                             precision=lax.Precision.HIGHEST)
    s = jax.nn.sigmoid(logits)
    s3 = s.reshape(ng, pg, tm)
    b3 = (s + br_ref[...]).reshape(ng, pg, tm)
    gi = lax.broadcasted_iota(i32, (ng, pg, tm), 0)
    pi = lax.broadcasted_iota(i32, (ng, pg, tm), 1)
    neg = -jnp.inf
    m1 = jnp.max(b3, axis=1, keepdims=True)
    first = jnp.min(jnp.where(b3 == m1, pi, pg), axis=1, keepdims=True)
    m2 = jnp.max(jnp.where(pi == first, neg, b3), axis=1, keepdims=True)
    gs = m1 + m2
    g2 = lax.broadcasted_iota(i32, (ng, 1, tm), 0)
    gmask = jnp.zeros((ng, 1, tm), jnp.bool_)
    for _ in range(TOPK_GROUPS):
        m = jnp.max(gs, axis=0, keepdims=True)
        fi = jnp.min(jnp.where(gs == m, g2, ng), axis=0, keepdims=True)
        pick = g2 == fi
        gmask = gmask | pick
        gs = jnp.where(pick, neg, gs)
    cand = jnp.where(gmask, b3, neg)
    eid = gi * pg + pi
    sel = jnp.zeros((ng, pg, tm), jnp.bool_)
    picks, idxs, wts = [], [], []
    for _ in range(TOP_K):
        m = jnp.max(jnp.max(cand, axis=0, keepdims=True), axis=1, keepdims=True)
        fi = jnp.min(jnp.min(jnp.where(cand == m, eid, ne), axis=0, keepdims=True), axis=1, keepdims=True)
        pick = eid == fi
        picks.append(pick)
        idxs.append(fi.reshape(1, tm))
        wts.append(jnp.sum(jnp.sum(jnp.where(pick, s3, 0.0), axis=0, keepdims=True), axis=1,
                           keepdims=True).reshape(1, tm))
        sel = sel | pick
        cand = jnp.where(pick, neg, cand)
    wsum = wts[0]
    for k in range(1, TOP_K):
        wsum = wsum + wts[k]
    idx_ref[...] = jnp.concatenate(idxs, axis=0)
    wt_ref[...] = jnp.concatenate([wk / wsum * ROUTED_SCALE for wk in wts], axis=0)
    sel2 = jnp.where(sel, 1.0, 0.0).reshape(ne, tm)
    prefix = (_dot(sel2.astype(bf16), u_ref[...]) + carry_scr[...]).reshape(ng, pg, tm)
    ranks = [jnp.sum(jnp.sum(jnp.where(pk, prefix, 0.0), axis=0, keepdims=True), axis=1,
                     keepdims=True).reshape(1, tm) for pk in picks]
    rank_ref[...] = jnp.concatenate(ranks, axis=0).astype(i32)
    total = carry_scr[...] + jnp.sum(sel2, axis=1, keepdims=True)
    carry_scr[...] = total
    cnt_ref[...] = total


def _router(x, g, mod, which, w_router, b_router, dims, n_rows):
    d = x.shape[1]
    tm = ROW_BLOCK
    ne = N_EXPERTS
    nlb, bpb, nb = dims
    mrow = lambda i: _mod_row(i, nlb, bpb, nb)
    tri = (jnp.arange(tm)[:, None] < jnp.arange(tm)[None, :]).astype(bf16)
    return pl.pallas_call(
        _router_kernel,
        grid=(n_rows // tm,),
        in_specs=[pl.BlockSpec((tm, d), lambda i: (i, 0)),
                  pl.BlockSpec((1, d), lambda i: (0, 0)),
                  pl.BlockSpec((None, 1, d), lambda i: (mrow(i) * 6 + which + 1, 0, 0)),
                  pl.BlockSpec((None, 1, d), lambda i: (mrow(i) * 6 + which, 0, 0)),
                  pl.BlockSpec((ne, d), lambda i: (0, 0)),
                  pl.BlockSpec((ne, 1), lambda i: (0, 0)),
                  pl.BlockSpec((tm, tm), lambda i: (0, 0))],
        out_specs=[pl.BlockSpec((tm, d), lambda i: (i, 0)),
                   pl.BlockSpec((TOP_K, tm), lambda i: (0, i)),
                   pl.BlockSpec((TOP_K, tm), lambda i: (0, i)),
                   pl.BlockSpec((TOP_K, tm), lambda i: (0, i)),
                   pl.BlockSpec((ne, 1), lambda i: (0, 0))],
        out_shape=[jax.ShapeDtypeStruct((n_rows, d), f32),
                   jax.ShapeDtypeStruct((TOP_K, n_rows), i32),
                   jax.ShapeDtypeStruct((TOP_K, n_rows), f32),
                   jax.ShapeDtypeStruct((TOP_K, n_rows), i32),
                   jax.ShapeDtypeStruct((ne, 1), f32)],
        scratch_shapes=[pltpu.VMEM((ne, 1), f32)],
        compiler_params=_params(("arbitrary",)),
    )(x, g.reshape(1, d), mod, mod, w_router.T, b_router.reshape(ne, 1).astype(f32), tri)


EXPERT_BURSTS = 8


def _expert_kernel(be_ref, nu_ref, tok0_ref, tokn_ref, posp_ref, h_hbm, wg_ref, wu_ref, wd_ref, y_hbm,
                   xbuf, ybuf, wgc, wuc, wdc, sem_g, sem_s):
    i = pl.program_id(0)
    nblk = pl.num_programs(0) - 1
    n_used = nu_ref[0]
    slot = i % 2
    mb = MOE_BLOCK
    de = wgc.shape[1]
    trash0 = y_hbm.shape[0] - mb

    def gather_row(idx_ref, r, dst_slot, priority=0):
        pltpu.make_async_copy(h_hbm.at[pl.ds(idx_ref[0, 0, r], 1)], xbuf.at[dst_slot, pl.ds(r, 1)],
                              sem_g.at[dst_slot]).start(priority=priority)

    def scatter_row(r, priority=0):
        p = jnp.where(i == 0, trash0 + r, posp_ref[0, 0, r])
        pltpu.make_async_copy(ybuf.at[1 - slot, pl.ds(r, 1)], y_hbm.at[pl.ds(p, 1)],
                              sem_s.at[0]).start(priority=priority)

    def burst(b):
        per = mb // EXPERT_BURSTS
        for r in range(b * per, (b + 1) * per):
            gather_row(tokn_ref, r, 1 - slot, r % 2)
            scatter_row(r, r % 2)

    @pl.when(i == 0)
    def _():
        ybuf[1] = jnp.zeros((mb, ybuf.shape[2]), f32)

        def body(r, carry):
            gather_row(tok0_ref, r, 0)
            return carry
        lax.fori_loop(0, mb, body, 0)

    pltpu.make_async_copy(h_hbm.at[pl.ds(0, mb)], xbuf.at[slot], sem_g.at[slot]).wait()
    ib = jnp.minimum(i, nblk - 1)
    changed = (i == 0) | (be_ref[ib] != be_ref[jnp.maximum(ib - 1, 0)])

    @pl.when(changed & (i < n_used))
    def _():
        wgc[...] = wg_ref[...].astype(bf16)
        wuc[...] = wu_ref[...].astype(bf16)
        wdc[...] = wd_ref[...].astype(bf16)

    @pl.when(i < n_used)
    def _():
        x = xbuf[slot].astype(bf16)
        nh = 2
        hw = de // nh
        hids = []
        for c in range(nh):
            gate = _dot(x, wgc[:, c * hw:(c + 1) * hw])
            burst(2 * c)
            up = _dot(x, wuc[:, c * hw:(c + 1) * hw])
            burst(2 * c + 1)
            hids.append((_silu(gate) * up).astype(bf16))
        hid = jnp.concatenate(hids, axis=1)
        nd = EXPERT_BURSTS - 2 * nh
        dw = ybuf.shape[2] // nd
        for c in range(nd):
            ybuf[slot, :, c * dw:(c + 1) * dw] = _dot(hid, wdc[:, c * dw:(c + 1) * dw])
            burst(2 * nh + c)

    @pl.when(i >= n_used)
    def _():
        for b in range(EXPERT_BURSTS):
            burst(b)

    pltpu.make_async_copy(ybuf.at[0], y_hbm.at[pl.ds(0, mb)], sem_s.at[0]).wait()

    @pl.when(i == nblk)
    def _():
        pltpu.make_async_copy(h_hbm.at[pl.ds(0, mb)], xbuf.at[1 - slot], sem_g.at[1 - slot]).wait()


def _experts(h, slot_tok, slot_pos, block_e, n_used, wg, wu, wd, layer, n_rows):
    d = h.shape[1]
    nblk = slot_tok.shape[0]
    de = wg.shape[3]
    mb = MOE_BLOCK
    last = nblk - 1
    smem = lambda f: pl.BlockSpec((1, 1, mb), f, memory_space=pltpu.SMEM)
    wspec = lambda shape: pl.BlockSpec((None,) + shape, lambda i, be, nu: (layer, be[jnp.minimum(i, last)], 0, 0))
    grid_spec = pltpu.PrefetchScalarGridSpec(
        num_scalar_prefetch=2,
        grid=(nblk + 1,),
        in_specs=[smem(lambda i, be, nu: (0, 0, 0)),
                  smem(lambda i, be, nu: (jnp.minimum(i + 1, last), 0, 0)),
                  smem(lambda i, be, nu: (jnp.clip(i - 1, 0, last), 0, 0)),
                  pl.BlockSpec(memory_space=pl.ANY),
                  wspec((None, d, de)), wspec((None, d, de)), wspec((None, de, d))],
        out_specs=pl.BlockSpec(memory_space=pl.ANY),
        scratch_shapes=[pltpu.VMEM((2, mb, d), f32), pltpu.VMEM((2, mb, d), f32),
                        pltpu.VMEM((d, de), bf16), pltpu.VMEM((d, de), bf16), pltpu.VMEM((de, d), bf16),
                        pltpu.SemaphoreType.DMA((2,)), pltpu.SemaphoreType.DMA((1,))],
    )
    st = slot_tok.reshape(nblk, 1, mb)
    sp = slot_pos.reshape(nblk, 1, mb)
    return pl.pallas_call(
        _expert_kernel,
        grid_spec=grid_spec,
        out_shape=jax.ShapeDtypeStruct((n_rows * TOP_K + mb, d), f32),
        compiler_params=_params(("arbitrary",)),
    )(block_e, n_used, st, st, sp, h, wg, wu, wd)


def _shared_kernel(h_ref, wg_ref, wu_ref, wd_ref, y_ref):
    x = h_ref[...].astype(bf16)
    hid = _silu(_dot(x, wg_ref[...])) * _dot(x, wu_ref[...])
    y_ref[...] = _dot(hid.astype(bf16), wd_ref[...])


def _shared(h, wg, wu, wd):
    t, d = h.shape
    de = wg.shape[1]
    tm = ROW_BLOCK
    return pl.pallas_call(
        _shared_kernel,
        grid=(t // tm,),
        in_specs=[pl.BlockSpec((tm, d), lambda i: (i, 0)),
                  pl.BlockSpec((d, de), lambda i: (0, 0)),
                  pl.BlockSpec((d, de), lambda i: (0, 0)),
                  pl.BlockSpec((de, d), lambda i: (0, 0))],
        out_specs=pl.BlockSpec((tm, d), lambda i: (i, 0)),
        out_shape=jax.ShapeDtypeStruct((t, d), f32),
        compiler_params=_params(("parallel",)),
    )(h, wg, wu, wd)


COMBINE_ROWS = 64


def _combine_kernel(y_ref, w_ref, sh_ref, x_ref, gate_ref, gf_ref, o_ref, *, final_norm):
    tm, d = x_ref.shape
    y = (y_ref[...] * w_ref[...]).reshape(tm, TOP_K, d)
    acc = sh_ref[...] + jnp.sum(y, axis=1)
    out = x_ref[...] + gate_ref[...] * acc
    if final_norm:
        out = out * lax.rsqrt(jnp.mean(out * out, axis=-1, keepdims=True) + EPS) * gf_ref[...]
    o_ref[...] = out


def _combine(y_rows, wts, shared, x, mod, which, g_final, dims, n_rows, final_norm):
    d = x.shape[1]
    tm = COMBINE_ROWS
    per = ROW_BLOCK // tm
    nlb, bpb, nb = dims
    mrow = lambda i: _mod_row(i // per, nlb, bpb, nb)
    wcol = wts.T.reshape(n_rows * TOP_K, 1)
    return pl.pallas_call(
        functools.partial(_combine_kernel, final_norm=final_norm),
        grid=(n_rows // tm,),
        in_specs=[pl.BlockSpec((tm * TOP_K, d), lambda i: (i, 0)),
                  pl.BlockSpec((tm * TOP_K, 1), lambda i: (i, 0)),
                  pl.BlockSpec((tm, d), lambda i: (i, 0)),
                  pl.BlockSpec((tm, d), lambda i: (i, 0)),
                  pl.BlockSpec((None, 1, d), lambda i: (mrow(i) * 6 + which, 0, 0)),
                  pl.BlockSpec((1, d), lambda i: (0, 0))],
        out_specs=pl.BlockSpec((tm, d), lambda i: (i, 0)),
        out_shape=jax.ShapeDtypeStruct((n_rows, d), f32),
        compiler_params=_params(("parallel",)),
    )(y_rows, wcol, shared, x, mod, g_final.reshape(1, d))


def _moe(x, g, mod, which_shift, w_router, b_router, wg, wu, wd, layer, sg, su, sd, g_final, dims, n_rows,
         final_norm):
    ne = N_EXPERTS
    mb = MOE_BLOCK
    h, idx, wts, rank, cnt = _router(x, g, mod, which_shift, w_router, b_router, dims, n_rows)
    counts = cnt[:, 0].astype(i32)
    padded = (counts + mb - 1) // mb * mb
    pend = jnp.cumsum(padded)
    start = pend - padded
    onehot = idx[:, :, None] == jnp.arange(ne, dtype=i32)[None, None, :]
    dest = jnp.sum(jnp.where(onehot, start[None, None, :], 0), axis=-1) + rank
    n_assign = n_rows * TOP_K
    nblk = (n_assign + ne * (mb - 1) + mb - 1) // mb
    pos = jnp.arange(n_rows, dtype=i32)[None, :] * TOP_K + jnp.arange(TOP_K, dtype=i32)[:, None]
    spare = n_assign + jnp.arange(nblk * mb, dtype=i32) % mb
    slot_pos = spare.at[dest.reshape(-1)].set(pos.reshape(-1), unique_indices=True)
    slot_tok = jnp.where(slot_pos < n_assign, slot_pos // TOP_K, 0)
    blk_start = jnp.arange(nblk, dtype=i32) * mb
    block_e = jnp.minimum(jnp.sum((pend[None, :] <= blk_start[:, None]).astype(i32), axis=1), ne - 1)
    n_used = (pend[-1:] // mb).astype(i32)
    y_rows = _experts(h, slot_tok.reshape(nblk, mb), slot_pos.reshape(nblk, mb), block_e, n_used, wg, wu, wd,
                      layer, n_rows)
    shared = _shared(h, sg, su, sd)
    return _combine(y_rows, wts, shared, x, mod, which_shift + 2, g_final, dims, n_rows, final_norm)


def _rope_tables(seq, head_dim):
    half = head_dim // 2
    quarter = half // 2
    t = jnp.arange(seq, dtype=i32)
    rows = (t // GRID_W).astype(f32)
    cols = (t % GRID_W).astype(f32)
    freqs = ROPE_THETA ** (-jnp.arange(0, half, 2, dtype=f32) / half)
    ar = rows[:, None] * freqs[None, :]
    ac = cols[:, None] * freqs[None, :]
    cos = jnp.concatenate([jnp.cos(ar), jnp.cos(ar), jnp.cos(ac), jnp.cos(ac)], axis=1)
    sin = jnp.concatenate([-jnp.sin(ar), jnp.sin(ar), -jnp.sin(ac), jnp.sin(ac)], axis=1)
    reps = LANE // head_dim
    cos = jnp.tile(cos, (1, reps))
    sin = jnp.tile(sin, (1, reps))
    cos = jnp.concatenate([cos, jnp.ones((ROW_BLOCK, LANE), f32)], axis=0)
    sin = jnp.concatenate([sin, jnp.zeros((ROW_BLOCK, LANE), f32)], axis=0)
    return cos, sin, quarter


def kernel(x, c, ctx, c_ctx, w_ada, b_ada, g_norm_mix, g_norm_ffn, w_in_even, w_out_even, hgrn_lb_logits,
           g_hgrn_norm, win_sink, w_in_odd, w_out_odd, g_q_norm, g_k_norm, w_router, b_router,
           w_exp_gate, w_exp_up, w_exp_down, w_sh_gate, w_sh_up, w_sh_down, g_norm_final):
    nb, seq, d = x.shape
    lc = ctx.shape[1]
    assert seq % ROW_BLOCK == 0 and lc % ROW_BLOCK == 0 and seq % lc == 0
    t_lat = nb * seq
    t_all = t_lat + nb * lc
    bpb = seq // ROW_BLOCK
    nlb = t_lat // ROW_BLOCK
    dims = (nlb, bpb, nb)

    xs = jnp.concatenate([x.reshape(t_lat, d), ctx.reshape(nb * lc, d)], axis=0)
    cc = jnp.concatenate([c, c_ctx[None, :], jnp.zeros((8 - nb - 1, d), f32)], axis=0)

    lb_w = jax.nn.softmax(hgrn_lb_logits.astype(f32), axis=0)
    lower_bounds = jnp.cumsum(lb_w, axis=0)[1:] - lb_w[0]

    mod = _ada(cc, w_ada, b_ada, 0).reshape(8 * 6, 1, d)
    w_in = w_in_even[0].astype(bf16)
    proj = _normmm(xs, g_norm_mix[0], mod, 0, w_in[:, :HGRN_COLS], f32, dims, tn=1280)
    cos, sin, quarter = _rope_tables(seq, WIN_HEAD_DIM)
    epi = dict(blocks={0: (WIN_HEADS + WIN_KV_HEADS) * WIN_HEAD_DIM // LANE}, quarter=quarter, head_norm=False,
               q_slices=WIN_HEADS * WIN_HEAD_DIM // LANE, q_scale=WIN_HEAD_DIM ** -0.5)
    wproj = _normmm(xs, g_norm_mix[0], mod, 0, w_in[:, HGRN_COLS:], bf16, dims, tn=WIN_COLS,
                    epilogue=epi, tables=(cos, sin))
    o_f, o_b = _hgrn(proj, lower_bounds[0], nb, seq, lc)
    a_mix = _hgrn_out(o_f, o_b, proj, g_hgrn_norm[0])
    b_mix = _win_attn(wproj, win_sink[0], nb, seq, lc)
    xs = _outproj(a_mix, 0, b_mix, 0, w_out_even[0].astype(bf16), xs, mod, 2, dims, t_all)
    xs = _moe(xs, g_norm_ffn[0], mod, 3, w_router[0], b_router[0],
              w_exp_gate, w_exp_up, w_exp_down, 0,
              w_sh_gate[0].astype(bf16), w_sh_up[0].astype(bf16), w_sh_down[0].astype(bf16),
              g_norm_final, dims, t_all, False)

    mod = _ada(cc, w_ada, b_ada, 1).reshape(8 * 6, 1, d)
    cos, sin, quarter = _rope_tables(seq, GLB_HEAD_DIM)
    tn = 1536
    n_sl = tn // LANE
    rope_slices = GLB_HEADS + GLB_KV_HEADS
    epi = dict(blocks={0: min(rope_slices, n_sl), 1: max(rope_slices - n_sl, 0)}, quarter=quarter, head_norm=True,
               q_slices=GLB_HEADS, q_scale=GLB_HEAD_DIM ** -0.5)
    qkv = _normmm(xs, g_norm_mix[1], mod, 0, w_in_odd[0].astype(bf16), bf16, dims, tn=tn,
                  epilogue=epi, tables=(cos, sin), gains=(g_q_norm[0], g_k_norm[0]))
    att = _glb_attn(qkv, nb, seq, lc)
    xl = _outproj(att, 0, att, 1, w_out_odd[0].astype(bf16), xs, mod, 2, dims, t_lat)
    out = _moe(xl, g_norm_ffn[1], mod, 3, w_router[1], b_router[1],
               w_exp_gate, w_exp_up, w_exp_down, 1,
               w_sh_gate[1].astype(bf16), w_sh_up[1].astype(bf16), w_sh_down[1].astype(bf16),
               g_norm_final, dims, t_lat, True)
    return out.reshape(nb, seq, d)
```

```python
import functools

import jax
import jax.numpy as jnp
from jax import lax
from jax.experimental import pallas as pl
from jax.experimental.pallas import tpu as pltpu

f32 = jnp.float32
bf16 = jnp.bfloat16
i32 = jnp.int32
u32 = jnp.uint32

EPS = 1e-6
ROPE_THETA = 10000.0
GRID_W = 64

HGRN_DK = 128
HGRN_HEADS = 8
HGRN_CHUNK = 64
HGRN_SUB = 16
HGRN_HEADS_PER_STEP = 4
WIN_HEAD_DIM = 64
WIN_HEADS = 16
WIN_KV_HEADS = 2
WIN_BLOCK = 128
GLB_HEAD_DIM = 128
GLB_HEADS = 16
GLB_KV_HEADS = 4
Q_BLOCK = 128
N_EXPERTS = 64
N_GROUPS = 8
TOPK_GROUPS = 4
TOP_K = 8
ROUTED_SCALE = 2.5
MOE_BLOCK = 256

LANE = 128
ROW_BLOCK = 256
VMEM_LIMIT = 56 * 1024 * 1024

A_K = HGRN_HEADS * HGRN_DK
HGRN_COLS = 5 * A_K
WIN_COLS = WIN_HEADS * WIN_HEAD_DIM + 2 * WIN_KV_HEADS * WIN_HEAD_DIM


def _params(sem, vmem=VMEM_LIMIT):
    return pltpu.CompilerParams(dimension_semantics=sem, vmem_limit_bytes=vmem)


def _silu(x):
    return x * jax.nn.sigmoid(x)


def _dot(a, b):
    return jnp.dot(a, b, preferred_element_type=f32)


def _dot_nt(a, b):
    return lax.dot_general(a, b, (((1,), (1,)), ((), ())), preferred_element_type=f32)


def _pack_halves(x):
    n = x.shape[1] // 2
    lo = lax.bitcast_convert_type(x[:, :n].astype(bf16).astype(f32), u32)
    hi = lax.bitcast_convert_type(x[:, n:].astype(bf16).astype(f32), u32)
    return (lo >> 16) | hi


def _unpack_halves(w):
    lo = lax.bitcast_convert_type(w << 16, f32)
    hi = lax.bitcast_convert_type(w & jnp.uint32(0xFFFF0000), f32)
    return lo, hi


def _dot_tn(a, b):
    return lax.dot_general(a, b, (((0,), (0,)), ((), ())), preferred_element_type=f32)


def _ada_kernel(c_ref, w_ref, b_ref, o_ref):
    a = _silu(c_ref[...]).astype(bf16)
    o_ref[...] = _dot(a, w_ref[...].astype(bf16)) + b_ref[...]


def _ada(c8, w, b, layer, tn=768):
    m, d = c8.shape
    n = w.shape[2]
    return pl.pallas_call(
        _ada_kernel,
        grid=(n // tn,),
        in_specs=[pl.BlockSpec((m, d), lambda j: (0, 0)),
                  pl.BlockSpec((None, d, tn), lambda j: (layer, 0, j)),
                  pl.BlockSpec((None, 1, tn), lambda j: (layer, 0, j))],
        out_specs=pl.BlockSpec((m, tn), lambda j: (0, j)),
        out_shape=jax.ShapeDtypeStruct((m, n), f32),
        compiler_params=_params(("arbitrary",)),
    )(c8, w, b.reshape(b.shape[0], 1, n))


def _norm_mod(x, g, sc, sh):
    y = x * lax.rsqrt(jnp.mean(x * x, axis=-1, keepdims=True) + EPS) * g
    return y * (1.0 + sc) + sh


def _rope_slice(x, cos, sin, quarter):
    lane = lax.broadcasted_iota(i32, x.shape, 1)
    up = pltpu.roll(x, LANE - quarter, axis=1)
    dn = pltpu.roll(x, quarter, axis=1)
    partner = jnp.where(lane % (2 * quarter) < quarter, up, dn)
    return x * cos + partner * sin


def _normmm_kernel(x_ref, g_ref, sc_ref, sh_ref, w_ref, *rest, epilogue, tn):
    if epilogue is None:
        (o_ref,) = rest
    elif epilogue["head_norm"]:
        cos_ref, sin_ref, gq_ref, gk_ref, o_ref = rest
    else:
        cos_ref, sin_ref, o_ref = rest
    j = pl.program_id(0)
    h = _norm_mod(x_ref[...], g_ref[...], sc_ref[...], sh_ref[...]).astype(bf16)
    y = _dot(h, w_ref[...])
    if epilogue is None:
        o_ref[...] = y.astype(o_ref.dtype)
        return

    n_sl = tn // LANE
    for jb in range(epilogue["n_col_blocks"]):
        n_rope = epilogue["blocks"].get(jb, 0)

        @pl.when(j == jb)
        def _(jb=jb, n_rope=n_rope):
            cos = cos_ref[...]
            sin = sin_ref[...]
            outs = []
            for s in range(n_sl):
                ys = y[:, s * LANE:(s + 1) * LANE]
                if s < n_rope:
                    col = jb * n_sl + s
                    is_q = col < epilogue["q_slices"]
                    if epilogue["head_norm"]:
                        gain = gq_ref[...] if is_q else gk_ref[...]
                        ys = ys * lax.rsqrt(jnp.mean(ys * ys, axis=-1, keepdims=True) + EPS) * gain
                    ys = _rope_slice(ys, cos, sin, epilogue["quarter"])
                    if is_q:
                        ys = ys * epilogue["q_scale"]
                outs.append(ys.astype(o_ref.dtype))
            o_ref[...] = jnp.concatenate(outs, axis=1)


def _mod_row(i, n_lat_blocks, blocks_per_batch, n_batch):
    return jnp.where(i < n_lat_blocks, i // blocks_per_batch, n_batch)


def _normmm(x, g, mod, which, w, out_dtype, dims, tn, epilogue=None, tables=None, gains=None):
    t, d = x.shape
    n = w.shape[1]
    tm = ROW_BLOCK
    nlb, bpb, nb = dims
    mrow = lambda i: _mod_row(i, nlb, bpb, nb)
    in_specs = [pl.BlockSpec((tm, d), lambda j, i: (i, 0)),
                pl.BlockSpec((1, d), lambda j, i: (0, 0)),
                pl.BlockSpec((None, 1, d), lambda j, i: (mrow(i) * 6 + which + 1, 0, 0)),
                pl.BlockSpec((None, 1, d), lambda j, i: (mrow(i) * 6 + which, 0, 0)),
                pl.BlockSpec((d, tn), lambda j, i: (0, j))]
    args = [x, g.reshape(1, d), mod, mod, w]
    if epilogue is not None:
        cos, sin = tables
        tab = lambda j, i: (jnp.where(i < nlb, i % bpb, bpb), 0)
        in_specs += [pl.BlockSpec((tm, LANE), tab), pl.BlockSpec((tm, LANE), tab)]
        args += [cos, sin]
        epilogue = dict(epilogue, n_col_blocks=n // tn)
        if epilogue["head_norm"]:
            in_specs += [pl.BlockSpec((1, LANE), lambda j, i: (0, 0))] * 2
            args += [gains[0].reshape(1, LANE), gains[1].reshape(1, LANE)]
    return pl.pallas_call(
        functools.partial(_normmm_kernel, epilogue=epilogue, tn=tn),
        grid=(n // tn, t // tm),
        in_specs=in_specs,
        out_specs=pl.BlockSpec((tm, tn), lambda j, i: (i, j)),
        out_shape=jax.ShapeDtypeStruct((t, n), out_dtype),
        compiler_params=_params(("parallel", "parallel")),
    )(*args)


def _gla_chunk(q_raw, v, f_raw, lb, st_ref, a_scr, rev):
    c = HGRN_CHUNK
    sub = HGRN_SUB
    q = _silu(q_raw)
    f = lb + (1.0 - lb) * jax.nn.sigmoid(f_raw)
    kk = 1.0 - f
    g = jnp.log(f)
    row = lax.broadcasted_iota(i32, (c, c), 0)
    col = lax.broadcasted_iota(i32, (c, c), 1)
    tri = (col >= row) if rev else (col <= row)
    b = jnp.dot(tri.astype(f32), g, preferred_element_type=f32, precision=lax.Precision.HIGHEST)
    b_end = b[0:1] if rev else b[c - 1:c]
    st = st_ref[...]
    o = _dot_nt((q * jnp.exp(b)).astype(bf16), st.astype(bf16))

    vb = v.astype(bf16)
    nsub = c // sub
    lane16 = lax.broadcasted_iota(i32, (sub, sub), 1)
    row16 = lax.broadcasted_iota(i32, (sub, sub), 0)
    diag_ok = (row16 <= lane16) if rev else (row16 >= lane16)
    rsub = lax.broadcasted_iota(i32, (c, sub), 0) // sub
    for jb in range(nsub):
        js = slice(jb * sub, (jb + 1) * sub)
        m_j = b[jb * sub:jb * sub + 1] if rev else b[(jb + 1) * sub - 1:(jb + 1) * sub]
        kd = kk[js] * jnp.exp(m_j - b[js])
        qd = q * jnp.exp(jnp.minimum(b - m_j, 0.0))
        a_col = _dot_nt(qd.astype(bf16), kd.astype(bf16))
        off_ok = (rsub < jb) if rev else (rsub > jb)
        a_col = jnp.where(off_ok, a_col, 0.0)
        qi, bi, ki = q[js], b[js], kk[js]
        a_dd = jnp.zeros((sub, sub), f32)
        for s in range(sub):
            e = jnp.exp(jnp.minimum(bi - bi[s:s + 1], 0.0))
            col_s = jnp.sum(qi * ki[s:s + 1] * e, axis=-1, keepdims=True)
            a_dd = jnp.where(lane16 == s, col_s, a_dd)
        a_dd = jnp.where(diag_ok, a_dd, 0.0)
        pieces = [a_dd if ib == jb else a_col[ib * sub:(ib + 1) * sub] for ib in range(nsub)]
        a_scr[:, js] = jnp.concatenate(pieces, axis=0)
    o = o + _dot(a_scr[...].astype(bf16), vb)
    kdec = kk * jnp.exp(b_end - b)
    st_ref[...] = jnp.exp(b_end) * st + _dot_tn(vb, kdec.astype(bf16))
    return o


def _hgrn_kernel(qf_ref, vf_ref, ff_ref, qb_ref, vb_ref, fb_ref, lb_ref, of_ref, ob_ref,
                 stf_ref, stb_ref, af_scr, ab_scr):
    @pl.when(pl.program_id(2) == 0)
    def _():
        stf_ref[...] = jnp.zeros_like(stf_ref)
        stb_ref[...] = jnp.zeros_like(stb_ref)

    for j in range(HGRN_HEADS_PER_STEP):
        sl = slice(j * HGRN_DK, (j + 1) * HGRN_DK)
        lb = lb_ref[:, sl]
        of_ref[:, sl] = _gla_chunk(qf_ref[:, sl], vf_ref[:, sl], ff_ref[:, sl], lb, stf_ref.at[j], af_scr.at[j], False)
        ob_ref[:, sl] = _gla_chunk(qb_ref[:, sl], vb_ref[:, sl], fb_ref[:, sl], lb, stb_ref.at[j], ab_scr.at[j], True)


def _hgrn(proj, lb, n_batch, seq, ctx_len):
    t = proj.shape[0]
    c = HGRN_CHUNK
    hp = HGRN_HEADS_PER_STEP
    wblk = hp * HGRN_DK
    ngrp = HGRN_HEADS // hp
    ncc, ncl = ctx_len // c, seq // c
    lat_chunks = n_batch * ncl

    def fwd_blk(b, s):
        return jnp.where(s < ncc, lat_chunks + b * ncc + s, b * ncl + (s - ncc))

    def bwd_blk(b, s):
        return jnp.where(s < ncc, lat_chunks + b * ncc + (ncc - 1 - s), b * ncl + (ncl - 1 - (s - ncc)))

    def spec(blk, section):
        return pl.BlockSpec((c, wblk), lambda b, h, s: (blk(b, s), section * ngrp + h))

    def ospec(blk):
        return pl.BlockSpec((c, wblk), lambda b, h, s: (blk(b, s), h))

    out = jax.ShapeDtypeStruct((t, A_K), f32)
    return pl.pallas_call(
        _hgrn_kernel,
        grid=(n_batch, ngrp, ncc + ncl),
        in_specs=[spec(fwd_blk, 0), spec(fwd_blk, 1), spec(fwd_blk, 2),
                  spec(bwd_blk, 0), spec(bwd_blk, 1), spec(bwd_blk, 3),
                  pl.BlockSpec((1, wblk), lambda b, h, s: (0, h))],
        out_specs=[ospec(fwd_blk), ospec(bwd_blk)],
        out_shape=[out, out],
        scratch_shapes=[pltpu.VMEM((hp, HGRN_DK, HGRN_DK), f32), pltpu.VMEM((hp, HGRN_DK, HGRN_DK), f32),
                        pltpu.VMEM((hp, c, c), f32), pltpu.VMEM((hp, c, c), f32)],
        compiler_params=_params(("parallel", "parallel", "arbitrary")),
    )(proj, proj, proj, proj, proj, proj, lb.reshape(1, A_K))


def _hgrn_out_kernel(of_ref, ob_ref, g_ref, gain_ref, o_ref):
    gain = gain_ref[...]
    outs = []
    for h in range(HGRN_HEADS):
        sl = slice(h * HGRN_DK, (h + 1) * HGRN_DK)
        o = of_ref[:, sl] + ob_ref[:, sl]
        o = o * lax.rsqrt(jnp.mean(o * o, axis=-1, keepdims=True) + EPS) * gain
        outs.append((o * _silu(g_ref[:, sl])).astype(bf16))
    o_ref[...] = jnp.concatenate(outs, axis=1)


def _hgrn_out(o_f, o_b, proj, gain):
    t = o_f.shape[0]
    tm = ROW_BLOCK
    return pl.pallas_call(
        _hgrn_out_kernel,
        grid=(t // tm,),
        in_specs=[pl.BlockSpec((tm, A_K), lambda i: (i, 0)),
                  pl.BlockSpec((tm, A_K), lambda i: (i, 0)),
                  pl.BlockSpec((tm, A_K), lambda i: (i, 4)),
                  pl.BlockSpec((1, HGRN_DK), lambda i: (0, 0))],
        out_specs=pl.BlockSpec((tm, A_K), lambda i: (i, 0)),
        out_shape=jax.ShapeDtypeStruct((t, A_K), bf16),
        compiler_params=_params(("parallel",)),
    )(o_f, o_b, proj, gain.reshape(1, HGRN_DK))


def _win_kernel(sink_ref, q_ref, kp_ref, kc_ref, kn_ref, vp_ref, vc_ref, vn_ref, kx_ref, vx_ref, o_ref,
                *, n_lat_blocks):
    n = pl.program_id(1)
    w = WIN_BLOCK
    dh = WIN_HEAD_DIM
    is_lat = n < n_lat_blocks
    ri = lax.broadcasted_iota(i32, (w, w), 0)
    ci = lax.broadcasted_iota(i32, (w, w), 1)
    ok_p = (ci >= ri) & is_lat & (n > 0)
    ok_c = jnp.broadcast_to(is_lat, (w, w))
    ok_n = (ci <= ri) & is_lat & (n < n_lat_blocks - 1)
    valid = jnp.concatenate([ok_p, ok_c, ok_n], axis=1)
    kwin = jnp.concatenate([kp_ref[...], kc_ref[...], kn_ref[...]], axis=0)
    vwin = jnp.concatenate([vp_ref[...], vc_ref[...], vn_ref[...]], axis=0)
    kx = kx_ref[...]
    vx = vx_ref[...]
    g = WIN_HEADS // WIN_KV_HEADS
    outs = []
    for h in range(WIN_HEADS):
        kv = h // g
        ks = slice(kv * dh, (kv + 1) * dh)
        qh = q_ref[:, h * dh:(h + 1) * dh]
        sw = jnp.where(valid, _dot_nt(qh, kwin[:, ks]), -jnp.inf)
        sx = _dot_nt(qh, kx[:, ks])
        sk = sink_ref[h]
        m = jnp.maximum(jnp.maximum(jnp.max(sw, axis=-1, keepdims=True), jnp.max(sx, axis=-1, keepdims=True)), sk)
        pw = jnp.exp(sw - m)
        px = jnp.exp(sx - m)
        den = jnp.sum(pw, axis=-1, keepdims=True) + jnp.sum(px, axis=-1, keepdims=True) + jnp.exp(sk - m)
        o = _dot(pw.astype(bf16), vwin[:, ks]) + _dot(px.astype(bf16), vx[:, ks])
        outs.append((o / den).astype(bf16))
    o_ref[...] = jnp.concatenate(outs, axis=1)


def _win_attn(wproj, sink, n_batch, seq, ctx_len):
    t = wproj.shape[0]
    w = WIN_BLOCK
    nlb = seq // w
    ncb = ctx_len // w
    lat_blocks = n_batch * nlb
    qc = WIN_HEADS * WIN_HEAD_DIM // LANE
    kcol, vcol = qc, qc + 1

    def qrow(b, n):
        return jnp.where(n < nlb, b * nlb + n, lat_blocks + b * ncb + (n - nlb))

    def krow(off):
        def f(b, n):
            return b * nlb + jnp.clip(n + off, 0, nlb - 1)
        return f

    def kspec(off, colblk):
        return pl.BlockSpec((w, LANE), lambda b, n: (krow(off)(b, n), colblk))

    def xspec(colblk):
        return pl.BlockSpec((ctx_len, LANE), lambda b, n: ((n_batch * seq) // ctx_len + b, colblk))

    return pl.pallas_call(
        functools.partial(_win_kernel, n_lat_blocks=nlb),
        grid=(n_batch, nlb + ncb),
        in_specs=[pl.BlockSpec(memory_space=pltpu.SMEM),
                  pl.BlockSpec((w, qc * LANE), lambda b, n: (qrow(b, n), 0)),
                  kspec(-1, kcol), kspec(0, kcol), kspec(1, kcol),
                  kspec(-1, vcol), kspec(0, vcol), kspec(1, vcol),
                  xspec(kcol), xspec(vcol)],
        out_specs=pl.BlockSpec((w, qc * LANE), lambda b, n: (qrow(b, n), 0)),
        out_shape=jax.ShapeDtypeStruct((t, qc * LANE), bf16),
        compiler_params=_params(("parallel", "arbitrary")),
    )(sink.astype(f32), wproj, wproj, wproj, wproj, wproj, wproj, wproj, wproj, wproj)


GLB_KEY_CHUNK = 512


def _glb_kernel(q_ref, kx_ref, kl_ref, vx_ref, vl_ref, o_ref, m_scr, l_scr, acc_scr):
    dh = GLB_HEAD_DIM
    g = GLB_HEADS // GLB_KV_HEADS
    ck = GLB_KEY_CHUNK
    tq = q_ref.shape[0]
    q = jnp.concatenate([q_ref[:, i * dh:(i + 1) * dh] for i in range(g)], axis=0)

    def update(k, v, first):
        s = _dot_nt(q, k)
        reps = s.shape[1] // LANE
        mx = jnp.broadcast_to(jnp.max(s, axis=-1, keepdims=True), (g * tq, LANE))
        if first:
            m_new = mx
            p = jnp.exp(s - jnp.concatenate([m_new] * reps, axis=1))
            l_scr[...] = jnp.broadcast_to(jnp.sum(p, axis=-1, keepdims=True), (g * tq, LANE))
            acc_scr[...] = _dot(p.astype(bf16), v)
        else:
            m_old = m_scr[...]
            m_new = jnp.maximum(m_old, mx)
            alpha = jnp.exp(m_old - m_new)
            p = jnp.exp(s - jnp.concatenate([m_new] * reps, axis=1))
            l_scr[...] = alpha * l_scr[...] + jnp.broadcast_to(jnp.sum(p, axis=-1, keepdims=True), (g * tq, LANE))
            acc_scr[...] = alpha * acc_scr[...] + _dot(p.astype(bf16), v)
        m_scr[...] = m_new

    chunks = [(kx_ref, vx_ref, c, min(ck, kx_ref.shape[0] - c)) for c in range(0, kx_ref.shape[0], ck)]
    chunks += [(kl_ref, vl_ref, c, min(ck, kl_ref.shape[0] - c)) for c in range(0, kl_ref.shape[0], ck)]
    for n, (kr, vr, c0, sz) in enumerate(chunks):
        update(kr[c0:c0 + sz, :], vr[c0:c0 + sz, :], n == 0)
    o = acc_scr[...] / l_scr[...]
    o_ref[...] = jnp.concatenate([o[i * tq:(i + 1) * tq] for i in range(g)], axis=1).astype(bf16)


def _glb_attn(qkv, n_batch, seq, ctx_len):
    tq = Q_BLOCK
    dh = GLB_HEAD_DIM
    g = GLB_HEADS // GLB_KV_HEADS
    nq = seq // tq
    kcol0 = GLB_HEADS
    vcol0 = GLB_HEADS + GLB_KV_HEADS
    ctx_blk0 = (n_batch * seq) // ctx_len
    return pl.pallas_call(
        _glb_kernel,
        grid=(n_batch, GLB_KV_HEADS, nq),
        in_specs=[pl.BlockSpec((tq, g * dh), lambda b, k, i: (b * nq + i, k)),
                  pl.BlockSpec((ctx_len, dh), lambda b, k, i: (ctx_blk0 + b, kcol0 + k)),
                  pl.BlockSpec((seq, dh), lambda b, k, i: (b, kcol0 + k)),
                  pl.BlockSpec((ctx_len, dh), lambda b, k, i: (ctx_blk0 + b, vcol0 + k)),
                  pl.BlockSpec((seq, dh), lambda b, k, i: (b, vcol0 + k))],
        out_specs=pl.BlockSpec((tq, g * dh), lambda b, k, i: (b * nq + i, k)),
        out_shape=jax.ShapeDtypeStruct((n_batch * seq, GLB_HEADS * dh), bf16),
        scratch_shapes=[pltpu.VMEM((g * tq, LANE), f32), pltpu.VMEM((g * tq, LANE), f32), pltpu.VMEM((g * tq, dh), f32)],
        compiler_params=_params(("parallel", "parallel", "arbitrary")),
    )(qkv, qkv, qkv, qkv, qkv)


def _outproj_kernel(a1_ref, a2_ref, w1_ref, w2_ref, x_ref, gate_ref, o_ref):
    y = _dot(a1_ref[...], w1_ref[...]) + _dot(a2_ref[...], w2_ref[...])
    o_ref[...] = x_ref[...] + gate_ref[...] * y


def _outproj(a1, c1, a2, c2, w, x, mod, which, dims, n_rows, tn=1024):
    d = x.shape[1]
    kh = w.shape[0] // 2
    tm = ROW_BLOCK
    nlb, bpb, nb = dims
    mrow = lambda i: _mod_row(i, nlb, bpb, nb)
    return pl.pallas_call(
        _outproj_kernel,
        grid=(d // tn, n_rows // tm),
        in_specs=[pl.BlockSpec((tm, kh), lambda j, i: (i, c1)),
                  pl.BlockSpec((tm, kh), lambda j, i: (i, c2)),
                  pl.BlockSpec((kh, tn), lambda j, i: (0, j)),
                  pl.BlockSpec((kh, tn), lambda j, i: (1, j)),
                  pl.BlockSpec((tm, tn), lambda j, i: (i, j)),
                  pl.BlockSpec((None, 1, tn), lambda j, i: (mrow(i) * 6 + which, 0, j))],
        out_specs=pl.BlockSpec((tm, tn), lambda j, i: (i, j)),
        out_shape=jax.ShapeDtypeStruct((n_rows, d), f32),
        compiler_params=_params(("parallel", "parallel")),
    )(a1, a2, w, w, x, mod)


def _router_kernel(x_ref, g_ref, sc_ref, sh_ref, wr_ref, br_ref, u_ref,
                   h_ref, idx_ref, wt_ref, rank_ref, cnt_ref, carry_scr):
    i = pl.program_id(0)
    tm = x_ref.shape[0]
    ne, ng, pg = N_EXPERTS, N_GROUPS, N_EXPERTS // N_GROUPS

    @pl.when(i == 0)
    def _():
        carry_scr[...] = jnp.zeros_like(carry_scr)

    h = _norm_mod(x_ref[...], g_ref[...], sc_ref[...], sh_ref[...])
    h_ref[...] = _pack_halves(h)
    logits = lax.dot_general(wr_ref[...], h, (((1,), (1,)), ((), ())), preferred_element_type=f32,
                             precision=lax.Precision.HIGHEST)
    s = jax.nn.sigmoid(logits)
    s3 = s.reshape(ng, pg, tm)
    b3 = (s + br_ref[...]).reshape(ng, pg, tm)
    gi = lax.broadcasted_iota(i32, (ng, pg, tm), 0)
    pi = lax.broadcasted_iota(i32, (ng, pg, tm), 1)
    neg = -jnp.inf
    m1 = jnp.max(b3, axis=1, keepdims=True)
    first = jnp.min(jnp.where(b3 == m1, pi, pg), axis=1, keepdims=True)
    m2 = jnp.max(jnp.where(pi == first, neg, b3), axis=1, keepdims=True)
    gs = m1 + m2
    g2 = lax.broadcasted_iota(i32, (ng, 1, tm), 0)
    gmask = jnp.zeros((ng, 1, tm), jnp.bool_)
    for _ in range(TOPK_GROUPS):
        m = jnp.max(gs, axis=0, keepdims=True)
        fi = jnp.min(jnp.where(gs == m, g2, ng), axis=0, keepdims=True)
        pick = g2 == fi
        gmask = gmask | pick
        gs = jnp.where(pick, neg, gs)
    cand = jnp.where(gmask, b3, neg)
    eid = gi * pg + pi
    sel = jnp.zeros((ng, pg, tm), jnp.bool_)
    picks, idxs, wts = [], [], []
    for _ in range(TOP_K):
        m = jnp.max(jnp.max(cand, axis=0, keepdims=True), axis=1, keepdims=True)
        fi = jnp.min(jnp.min(jnp.where(cand == m, eid, ne), axis=0, keepdims=True), axis=1, keepdims=True)
        pick = eid == fi
        picks.append(pick)
        idxs.append(fi.reshape(1, tm))
        wts.append(jnp.sum(jnp.sum(jnp.where(pick, s3, 0.0), axis=0, keepdims=True), axis=1,
                           keepdims=True).reshape(1, tm))
        sel = sel | pick
        cand = jnp.where(pick, neg, cand)
    wsum = wts[0]
    for k in range(1, TOP_K):
        wsum = wsum + wts[k]
    idx_ref[...] = jnp.concatenate(idxs, axis=0)
    wt_ref[...] = jnp.concatenate([wk / wsum * ROUTED_SCALE for wk in wts], axis=0)
    sel2 = jnp.where(sel, 1.0, 0.0).reshape(ne, tm)
    prefix = (_dot(sel2.astype(bf16), u_ref[...]) + carry_scr[...]).reshape(ng, pg, tm)
    ranks = [jnp.sum(jnp.sum(jnp.where(pk, prefix, 0.0), axis=0, keepdims=True), axis=1,
                     keepdims=True).reshape(1, tm) for pk in picks]
    rank_ref[...] = jnp.concatenate(ranks, axis=0).astype(i32)
    total = carry_scr[...] + jnp.sum(sel2, axis=1, keepdims=True)
    carry_scr[...] = total
    cnt_ref[...] = total


def _router(x, g, mod, which, w_router, b_router, dims, n_rows):
    d = x.shape[1]
    tm = ROW_BLOCK
    ne = N_EXPERTS
    nlb, bpb, nb = dims
    mrow = lambda i: _mod_row(i, nlb, bpb, nb)
    tri = (jnp.arange(tm)[:, None] < jnp.arange(tm)[None, :]).astype(bf16)
    return pl.pallas_call(
        _router_kernel,
        grid=(n_rows // tm,),
        in_specs=[pl.BlockSpec((tm, d), lambda i: (i, 0)),
                  pl.BlockSpec((1, d), lambda i: (0, 0)),
                  pl.BlockSpec((None, 1, d), lambda i: (mrow(i) * 6 + which + 1, 0, 0)),
                  pl.BlockSpec((None, 1, d), lambda i: (mrow(i) * 6 + which, 0, 0)),
                  pl.BlockSpec((ne, d), lambda i: (0, 0)),
                  pl.BlockSpec((ne, 1), lambda i: (0, 0)),
                  pl.BlockSpec((tm, tm), lambda i: (0, 0))],
        out_specs=[pl.BlockSpec((tm, d // 2), lambda i: (i, 0)),
                   pl.BlockSpec((TOP_K, tm), lambda i: (0, i)),
                   pl.BlockSpec((TOP_K, tm), lambda i: (0, i)),
                   pl.BlockSpec((TOP_K, tm), lambda i: (0, i)),
                   pl.BlockSpec((ne, 1), lambda i: (0, 0))],
        out_shape=[jax.ShapeDtypeStruct((n_rows, d // 2), u32),
                   jax.ShapeDtypeStruct((TOP_K, n_rows), i32),
                   jax.ShapeDtypeStruct((TOP_K, n_rows), f32),
                   jax.ShapeDtypeStruct((TOP_K, n_rows), i32),
                   jax.ShapeDtypeStruct((ne, 1), f32)],
        scratch_shapes=[pltpu.VMEM((ne, 1), f32)],
        compiler_params=_params(("arbitrary",)),
    )(x, g.reshape(1, d), mod, mod, w_router.T, b_router.reshape(ne, 1).astype(f32), tri)


EXPERT_BURSTS = 8


def _expert_kernel(be_ref, nu_ref, tok0_ref, tokn_ref, posp_ref, h_hbm, wg_ref, wu_ref, wd_ref, y_hbm,
                   xbuf, ybuf, wgc, wuc, wdc, sem_g, sem_s):
    i = pl.program_id(0)
    nblk = pl.num_programs(0) - 1
    n_used = nu_ref[0]
    slot = i % 2
    mb = MOE_BLOCK
    de = wgc.shape[1]
    dp = ybuf.shape[2]
    spare0 = y_hbm.shape[0] - 2 * mb
    per = mb // (EXPERT_BURSTS // 2)

    def gather_row(idx_ref, r, dst_slot, priority=0):
        pltpu.make_async_copy(h_hbm.at[pl.ds(idx_ref[0, 0, r], 1)], xbuf.at[dst_slot, pl.ds(r, 1)],
                              sem_g.at[dst_slot]).start(priority=priority)

    def gather_burst(b):
        for r in range(b * per, (b + 1) * per):
            gather_row(tokn_ref, r, 1 - slot, r % 2)

    def scatter_burst(b):
        for r in range(b * per, (b + 1) * per):
            p = jnp.where(i == 0, spare0 + r, posp_ref[0, 0, r])
            pltpu.make_async_copy(ybuf.at[1 - slot, pl.ds(r, 1)], y_hbm.at[pl.ds(p, 1)],
                                  sem_s.at[0]).start(priority=r % 2)

    def wait_scatter():
        pltpu.make_async_copy(ybuf.at[0], y_hbm.at[pl.ds(0, mb)], sem_s.at[0]).wait()

    @pl.when(i == 0)
    def _():
        ybuf[1] = jnp.zeros((mb, dp), u32)

        def body(r, carry):
            gather_row(tok0_ref, r, 0)
            pltpu.make_async_copy(ybuf.at[1, pl.ds(r, 1)], y_hbm.at[pl.ds(spare0 + mb + r, 1)], sem_s.at[0]).start()
            return carry
        lax.fori_loop(0, mb, body, 0)

    pltpu.make_async_copy(h_hbm.at[pl.ds(0, mb)], xbuf.at[slot], sem_g.at[slot]).wait()
    ib = jnp.minimum(i, nblk - 1)
    changed = (i == 0) | (be_ref[ib] != be_ref[jnp.maximum(ib - 1, 0)])

    @pl.when(changed & (i < n_used))
    def _():
        wgc[...] = wg_ref[...].astype(bf16)
        wuc[...] = wu_ref[...].astype(bf16)
        wdc[...] = wd_ref[...].astype(bf16)

    @pl.when(i < n_used)
    def _():
        lo, hi = _unpack_halves(xbuf[slot])
        x = jnp.concatenate([lo.astype(bf16), hi.astype(bf16)], axis=1)
        nh = EXPERT_BURSTS // 4
        hw = de // nh
        hids = []
        for c in range(nh):
            gate = _dot(x, wgc[:, c * hw:(c + 1) * hw])
            gather_burst(2 * c)
            up = _dot(x, wuc[:, c * hw:(c + 1) * hw])
            gather_burst(2 * c + 1)
            hids.append((_silu(gate) * up).astype(bf16))
        hid = jnp.concatenate(hids, axis=1)
        wait_scatter()
        nd = EXPERT_BURSTS // 2
        dw = dp // nd
        for c in range(nd):
            wd_c = jnp.concatenate([wdc[:, c * dw:(c + 1) * dw], wdc[:, dp + c * dw:dp + (c + 1) * dw]], axis=1)
            ybuf[slot, :, c * dw:(c + 1) * dw] = _pack_halves(_dot(hid, wd_c))
            scatter_burst(c)

    @pl.when(i >= n_used)
    def _():
        wait_scatter()
        for b in range(EXPERT_BURSTS // 2):
            gather_burst(b)
            scatter_burst(b)

    @pl.when(i == nblk)
    def _():
        wait_scatter()
        pltpu.make_async_copy(h_hbm.at[pl.ds(0, mb)], xbuf.at[1 - slot], sem_g.at[1 - slot]).wait()


def _experts(h, slot_tok, slot_pos, block_e, n_used, wg, wu, wd, layer, n_rows):
    dp = h.shape[1]
    d = wg.shape[2]
    nblk = slot_tok.shape[0]
    de = wg.shape[3]
    mb = MOE_BLOCK
    last = nblk - 1
    smem = lambda f: pl.BlockSpec((1, 1, mb), f, memory_space=pltpu.SMEM)
    wspec = lambda shape: pl.BlockSpec((None,) + shape, lambda i, be, nu: (layer, be[jnp.minimum(i, last)], 0, 0))
    grid_spec = pltpu.PrefetchScalarGridSpec(
        num_scalar_prefetch=2,
        grid=(nblk + 1,),
        in_specs=[smem(lambda i, be, nu: (0, 0, 0)),
                  smem(lambda i, be, nu: (jnp.minimum(i + 1, last), 0, 0)),
                  smem(lambda i, be, nu: (jnp.clip(i - 1, 0, last), 0, 0)),
                  pl.BlockSpec(memory_space=pl.ANY),
                  wspec((None, d, de)), wspec((None, d, de)), wspec((None, de, d))],
        out_specs=pl.BlockSpec(memory_space=pl.ANY),
        scratch_shapes=[pltpu.VMEM((2, mb, dp), u32), pltpu.VMEM((2, mb, dp), u32),
                        pltpu.VMEM((d, de), bf16), pltpu.VMEM((d, de), bf16), pltpu.VMEM((de, d), bf16),
                        pltpu.SemaphoreType.DMA((2,)), pltpu.SemaphoreType.DMA((1,))],
    )
    st = slot_tok.reshape(nblk, 1, mb)
    sp = slot_pos.reshape(nblk, 1, mb)
    return pl.pallas_call(
        _expert_kernel,
        grid_spec=grid_spec,
        out_shape=jax.ShapeDtypeStruct((n_rows * TOP_K + 2 * mb, dp), u32),
        compiler_params=_params(("arbitrary",)),
    )(block_e, n_used, st, st, sp, h, wg, wu, wd)


def _shared_kernel(h_ref, wg_ref, wu_ref, wd_ref, y_ref):
    lo, hi = _unpack_halves(h_ref[...])
    x = jnp.concatenate([lo.astype(bf16), hi.astype(bf16)], axis=1)
    hid = _silu(_dot(x, wg_ref[...])) * _dot(x, wu_ref[...])
    y_ref[...] = _dot(hid.astype(bf16), wd_ref[...])


def _shared(h, wg, wu, wd):
    t, dp = h.shape
    d, de = wg.shape
    tm = ROW_BLOCK
    return pl.pallas_call(
        _shared_kernel,
        grid=(t // tm,),
        in_specs=[pl.BlockSpec((tm, dp), lambda i: (i, 0)),
                  pl.BlockSpec((d, de), lambda i: (0, 0)),
                  pl.BlockSpec((d, de), lambda i: (0, 0)),
                  pl.BlockSpec((de, d), lambda i: (0, 0))],
        out_specs=pl.BlockSpec((tm, d), lambda i: (i, 0)),
        out_shape=jax.ShapeDtypeStruct((t, d), f32),
        compiler_params=_params(("parallel",)),
    )(h, wg, wu, wd)


COMBINE_ROWS = 64


def _combine_kernel(y_ref, w_ref, sh_ref, x_ref, gate_ref, gf_ref, o_ref, *, final_norm):
    tm, d = x_ref.shape
    w = w_ref[...]
    halves = [jnp.sum((part * w).reshape(tm, TOP_K, d // 2), axis=1)
              for part in _unpack_halves(y_ref[...])]
    acc = sh_ref[...] + jnp.concatenate(halves, axis=1)
    out = x_ref[...] + gate_ref[...] * acc
    if final_norm:
        out = out * lax.rsqrt(jnp.mean(out * out, axis=-1, keepdims=True) + EPS) * gf_ref[...]
    o_ref[...] = out


def _combine(y_rows, wts, shared, x, mod, which, g_final, dims, n_rows, final_norm):
    d = x.shape[1]
    tm = COMBINE_ROWS
    per = ROW_BLOCK // tm
    nlb, bpb, nb = dims
    mrow = lambda i: _mod_row(i // per, nlb, bpb, nb)
    wcol = wts.T.reshape(n_rows * TOP_K, 1)
    return pl.pallas_call(
        functools.partial(_combine_kernel, final_norm=final_norm),
        grid=(n_rows // tm,),
        in_specs=[pl.BlockSpec((tm * TOP_K, d // 2), lambda i: (i, 0)),
                  pl.BlockSpec((tm * TOP_K, 1), lambda i: (i, 0)),
                  pl.BlockSpec((tm, d), lambda i: (i, 0)),
                  pl.BlockSpec((tm, d), lambda i: (i, 0)),
                  pl.BlockSpec((None, 1, d), lambda i: (mrow(i) * 6 + which, 0, 0)),
                  pl.BlockSpec((1, d), lambda i: (0, 0))],
        out_specs=pl.BlockSpec((tm, d), lambda i: (i, 0)),
        out_shape=jax.ShapeDtypeStruct((n_rows, d), f32),
        compiler_params=_params(("parallel",)),
    )(y_rows, wcol, shared, x, mod, g_final.reshape(1, d))


def _moe(x, g, mod, which_shift, w_router, b_router, wg, wu, wd, layer, sg, su, sd, g_final, dims, n_rows,
         final_norm):
    ne = N_EXPERTS
    mb = MOE_BLOCK
    h, idx, wts, rank, cnt = _router(x, g, mod, which_shift, w_router, b_router, dims, n_rows)
    counts = cnt[:, 0].astype(i32)
    padded = (counts + mb - 1) // mb * mb
    pend = jnp.cumsum(padded)
    start = pend - padded
    onehot = idx[:, :, None] == jnp.arange(ne, dtype=i32)[None, None, :]
    dest = jnp.sum(jnp.where(onehot, start[None, None, :], 0), axis=-1) + rank
    n_assign = n_rows * TOP_K
    nblk = (n_assign + ne * (mb - 1) + mb - 1) // mb
    pos = jnp.arange(n_rows, dtype=i32)[None, :] * TOP_K + jnp.arange(TOP_K, dtype=i32)[:, None]
    spare = n_assign + jnp.arange(nblk * mb, dtype=i32) % mb
    slot_pos = spare.at[dest.reshape(-1)].set(pos.reshape(-1), unique_indices=True)
    slot_tok = jnp.where(slot_pos < n_assign, slot_pos // TOP_K, 0)
    blk_start = jnp.arange(nblk, dtype=i32) * mb
    block_e = jnp.minimum(jnp.sum((pend[None, :] <= blk_start[:, None]).astype(i32), axis=1), ne - 1)
    n_used = (pend[-1:] // mb).astype(i32)
    y_rows = _experts(h, slot_tok.reshape(nblk, mb), slot_pos.reshape(nblk, mb), block_e, n_used, wg, wu, wd,
                      layer, n_rows)
    shared = _shared(h, sg, su, sd)
    return _combine(y_rows, wts, shared, x, mod, which_shift + 2, g_final, dims, n_rows, final_norm)


def _rope_tables(seq, head_dim):
    half = head_dim // 2
    quarter = half // 2
    t = jnp.arange(seq, dtype=i32)
    rows = (t // GRID_W).astype(f32)
    cols = (t % GRID_W).astype(f32)
    freqs = ROPE_THETA ** (-jnp.arange(0, half, 2, dtype=f32) / half)
    ar = rows[:, None] * freqs[None, :]
    ac = cols[:, None] * freqs[None, :]
    cos = jnp.concatenate([jnp.cos(ar), jnp.cos(ar), jnp.cos(ac), jnp.cos(ac)], axis=1)
    sin = jnp.concatenate([-jnp.sin(ar), jnp.sin(ar), -jnp.sin(ac), jnp.sin(ac)], axis=1)
    reps = LANE // head_dim
    cos = jnp.tile(cos, (1, reps))
    sin = jnp.tile(sin, (1, reps))
    cos = jnp.concatenate([cos, jnp.ones((ROW_BLOCK, LANE), f32)], axis=0)
    sin = jnp.concatenate([sin, jnp.zeros((ROW_BLOCK, LANE), f32)], axis=0)
    return cos, sin, quarter


def kernel(x, c, ctx, c_ctx, w_ada, b_ada, g_norm_mix, g_norm_ffn, w_in_even, w_out_even, hgrn_lb_logits,
           g_hgrn_norm, win_sink, w_in_odd, w_out_odd, g_q_norm, g_k_norm, w_router, b_router,
           w_exp_gate, w_exp_up, w_exp_down, w_sh_gate, w_sh_up, w_sh_down, g_norm_final):
    nb, seq, d = x.shape
    lc = ctx.shape[1]
    assert seq % ROW_BLOCK == 0 and lc % ROW_BLOCK == 0 and seq % lc == 0
    t_lat = nb * seq
    t_all = t_lat + nb * lc
    bpb = seq // ROW_BLOCK
    nlb = t_lat // ROW_BLOCK
    dims = (nlb, bpb, nb)

    xs = jnp.concatenate([x.reshape(t_lat, d), ctx.reshape(nb * lc, d)], axis=0)
    cc = jnp.concatenate([c, c_ctx[None, :], jnp.zeros((8 - nb - 1, d), f32)], axis=0)

    lb_w = jax.nn.softmax(hgrn_lb_logits.astype(f32), axis=0)
    lower_bounds = jnp.cumsum(lb_w, axis=0)[1:] - lb_w[0]

    mod = _ada(cc, w_ada, b_ada, 0).reshape(8 * 6, 1, d)
    w_in = w_in_even[0].astype(bf16)
    proj = _normmm(xs, g_norm_mix[0], mod, 0, w_in[:, :HGRN_COLS], f32, dims, tn=1280)
    cos, sin, quarter = _rope_tables(seq, WIN_HEAD_DIM)
    epi = dict(blocks={0: (WIN_HEADS + WIN_KV_HEADS) * WIN_HEAD_DIM // LANE}, quarter=quarter, head_norm=False,
               q_slices=WIN_HEADS * WIN_HEAD_DIM // LANE, q_scale=WIN_HEAD_DIM ** -0.5)
    wproj = _normmm(xs, g_norm_mix[0], mod, 0, w_in[:, HGRN_COLS:], bf16, dims, tn=WIN_COLS,
                    epilogue=epi, tables=(cos, sin))
    o_f, o_b = _hgrn(proj, lower_bounds[0], nb, seq, lc)
    a_mix = _hgrn_out(o_f, o_b, proj, g_hgrn_norm[0])
    b_mix = _win_attn(wproj, win_sink[0], nb, seq, lc)
    xs = _outproj(a_mix, 0, b_mix, 0, w_out_even[0].astype(bf16), xs, mod, 2, dims, t_all)
    xs = _moe(xs, g_norm_ffn[0], mod, 3, w_router[0], b_router[0],
              w_exp_gate, w_exp_up, w_exp_down, 0,
              w_sh_gate[0].astype(bf16), w_sh_up[0].astype(bf16), w_sh_down[0].astype(bf16),
              g_norm_final, dims, t_all, False)

    mod = _ada(cc, w_ada, b_ada, 1).reshape(8 * 6, 1, d)
    cos, sin, quarter = _rope_tables(seq, GLB_HEAD_DIM)
    tn = 1536
    n_sl = tn // LANE
    rope_slices = GLB_HEADS + GLB_KV_HEADS
    epi = dict(blocks={0: min(rope_slices, n_sl), 1: max(rope_slices - n_sl, 0)}, quarter=quarter, head_norm=True,
               q_slices=GLB_HEADS, q_scale=GLB_HEAD_DIM ** -0.5)
    qkv = _normmm(xs, g_norm_mix[1], mod, 0, w_in_odd[0].astype(bf16), bf16, dims, tn=tn,
                  epilogue=epi, tables=(cos, sin), gains=(g_q_norm[0], g_k_norm[0]))
    att = _glb_attn(qkv, nb, seq, lc)
    xl = _outproj(att, 0, att, 1, w_out_odd[0].astype(bf16), xs, mod, 2, dims, t_lat)
    out = _moe(xl, g_norm_ffn[1], mod, 3, w_router[1], b_router[1],
               w_exp_gate, w_exp_up, w_exp_down, 1,
               w_sh_gate[1].astype(bf16), w_sh_up[1].astype(bf16), w_sh_down[1].astype(bf16),
               g_norm_final, dims, t_lat, True)
    return out.reshape(nb, seq, d)
```

```python
import functools

import jax
import jax.numpy as jnp
from jax import lax
from jax.experimental import pallas as pl
from jax.experimental.pallas import tpu as pltpu

f32 = jnp.float32
bf16 = jnp.bfloat16
i32 = jnp.int32
u32 = jnp.uint32

EPS = 1e-6
ROPE_THETA = 10000.0
GRID_W = 64

HGRN_DK = 128
HGRN_HEADS = 8
HGRN_CHUNK = 64
HGRN_SUB = 16
HGRN_HEADS_PER_STEP = 4
WIN_HEAD_DIM = 64
WIN_HEADS = 16
WIN_KV_HEADS = 2
WIN_BLOCK = 128
GLB_HEAD_DIM = 128
GLB_HEADS = 16
GLB_KV_HEADS = 4
Q_BLOCK = 256
N_EXPERTS = 64
N_GROUPS = 8
TOPK_GROUPS = 4
TOP_K = 8
ROUTED_SCALE = 2.5
MOE_BLOCK = 256

LANE = 128
ROW_BLOCK = 256
VMEM_LIMIT = 56 * 1024 * 1024

A_K = HGRN_HEADS * HGRN_DK
HGRN_COLS = 5 * A_K
WIN_COLS = WIN_HEADS * WIN_HEAD_DIM + 2 * WIN_KV_HEADS * WIN_HEAD_DIM


def _params(sem, vmem=VMEM_LIMIT):
    return pltpu.CompilerParams(dimension_semantics=sem, vmem_limit_bytes=vmem)


def _silu(x):
    return x * jax.nn.sigmoid(x)


def _dot(a, b):
    return jnp.dot(a, b, preferred_element_type=f32)


def _dot_nt(a, b):
    return lax.dot_general(a, b, (((1,), (1,)), ((), ())), preferred_element_type=f32)


def _pack_halves(x):
    n = x.shape[1] // 2
    lo = lax.bitcast_convert_type(x[:, :n].astype(bf16).astype(f32), u32)
    hi = lax.bitcast_convert_type(x[:, n:].astype(bf16).astype(f32), u32)
    return (lo >> 16) | hi


def _unpack_halves(w):
    lo = lax.bitcast_convert_type(w << 16, f32)
    hi = lax.bitcast_convert_type(w & jnp.uint32(0xFFFF0000), f32)
    return lo, hi


def _dot_tn(a, b):
    return lax.dot_general(a, b, (((0,), (0,)), ((), ())), preferred_element_type=f32)


def _ada_kernel(c_ref, w_ref, b_ref, o_ref):
    a = _silu(c_ref[...]).astype(bf16)
    o_ref[...] = _dot(a, w_ref[...].astype(bf16)) + b_ref[...]


def _ada(c8, w, b, layer, tn=768):
    m, d = c8.shape
    n = w.shape[2]
    return pl.pallas_call(
        _ada_kernel,
        grid=(n // tn,),
        in_specs=[pl.BlockSpec((m, d), lambda j: (0, 0)),
                  pl.BlockSpec((None, d, tn), lambda j: (layer, 0, j)),
                  pl.BlockSpec((None, 1, tn), lambda j: (layer, 0, j))],
        out_specs=pl.BlockSpec((m, tn), lambda j: (0, j)),
        out_shape=jax.ShapeDtypeStruct((m, n), f32),
        compiler_params=_params(("arbitrary",)),
    )(c8, w, b.reshape(b.shape[0], 1, n))


def _norm_mod(x, g, sc, sh):
    y = x * lax.rsqrt(jnp.mean(x * x, axis=-1, keepdims=True) + EPS) * g
    return y * (1.0 + sc) + sh


def _rope_slice(x, cos, sin, quarter):
    lane = lax.broadcasted_iota(i32, x.shape, 1)
    up = pltpu.roll(x, LANE - quarter, axis=1)
    dn = pltpu.roll(x, quarter, axis=1)
    partner = jnp.where(lane % (2 * quarter) < quarter, up, dn)
    return x * cos + partner * sin


def _normmm_kernel(x_ref, g_ref, sc_ref, sh_ref, w_ref, *rest, epilogue, tn):
    if epilogue is None:
        (o_ref,) = rest
    elif epilogue["head_norm"]:
        cos_ref, sin_ref, gq_ref, gk_ref, o_ref = rest
    else:
        cos_ref, sin_ref, o_ref = rest
    j = pl.program_id(0)
    h = _norm_mod(x_ref[...], g_ref[...], sc_ref[...], sh_ref[...]).astype(bf16)
    y = _dot(h, w_ref[...])
    if epilogue is None:
        o_ref[...] = y.astype(o_ref.dtype)
        return

    n_sl = tn // LANE
    for jb in range(epilogue["n_col_blocks"]):
        n_rope = epilogue["blocks"].get(jb, 0)

        @pl.when(j == jb)
        def _(jb=jb, n_rope=n_rope):
            cos = cos_ref[...]
            sin = sin_ref[...]
            outs = []
            for s in range(n_sl):
                ys = y[:, s * LANE:(s + 1) * LANE]
                if s < n_rope:
                    col = jb * n_sl + s
                    is_q = col < epilogue["q_slices"]
                    if epilogue["head_norm"]:
                        gain = gq_ref[...] if is_q else gk_ref[...]
                        ys = ys * lax.rsqrt(jnp.mean(ys * ys, axis=-1, keepdims=True) + EPS) * gain
                    ys = _rope_slice(ys, cos, sin, epilogue["quarter"])
                    if is_q:
                        ys = ys * epilogue["q_scale"]
                outs.append(ys.astype(o_ref.dtype))
            o_ref[...] = jnp.concatenate(outs, axis=1)


def _mod_row(i, n_lat_blocks, blocks_per_batch, n_batch):
    return jnp.where(i < n_lat_blocks, i // blocks_per_batch, n_batch)


def _normmm(x, g, mod, which, w, out_dtype, dims, tn, epilogue=None, tables=None, gains=None):
    t, d = x.shape
    n = w.shape[1]
    tm = ROW_BLOCK
    nlb, bpb, nb = dims
    mrow = lambda i: _mod_row(i, nlb, bpb, nb)
    in_specs = [pl.BlockSpec((tm, d), lambda j, i: (i, 0)),
                pl.BlockSpec((1, d), lambda j, i: (0, 0)),
                pl.BlockSpec((None, 1, d), lambda j, i: (mrow(i) * 6 + which + 1, 0, 0)),
                pl.BlockSpec((None, 1, d), lambda j, i: (mrow(i) * 6 + which, 0, 0)),
                pl.BlockSpec((d, tn), lambda j, i: (0, j))]
    args = [x, g.reshape(1, d), mod, mod, w]
    if epilogue is not None:
        cos, sin = tables
        tab = lambda j, i: (jnp.where(i < nlb, i % bpb, bpb), 0)
        in_specs += [pl.BlockSpec((tm, LANE), tab), pl.BlockSpec((tm, LANE), tab)]
        args += [cos, sin]
        epilogue = dict(epilogue, n_col_blocks=n // tn)
        if epilogue["head_norm"]:
            in_specs += [pl.BlockSpec((1, LANE), lambda j, i: (0, 0))] * 2
            args += [gains[0].reshape(1, LANE), gains[1].reshape(1, LANE)]
    return pl.pallas_call(
        functools.partial(_normmm_kernel, epilogue=epilogue, tn=tn),
        grid=(n // tn, t // tm),
        in_specs=in_specs,
        out_specs=pl.BlockSpec((tm, tn), lambda j, i: (i, j)),
        out_shape=jax.ShapeDtypeStruct((t, n), out_dtype),
        compiler_params=_params(("parallel", "parallel")),
    )(*args)


def _gla_chunk(q_raw, v, f_raw, lb, st_ref, a_scr, rev):
    c = HGRN_CHUNK
    sub = HGRN_SUB
    q = _silu(q_raw)
    f = lb + (1.0 - lb) * jax.nn.sigmoid(f_raw)
    kk = 1.0 - f
    g = jnp.log(f)
    row = lax.broadcasted_iota(i32, (c, c), 0)
    col = lax.broadcasted_iota(i32, (c, c), 1)
    tri = (col >= row) if rev else (col <= row)
    b = jnp.dot(tri.astype(f32), g, preferred_element_type=f32, precision=lax.Precision.HIGHEST)
    b_end = b[0:1] if rev else b[c - 1:c]
    st = st_ref[...]
    o = _dot_nt((q * jnp.exp(b)).astype(bf16), st.astype(bf16))

    vb = v.astype(bf16)
    nsub = c // sub
    lane16 = lax.broadcasted_iota(i32, (sub, sub), 1)
    row16 = lax.broadcasted_iota(i32, (sub, sub), 0)
    diag_ok = (row16 <= lane16) if rev else (row16 >= lane16)
    rsub = lax.broadcasted_iota(i32, (c, sub), 0) // sub
    for jb in range(nsub):
        js = slice(jb * sub, (jb + 1) * sub)
        m_j = b[jb * sub:jb * sub + 1] if rev else b[(jb + 1) * sub - 1:(jb + 1) * sub]
        kd = kk[js] * jnp.exp(m_j - b[js])
        qd = q * jnp.exp(jnp.minimum(b - m_j, 0.0))
        a_col = _dot_nt(qd.astype(bf16), kd.astype(bf16))
        off_ok = (rsub < jb) if rev else (rsub > jb)
        a_col = jnp.where(off_ok, a_col, 0.0)
        qi, bi, ki = q[js], b[js], kk[js]
        a_dd = jnp.zeros((sub, sub), f32)
        for s in range(sub):
            e = jnp.exp(bi - bi[s:s + 1])
            col_s = jnp.sum(qi * ki[s:s + 1] * e, axis=-1, keepdims=True)
            a_dd = jnp.where(lane16 == s, col_s, a_dd)
        a_dd = jnp.where(diag_ok, a_dd, 0.0)
        pieces = [a_dd if ib == jb else a_col[ib * sub:(ib + 1) * sub] for ib in range(nsub)]
        a_scr[:, js] = jnp.concatenate(pieces, axis=0)
    o = o + _dot(a_scr[...].astype(bf16), vb)
    kdec = kk * jnp.exp(b_end - b)
    st_ref[...] = jnp.exp(b_end) * st + _dot_tn(vb, kdec.astype(bf16))
    return o


def _hgrn_kernel(qf_ref, vf_ref, ff_ref, qb_ref, vb_ref, fb_ref, lb_ref, of_ref, ob_ref,
                 stf_ref, stb_ref, af_scr, ab_scr):
    @pl.when(pl.program_id(2) == 0)
    def _():
        stf_ref[...] = jnp.zeros_like(stf_ref)
        stb_ref[...] = jnp.zeros_like(stb_ref)

    for j in range(HGRN_HEADS_PER_STEP):
        sl = slice(j * HGRN_DK, (j + 1) * HGRN_DK)
        lb = lb_ref[:, sl]
        of_ref[:, sl] = _gla_chunk(qf_ref[:, sl], vf_ref[:, sl], ff_ref[:, sl], lb, stf_ref.at[j], af_scr.at[j], False)
        ob_ref[:, sl] = _gla_chunk(qb_ref[:, sl], vb_ref[:, sl], fb_ref[:, sl], lb, stb_ref.at[j], ab_scr.at[j], True)


def _hgrn(proj, lb, n_batch, seq, ctx_len):
    t = proj.shape[0]
    c = HGRN_CHUNK
    hp = HGRN_HEADS_PER_STEP
    wblk = hp * HGRN_DK
    ngrp = HGRN_HEADS // hp
    ncc, ncl = ctx_len // c, seq // c
    lat_chunks = n_batch * ncl

    def fwd_blk(b, s):
        return jnp.where(s < ncc, lat_chunks + b * ncc + s, b * ncl + (s - ncc))

    def bwd_blk(b, s):
        return jnp.where(s < ncc, lat_chunks + b * ncc + (ncc - 1 - s), b * ncl + (ncl - 1 - (s - ncc)))

    def spec(blk, section):
        return pl.BlockSpec((c, wblk), lambda b, h, s: (blk(b, s), section * ngrp + h))

    def ospec(blk):
        return pl.BlockSpec((c, wblk), lambda b, h, s: (blk(b, s), h))

    out = jax.ShapeDtypeStruct((t, A_K), f32)
    return pl.pallas_call(
        _hgrn_kernel,
        grid=(n_batch, ngrp, ncc + ncl),
        in_specs=[spec(fwd_blk, 0), spec(fwd_blk, 1), spec(fwd_blk, 2),
                  spec(bwd_blk, 0), spec(bwd_blk, 1), spec(bwd_blk, 3),
                  pl.BlockSpec((1, wblk), lambda b, h, s: (0, h))],
        out_specs=[ospec(fwd_blk), ospec(bwd_blk)],
        out_shape=[out, out],
        scratch_shapes=[pltpu.VMEM((hp, HGRN_DK, HGRN_DK), f32), pltpu.VMEM((hp, HGRN_DK, HGRN_DK), f32),
                        pltpu.VMEM((hp, c, c), f32), pltpu.VMEM((hp, c, c), f32)],
        compiler_params=_params(("parallel", "parallel", "arbitrary")),
    )(proj, proj, proj, proj, proj, proj, lb.reshape(1, A_K))


def _hgrn_out_kernel(of_ref, ob_ref, g_ref, gain_ref, o_ref):
    gain = gain_ref[...]
    outs = []
    for h in range(HGRN_HEADS):
        sl = slice(h * HGRN_DK, (h + 1) * HGRN_DK)
        o = of_ref[:, sl] + ob_ref[:, sl]
        o = o * lax.rsqrt(jnp.mean(o * o, axis=-1, keepdims=True) + EPS) * gain
        outs.append((o * _silu(g_ref[:, sl])).astype(bf16))
    o_ref[...] = jnp.concatenate(outs, axis=1)


def _hgrn_out(o_f, o_b, proj, gain):
    t = o_f.shape[0]
    tm = ROW_BLOCK
    return pl.pallas_call(
        _hgrn_out_kernel,
        grid=(t // tm,),
        in_specs=[pl.BlockSpec((tm, A_K), lambda i: (i, 0)),
                  pl.BlockSpec((tm, A_K), lambda i: (i, 0)),
                  pl.BlockSpec((tm, A_K), lambda i: (i, 4)),
                  pl.BlockSpec((1, HGRN_DK), lambda i: (0, 0))],
        out_specs=pl.BlockSpec((tm, A_K), lambda i: (i, 0)),
        out_shape=jax.ShapeDtypeStruct((t, A_K), bf16),
        compiler_params=_params(("parallel",)),
    )(o_f, o_b, proj, gain.reshape(1, HGRN_DK))


def _win_kernel(sink_ref, q_ref, kp_ref, kc_ref, kn_ref, vp_ref, vc_ref, vn_ref, kx_ref, vx_ref, o_ref,
                *, n_lat_blocks):
    n = pl.program_id(1)
    w = WIN_BLOCK
    dh = WIN_HEAD_DIM
    is_lat = n < n_lat_blocks
    ri = lax.broadcasted_iota(i32, (w, w), 0)
    ci = lax.broadcasted_iota(i32, (w, w), 1)
    ok_p = (ci >= ri) & is_lat & (n > 0)
    ok_c = jnp.broadcast_to(is_lat, (w, w))
    ok_n = (ci <= ri) & is_lat & (n < n_lat_blocks - 1)
    valid = jnp.concatenate([ok_p, ok_c, ok_n], axis=1)
    kwin = jnp.concatenate([kp_ref[...], kc_ref[...], kn_ref[...]], axis=0)
    vwin = jnp.concatenate([vp_ref[...], vc_ref[...], vn_ref[...]], axis=0)
    kx = kx_ref[...]
    vx = vx_ref[...]
    g = WIN_HEADS // WIN_KV_HEADS
    outs = []
    for h in range(WIN_HEADS):
        kv = h // g
        ks = slice(kv * dh, (kv + 1) * dh)
        qh = q_ref[:, h * dh:(h + 1) * dh]
        sw = jnp.where(valid, _dot_nt(qh, kwin[:, ks]), -jnp.inf)
        sx = _dot_nt(qh, kx[:, ks])
        sk = sink_ref[h]
        m = jnp.maximum(jnp.maximum(jnp.max(sw, axis=-1, keepdims=True), jnp.max(sx, axis=-1, keepdims=True)), sk)
        pw = jnp.exp(sw - m)
        px = jnp.exp(sx - m)
        den = jnp.sum(pw, axis=-1, keepdims=True) + jnp.sum(px, axis=-1, keepdims=True) + jnp.exp(sk - m)
        o = _dot(pw.astype(bf16), vwin[:, ks]) + _dot(px.astype(bf16), vx[:, ks])
        outs.append((o / den).astype(bf16))
    o_ref[...] = jnp.concatenate(outs, axis=1)


def _win_attn(wproj, sink, n_batch, seq, ctx_len):
    t = wproj.shape[0]
    w = WIN_BLOCK
    nlb = seq // w
    ncb = ctx_len // w
    lat_blocks = n_batch * nlb
    qc = WIN_HEADS * WIN_HEAD_DIM // LANE
    kcol, vcol = qc, qc + 1

    def qrow(b, n):
        return jnp.where(n < nlb, b * nlb + n, lat_blocks + b * ncb + (n - nlb))

    def krow(off):
        def f(b, n):
            return b * nlb + jnp.clip(n + off, 0, nlb - 1)
        return f

    def kspec(off, colblk):
        return pl.BlockSpec((w, LANE), lambda b, n: (krow(off)(b, n), colblk))

    def xspec(colblk):
        return pl.BlockSpec((ctx_len, LANE), lambda b, n: ((n_batch * seq) // ctx_len + b, colblk))

    return pl.pallas_call(
        functools.partial(_win_kernel, n_lat_blocks=nlb),
        grid=(n_batch, nlb + ncb),
        in_specs=[pl.BlockSpec(memory_space=pltpu.SMEM),
                  pl.BlockSpec((w, qc * LANE), lambda b, n: (qrow(b, n), 0)),
                  kspec(-1, kcol), kspec(0, kcol), kspec(1, kcol),
                  kspec(-1, vcol), kspec(0, vcol), kspec(1, vcol),
                  xspec(kcol), xspec(vcol)],
        out_specs=pl.BlockSpec((w, qc * LANE), lambda b, n: (qrow(b, n), 0)),
        out_shape=jax.ShapeDtypeStruct((t, qc * LANE), bf16),
        compiler_params=_params(("parallel", "arbitrary")),
    )(sink.astype(f32), wproj, wproj, wproj, wproj, wproj, wproj, wproj, wproj, wproj)


GLB_KEY_CHUNK = 256


def _glb_kernel(q_ref, kx_ref, kl_ref, vx_ref, vl_ref, o_ref, m_scr, l_scr, acc_scr):
    dh = GLB_HEAD_DIM
    g = GLB_HEADS // GLB_KV_HEADS
    ck = GLB_KEY_CHUNK
    tq = q_ref.shape[0]
    q = jnp.concatenate([q_ref[:, i * dh:(i + 1) * dh] for i in range(g)], axis=0)

    def update(k, v, first):
        s = _dot_nt(q, k)
        reps = s.shape[1] // LANE
        mx = jnp.broadcast_to(jnp.max(s, axis=-1, keepdims=True), (g * tq, LANE))
        if first:
            m_new = mx
            p = jnp.exp(s - jnp.concatenate([m_new] * reps, axis=1))
            l_scr[...] = jnp.broadcast_to(jnp.sum(p, axis=-1, keepdims=True), (g * tq, LANE))
            acc_scr[...] = _dot(p.astype(bf16), v)
        else:
            m_old = m_scr[...]
            m_new = jnp.maximum(m_old, mx)
            alpha = jnp.exp(m_old - m_new)
            p = jnp.exp(s - jnp.concatenate([m_new] * reps, axis=1))
            l_scr[...] = alpha * l_scr[...] + jnp.broadcast_to(jnp.sum(p, axis=-1, keepdims=True), (g * tq, LANE))
            acc_scr[...] = alpha * acc_scr[...] + _dot(p.astype(bf16), v)
        m_scr[...] = m_new

    chunks = [(kx_ref, vx_ref, c, min(ck, kx_ref.shape[0] - c)) for c in range(0, kx_ref.shape[0], ck)]
    chunks += [(kl_ref, vl_ref, c, min(ck, kl_ref.shape[0] - c)) for c in range(0, kl_ref.shape[0], ck)]
    for n, (kr, vr, c0, sz) in enumerate(chunks):
        update(kr[c0:c0 + sz, :], vr[c0:c0 + sz, :], n == 0)
    o = acc_scr[...] / l_scr[...]
    o_ref[...] = jnp.concatenate([o[i * tq:(i + 1) * tq] for i in range(g)], axis=1).astype(bf16)


def _glb_attn(qkv, n_batch, seq, ctx_len):
    tq = Q_BLOCK
    dh = GLB_HEAD_DIM
    g = GLB_HEADS // GLB_KV_HEADS
    nq = seq // tq
    kcol0 = GLB_HEADS
    vcol0 = GLB_HEADS + GLB_KV_HEADS
    ctx_blk0 = (n_batch * seq) // ctx_len
    return pl.pallas_call(
        _glb_kernel,
        grid=(n_batch, GLB_KV_HEADS, nq),
        in_specs=[pl.BlockSpec((tq, g * dh), lambda b, k, i: (b * nq + i, k)),
                  pl.BlockSpec((ctx_len, dh), lambda b, k, i: (ctx_blk0 + b, kcol0 + k)),
                  pl.BlockSpec((seq, dh), lambda b, k, i: (b, kcol0 + k)),
                  pl.BlockSpec((ctx_len, dh), lambda b, k, i: (ctx_blk0 + b, vcol0 + k)),
                  pl.BlockSpec((seq, dh), lambda b, k, i: (b, vcol0 + k))],
        out_specs=pl.BlockSpec((tq, g * dh), lambda b, k, i: (b * nq + i, k)),
        out_shape=jax.ShapeDtypeStruct((n_batch * seq, GLB_HEADS * dh), bf16),
        scratch_shapes=[pltpu.VMEM((g * tq, LANE), f32), pltpu.VMEM((g * tq, LANE), f32), pltpu.VMEM((g * tq, dh), f32)],
        compiler_params=_params(("parallel", "parallel", "arbitrary")),
    )(qkv, qkv, qkv, qkv, qkv)


def _outproj_kernel(a1_ref, a2_ref, w1_ref, w2_ref, x_ref, gate_ref, o_ref):
    y = _dot(a1_ref[...], w1_ref[...]) + _dot(a2_ref[...], w2_ref[...])
    o_ref[...] = x_ref[...] + gate_ref[...] * y


def _outproj(a1, c1, a2, c2, w, x, mod, which, dims, n_rows, tn=1024):
    d = x.shape[1]
    kh = w.shape[0] // 2
    tm = ROW_BLOCK
    nlb, bpb, nb = dims
    mrow = lambda i: _mod_row(i, nlb, bpb, nb)
    return pl.pallas_call(
        _outproj_kernel,
        grid=(d // tn, n_rows // tm),
        in_specs=[pl.BlockSpec((tm, kh), lambda j, i: (i, c1)),
                  pl.BlockSpec((tm, kh), lambda j, i: (i, c2)),
                  pl.BlockSpec((kh, tn), lambda j, i: (0, j)),
                  pl.BlockSpec((kh, tn), lambda j, i: (1, j)),
                  pl.BlockSpec((tm, tn), lambda j, i: (i, j)),
                  pl.BlockSpec((None, 1, tn), lambda j, i: (mrow(i) * 6 + which, 0, j))],
        out_specs=pl.BlockSpec((tm, tn), lambda j, i: (i, j)),
        out_shape=jax.ShapeDtypeStruct((n_rows, d), f32),
        compiler_params=_params(("parallel", "parallel")),
    )(a1, a2, w, w, x, mod)


def _router_kernel(x_ref, g_ref, sc_ref, sh_ref, wr_ref, br_ref, u_ref,
                   h_ref, idx_ref, wt_ref, rank_ref, cnt_ref, carry_scr):
    i = pl.program_id(0)
    tm = x_ref.shape[0]
    ne, ng, pg = N_EXPERTS, N_GROUPS, N_EXPERTS // N_GROUPS

    @pl.when(i == 0)
    def _():
        carry_scr[...] = jnp.zeros_like(carry_scr)

    h = _norm_mod(x_ref[...], g_ref[...], sc_ref[...], sh_ref[...])
    h_ref[...] = _pack_halves(h)
    logits = lax.dot_general(wr_ref[...], h, (((1,), (1,)), ((), ())), preferred_element_type=f32,
                             precision=lax.Precision.HIGHEST)
    s = jax.nn.sigmoid(logits)
    s3 = s.reshape(ng, pg, tm)
    b3 = (s + br_ref[...]).reshape(ng, pg, tm)
    gi = lax.broadcasted_iota(i32, (ng, pg, tm), 0)
    pi = lax.broadcasted_iota(i32, (ng, pg, tm), 1)
    neg = -jnp.inf
    m1 = jnp.max(b3, axis=1, keepdims=True)
    first = jnp.min(jnp.where(b3 == m1, pi, pg), axis=1, keepdims=True)
    m2 = jnp.max(jnp.where(pi == first, neg, b3), axis=1, keepdims=True)
    gs = m1 + m2
    g2 = lax.broadcasted_iota(i32, (ng, 1, tm), 0)
    gmask = jnp.zeros((ng, 1, tm), jnp.bool_)
    for _ in range(TOPK_GROUPS):
        m = jnp.max(gs, axis=0, keepdims=True)
        fi = jnp.min(jnp.where(gs == m, g2, ng), axis=0, keepdims=True)
        pick = g2 == fi
        gmask = gmask | pick
        gs = jnp.where(pick, neg, gs)
    cand = jnp.where(gmask, b3, neg)
    eid = gi * pg + pi
    sel = jnp.zeros((ng, pg, tm), jnp.bool_)
    picks, idxs, wts = [], [], []
    for _ in range(TOP_K):
        m = jnp.max(jnp.max(cand, axis=0, keepdims=True), axis=1, keepdims=True)
        fi = jnp.min(jnp.min(jnp.where(cand == m, eid, ne), axis=0, keepdims=True), axis=1, keepdims=True)
        pick = eid == fi
        picks.append(pick)
        idxs.append(fi.reshape(1, tm))
        wts.append(jnp.sum(jnp.sum(jnp.where(pick, s3, 0.0), axis=0, keepdims=True), axis=1,
                           keepdims=True).reshape(1, tm))
        sel = sel | pick
        cand = jnp.where(pick, neg, cand)
    wsum = wts[0]
    for k in range(1, TOP_K):
        wsum = wsum + wts[k]
    idx_ref[...] = jnp.concatenate(idxs, axis=0)
    wt_ref[...] = jnp.concatenate([wk / wsum * ROUTED_SCALE for wk in wts], axis=0)
    sel2 = jnp.where(sel, 1.0, 0.0).reshape(ne, tm)
    prefix = (_dot(sel2.astype(bf16), u_ref[...]) + carry_scr[...]).reshape(ng, pg, tm)
    ranks = [jnp.sum(jnp.sum(jnp.where(pk, prefix, 0.0), axis=0, keepdims=True), axis=1,
                     keepdims=True).reshape(1, tm) for pk in picks]
    rank_ref[...] = jnp.concatenate(ranks, axis=0).astype(i32)
    total = carry_scr[...] + jnp.sum(sel2, axis=1, keepdims=True)
    carry_scr[...] = total
    cnt_ref[...] = total


def _router(x, g, mod, which, w_router, b_router, dims, n_rows):
    d = x.shape[1]
    tm = ROW_BLOCK
    ne = N_EXPERTS
    nlb, bpb, nb = dims
    mrow = lambda i: _mod_row(i, nlb, bpb, nb)
    tri = (jnp.arange(tm)[:, None] < jnp.arange(tm)[None, :]).astype(bf16)
    return pl.pallas_call(
        _router_kernel,
        grid=(n_rows // tm,),
        in_specs=[pl.BlockSpec((tm, d), lambda i: (i, 0)),
                  pl.BlockSpec((1, d), lambda i: (0, 0)),
                  pl.BlockSpec((None, 1, d), lambda i: (mrow(i) * 6 + which + 1, 0, 0)),
                  pl.BlockSpec((None, 1, d), lambda i: (mrow(i) * 6 + which, 0, 0)),
                  pl.BlockSpec((ne, d), lambda i: (0, 0)),
                  pl.BlockSpec((ne, 1), lambda i: (0, 0)),
                  pl.BlockSpec((tm, tm), lambda i: (0, 0))],
        out_specs=[pl.BlockSpec((tm, d // 2), lambda i: (i, 0)),
                   pl.BlockSpec((TOP_K, tm), lambda i: (0, i)),
                   pl.BlockSpec((TOP_K, tm), lambda i: (0, i)),
                   pl.BlockSpec((TOP_K, tm), lambda i: (0, i)),
                   pl.BlockSpec((ne, 1), lambda i: (0, 0))],
        out_shape=[jax.ShapeDtypeStruct((n_rows, d // 2), u32),
                   jax.ShapeDtypeStruct((TOP_K, n_rows), i32),
                   jax.ShapeDtypeStruct((TOP_K, n_rows), f32),
                   jax.ShapeDtypeStruct((TOP_K, n_rows), i32),
                   jax.ShapeDtypeStruct((ne, 1), f32)],
        scratch_shapes=[pltpu.VMEM((ne, 1), f32)],
        compiler_params=_params(("arbitrary",)),
    )(x, g.reshape(1, d), mod, mod, w_router.T, b_router.reshape(ne, 1).astype(f32), tri)


EXPERT_BURSTS = 8


def _expert_kernel(be_ref, nu_ref, tok0_ref, tokn_ref, posp_ref, h_hbm, wg_ref, wu_ref, wd_ref, y_hbm,
                   xbuf, ybuf, wgc, wuc, wdc, sem_g, sem_s):
    i = pl.program_id(0)
    nblk = pl.num_programs(0) - 1
    n_used = nu_ref[0]
    slot = i % 2
    mb = MOE_BLOCK
    de = wgc.shape[1]
    dp = ybuf.shape[2]
    spare0 = y_hbm.shape[0] - 2 * mb
    per = mb // (EXPERT_BURSTS // 2)

    def gather_row(idx_ref, r, dst_slot, priority=0):
        pltpu.make_async_copy(h_hbm.at[pl.ds(idx_ref[0, 0, r], 1)], xbuf.at[dst_slot, pl.ds(r, 1)],
                              sem_g.at[dst_slot]).start(priority=priority)

    def gather_burst(b):
        for r in range(b * per, (b + 1) * per):
            gather_row(tokn_ref, r, 1 - slot, r % 2)

    def scatter_burst(b):
        for r in range(b * per, (b + 1) * per):
            p = jnp.where(i == 0, spare0 + r, posp_ref[0, 0, r])
            pltpu.make_async_copy(ybuf.at[1 - slot, pl.ds(r, 1)], y_hbm.at[pl.ds(p, 1)],
                                  sem_s.at[0]).start(priority=r % 2)

    def wait_scatter():
        pltpu.make_async_copy(ybuf.at[0], y_hbm.at[pl.ds(0, mb)], sem_s.at[0]).wait()

    @pl.when(i == 0)
    def _():
        ybuf[1] = jnp.zeros((mb, dp), u32)

        def body(r, carry):
            gather_row(tok0_ref, r, 0)
            pltpu.make_async_copy(ybuf.at[1, pl.ds(r, 1)], y_hbm.at[pl.ds(spare0 + mb + r, 1)], sem_s.at[0]).start()
            return carry
        lax.fori_loop(0, mb, body, 0)

    @pl.when(i <= n_used)
    def _():
        pltpu.make_async_copy(h_hbm.at[pl.ds(0, mb)], xbuf.at[slot], sem_g.at[slot]).wait()

    ib = jnp.minimum(i, nblk - 1)
    changed = (i == 0) | (be_ref[ib] != be_ref[jnp.maximum(ib - 1, 0)])

    @pl.when(changed & (i < n_used))
    def _():
        wgc[...] = wg_ref[...].astype(bf16)
        wuc[...] = wu_ref[...].astype(bf16)
        wdc[...] = wd_ref[...].astype(bf16)

    @pl.when(i < n_used)
    def _():
        lo, hi = _unpack_halves(xbuf[slot])
        x = jnp.concatenate([lo.astype(bf16), hi.astype(bf16)], axis=1)
        nh = EXPERT_BURSTS // 4
        hw = de // nh
        hids = []
        for c in range(nh):
            gate = _dot(x, wgc[:, c * hw:(c + 1) * hw])
            gather_burst(2 * c)
            up = _dot(x, wuc[:, c * hw:(c + 1) * hw])
            gather_burst(2 * c + 1)
            hids.append((_silu(gate) * up).astype(bf16))
        hid = jnp.concatenate(hids, axis=1)
        wait_scatter()
        nd = EXPERT_BURSTS // 2
        dw = dp // nd
        for c in range(nd):
            wd_c = jnp.concatenate([wdc[:, c * dw:(c + 1) * dw], wdc[:, dp + c * dw:dp + (c + 1) * dw]], axis=1)
            ybuf[slot, :, c * dw:(c + 1) * dw] = _pack_halves(_dot(hid, wd_c))
            scatter_burst(c)

    @pl.when(i == n_used)
    def _():
        wait_scatter()
        for b in range(EXPERT_BURSTS // 2):
            scatter_burst(b)
        wait_scatter()


def _experts(h, slot_tok, slot_pos, block_e, n_used, wg, wu, wd, layer, n_rows):
    dp = h.shape[1]
    d = wg.shape[2]
    nblk = slot_tok.shape[0]
    de = wg.shape[3]
    mb = MOE_BLOCK
    last = nblk - 1
    smem = lambda f: pl.BlockSpec((1, 1, mb), f, memory_space=pltpu.SMEM)
    wspec = lambda shape: pl.BlockSpec((None,) + shape, lambda i, be, nu: (layer, be[jnp.minimum(i, last)], 0, 0))
    grid_spec = pltpu.PrefetchScalarGridSpec(
        num_scalar_prefetch=2,
        grid=(nblk + 1,),
        in_specs=[smem(lambda i, be, nu: (0, 0, 0)),
                  smem(lambda i, be, nu: (jnp.minimum(i + 1, last), 0, 0)),
                  smem(lambda i, be, nu: (jnp.clip(i - 1, 0, last), 0, 0)),
                  pl.BlockSpec(memory_space=pl.ANY),
                  wspec((None, d, de)), wspec((None, d, de)), wspec((None, de, d))],
        out_specs=pl.BlockSpec(memory_space=pl.ANY),
        scratch_shapes=[pltpu.VMEM((2, mb, dp), u32), pltpu.VMEM((2, mb, dp), u32),
                        pltpu.VMEM((d, de), bf16), pltpu.VMEM((d, de), bf16), pltpu.VMEM((de, d), bf16),
                        pltpu.SemaphoreType.DMA((2,)), pltpu.SemaphoreType.DMA((1,))],
    )
    st = slot_tok.reshape(nblk, 1, mb)
    sp = slot_pos.reshape(nblk, 1, mb)
    return pl.pallas_call(
        _expert_kernel,
        grid_spec=grid_spec,
        out_shape=jax.ShapeDtypeStruct((n_rows * TOP_K + 2 * mb, dp), u32),
        compiler_params=_params(("arbitrary",)),
    )(block_e, n_used, st, st, sp, h, wg, wu, wd)


def _shared_kernel(h_ref, wg_ref, wu_ref, wd_ref, y_ref):
    lo, hi = _unpack_halves(h_ref[...])
    x = jnp.concatenate([lo.astype(bf16), hi.astype(bf16)], axis=1)
    hid = _silu(_dot(x, wg_ref[...])) * _dot(x, wu_ref[...])
    y_ref[...] = _dot(hid.astype(bf16), wd_ref[...])


def _shared(h, wg, wu, wd):
    t, dp = h.shape
    d, de = wg.shape
    tm = ROW_BLOCK
    return pl.pallas_call(
        _shared_kernel,
        grid=(t // tm,),
        in_specs=[pl.BlockSpec((tm, dp), lambda i: (i, 0)),
                  pl.BlockSpec((d, de), lambda i: (0, 0)),
                  pl.BlockSpec((d, de), lambda i: (0, 0)),
                  pl.BlockSpec((de, d), lambda i: (0, 0))],
        out_specs=pl.BlockSpec((tm, d), lambda i: (i, 0)),
        out_shape=jax.ShapeDtypeStruct((t, d), f32),
        compiler_params=_params(("parallel",)),
    )(h, wg, wu, wd)


COMBINE_ROWS = 128


def _combine_kernel(*refs, final_norm):
    y_refs = refs[:TOP_K]
    w_ref, sh_ref, x_ref, gate_ref, gf_ref, o_ref = refs[TOP_K:]
    w = w_ref[...]
    acc_lo = acc_hi = 0.0
    for k in range(TOP_K):
        lo, hi = _unpack_halves(y_refs[k][...])
        acc_lo = acc_lo + lo * w[:, k:k + 1]
        acc_hi = acc_hi + hi * w[:, k:k + 1]
    acc = sh_ref[...] + jnp.concatenate([acc_lo, acc_hi], axis=1)
    out = x_ref[...] + gate_ref[...] * acc
    if final_norm:
        out = out * lax.rsqrt(jnp.mean(out * out, axis=-1, keepdims=True) + EPS) * gf_ref[...]
    o_ref[...] = out


def _combine(y_rows, wts, shared, x, mod, which, g_final, dims, n_rows, final_norm):
    d = x.shape[1]
    tm = COMBINE_ROWS
    per = ROW_BLOCK // tm
    nsteps = n_rows // tm
    nlb, bpb, nb = dims
    mrow = lambda i: _mod_row(i // per, nlb, bpb, nb)
    yspec = lambda k: pl.BlockSpec((tm, d // 2), lambda i: (k * nsteps + i, 0))
    return pl.pallas_call(
        functools.partial(_combine_kernel, final_norm=final_norm),
        grid=(nsteps,),
        in_specs=[yspec(k) for k in range(TOP_K)] + [
                  pl.BlockSpec((tm, TOP_K), lambda i: (i, 0)),
                  pl.BlockSpec((tm, d), lambda i: (i, 0)),
                  pl.BlockSpec((tm, d), lambda i: (i, 0)),
                  pl.BlockSpec((None, 1, d), lambda i: (mrow(i) * 6 + which, 0, 0)),
                  pl.BlockSpec((1, d), lambda i: (0, 0))],
        out_specs=pl.BlockSpec((tm, d), lambda i: (i, 0)),
        out_shape=jax.ShapeDtypeStruct((n_rows, d), f32),
        compiler_params=_params(("parallel",)),
    )(*([y_rows] * TOP_K), wts.T, shared, x, mod, g_final.reshape(1, d))


def _moe(x, g, mod, which_shift, w_router, b_router, wg, wu, wd, layer, sg, su, sd, g_final, dims, n_rows,
         final_norm):
    ne = N_EXPERTS
    mb = MOE_BLOCK
    h, idx, wts, rank, cnt = _router(x, g, mod, which_shift, w_router, b_router, dims, n_rows)
    counts = cnt[:, 0].astype(i32)
    padded = (counts + mb - 1) // mb * mb
    pend = jnp.cumsum(padded)
    start = pend - padded
    onehot = idx[:, :, None] == jnp.arange(ne, dtype=i32)[None, None, :]
    dest = jnp.sum(jnp.where(onehot, start[None, None, :], 0), axis=-1) + rank
    n_assign = n_rows * TOP_K
    nblk = (n_assign + ne * (mb - 1) + mb - 1) // mb
    pos = jnp.arange(TOP_K * n_rows, dtype=i32)
    spare = n_assign + jnp.arange(nblk * mb, dtype=i32) % mb
    slot_pos = spare.at[dest.reshape(-1)].set(pos, unique_indices=True)
    slot_tok = jnp.where(slot_pos < n_assign, slot_pos % n_rows, 0)
    blk_start = jnp.arange(nblk, dtype=i32) * mb
    block_e = jnp.minimum(jnp.sum((pend[None, :] <= blk_start[:, None]).astype(i32), axis=1), ne - 1)
    n_used = (pend[-1:] // mb).astype(i32)
    y_rows = _experts(h, slot_tok.reshape(nblk, mb), slot_pos.reshape(nblk, mb), block_e, n_used, wg, wu, wd,
                      layer, n_rows)
    shared = _shared(h, sg, su, sd)
    return _combine(y_rows, wts, shared, x, mod, which_shift + 2, g_final, dims, n_rows, final_norm)


def _rope_tables(seq, head_dim):
    half = head_dim // 2
    quarter = half // 2
    t = jnp.arange(seq, dtype=i32)
    rows = (t // GRID_W).astype(f32)
    cols = (t % GRID_W).astype(f32)
    freqs = ROPE_THETA ** (-jnp.arange(0, half, 2, dtype=f32) / half)
    ar = rows[:, None] * freqs[None, :]
    ac = cols[:, None] * freqs[None, :]
    cos = jnp.concatenate([jnp.cos(ar), jnp.cos(ar), jnp.cos(ac), jnp.cos(ac)], axis=1)
    sin = jnp.concatenate([-jnp.sin(ar), jnp.sin(ar), -jnp.sin(ac), jnp.sin(ac)], axis=1)
    reps = LANE // head_dim
    cos = jnp.tile(cos, (1, reps))
    sin = jnp.tile(sin, (1, reps))
    cos = jnp.concatenate([cos, jnp.ones((ROW_BLOCK, LANE), f32)], axis=0)
    sin = jnp.concatenate([sin, jnp.zeros((ROW_BLOCK, LANE), f32)], axis=0)
    return cos, sin, quarter


def kernel(x, c, ctx, c_ctx, w_ada, b_ada, g_norm_mix, g_norm_ffn, w_in_even, w_out_even, hgrn_lb_logits,
           g_hgrn_norm, win_sink, w_in_odd, w_out_odd, g_q_norm, g_k_norm, w_router, b_router,
           w_exp_gate, w_exp_up, w_exp_down, w_sh_gate, w_sh_up, w_sh_down, g_norm_final):
    nb, seq, d = x.shape
    lc = ctx.shape[1]
    assert seq % ROW_BLOCK == 0 and lc % ROW_BLOCK == 0 and seq % lc == 0
    t_lat = nb * seq
    t_all = t_lat + nb * lc
    bpb = seq // ROW_BLOCK
    nlb = t_lat // ROW_BLOCK
    dims = (nlb, bpb, nb)

    xs = jnp.concatenate([x.reshape(t_lat, d), ctx.reshape(nb * lc, d)], axis=0)
    cc = jnp.concatenate([c, c_ctx[None, :], jnp.zeros((8 - nb - 1, d), f32)], axis=0)

    lb_w = jax.nn.softmax(hgrn_lb_logits.astype(f32), axis=0)
    lower_bounds = jnp.cumsum(lb_w, axis=0)[1:] - lb_w[0]

    mod = _ada(cc, w_ada, b_ada, 0).reshape(8 * 6, 1, d)
    w_in = w_in_even[0].astype(bf16)
    proj = _normmm(xs, g_norm_mix[0], mod, 0, w_in[:, :HGRN_COLS], f32, dims, tn=1280)
    cos, sin, quarter = _rope_tables(seq, WIN_HEAD_DIM)
    epi = dict(blocks={0: (WIN_HEADS + WIN_KV_HEADS) * WIN_HEAD_DIM // LANE}, quarter=quarter, head_norm=False,
               q_slices=WIN_HEADS * WIN_HEAD_DIM // LANE, q_scale=WIN_HEAD_DIM ** -0.5)
    wproj = _normmm(xs, g_norm_mix[0], mod, 0, w_in[:, HGRN_COLS:], bf16, dims, tn=WIN_COLS,
                    epilogue=epi, tables=(cos, sin))
    o_f, o_b = _hgrn(proj, lower_bounds[0], nb, seq, lc)
    a_mix = _hgrn_out(o_f, o_b, proj, g_hgrn_norm[0])
    b_mix = _win_attn(wproj, win_sink[0], nb, seq, lc)
    xs = _outproj(a_mix, 0, b_mix, 0, w_out_even[0].astype(bf16), xs, mod, 2, dims, t_all)
    xs = _moe(xs, g_norm_ffn[0], mod, 3, w_router[0], b_router[0],
              w_exp_gate, w_exp_up, w_exp_down, 0,
              w_sh_gate[0].astype(bf16), w_sh_up[0].astype(bf16), w_sh_down[0].astype(bf16),
              g_norm_final, dims, t_all, False)

    mod = _ada(cc, w_ada, b_ada, 1).reshape(8 * 6, 1, d)
    cos, sin, quarter = _rope_tables(seq, GLB_HEAD_DIM)
    tn = 1536
    n_sl = tn // LANE
    rope_slices = GLB_HEADS + GLB_KV_HEADS
    epi = dict(blocks={0: min(rope_slices, n_sl), 1: max(rope_slices - n_sl, 0)}, quarter=quarter, head_norm=True,
               q_slices=GLB_HEADS, q_scale=GLB_HEAD_DIM ** -0.5)
    qkv = _normmm(xs, g_norm_mix[1], mod, 0, w_in_odd[0].astype(bf16), bf16, dims, tn=tn,
                  epilogue=epi, tables=(cos, sin), gains=(g_q_norm[0], g_k_norm[0]))
    att = _glb_attn(qkv, nb, seq, lc)
    xl = _outproj(att, 0, att, 1, w_out_odd[0].astype(bf16), xs, mod, 2, dims, t_lat)
    out = _moe(xl, g_norm_ffn[1], mod, 3, w_router[1], b_router[1],
               w_exp_gate, w_exp_up, w_exp_down, 1,
               w_sh_gate[1].astype(bf16), w_sh_up[1].astype(bf16), w_sh_down[1].astype(bf16),
               g_norm_final, dims, t_lat, True)
    return out.reshape(nb, seq, d)
```

```python
import functools

import jax
import jax.numpy as jnp
from jax import lax
from jax.experimental import pallas as pl
from jax.experimental.pallas import tpu as pltpu

f32 = jnp.float32
bf16 = jnp.bfloat16
i32 = jnp.int32
u32 = jnp.uint32

EPS = 1e-6
ROPE_THETA = 10000.0
GRID_W = 64

HGRN_DK = 128
HGRN_HEADS = 8
HGRN_CHUNK = 64
HGRN_SUB = 16
HGRN_HEADS_PER_STEP = 4
WIN_HEAD_DIM = 64
WIN_HEADS = 16
WIN_KV_HEADS = 2
WIN_BLOCK = 128
GLB_HEAD_DIM = 128
GLB_HEADS = 16
GLB_KV_HEADS = 4
Q_BLOCK = 256
N_EXPERTS = 64
N_GROUPS = 8
TOPK_GROUPS = 4
TOP_K = 8
ROUTED_SCALE = 2.5
MOE_BLOCK = 256

LANE = 128
ROW_BLOCK = 256
VMEM_LIMIT = 56 * 1024 * 1024

A_K = HGRN_HEADS * HGRN_DK
HGRN_COLS = 5 * A_K
WIN_COLS = WIN_HEADS * WIN_HEAD_DIM + 2 * WIN_KV_HEADS * WIN_HEAD_DIM


def _params(sem, vmem=VMEM_LIMIT):
    return pltpu.CompilerParams(dimension_semantics=sem, vmem_limit_bytes=vmem)


def _silu(x):
    return x * jax.nn.sigmoid(x)


def _dot(a, b):
    return jnp.dot(a, b, preferred_element_type=f32)


def _dot_nt(a, b):
    return lax.dot_general(a, b, (((1,), (1,)), ((), ())), preferred_element_type=f32)


def _pack_halves(x):
    n = x.shape[1] // 2
    lo = lax.bitcast_convert_type(x[:, :n].astype(bf16).astype(f32), u32)
    hi = lax.bitcast_convert_type(x[:, n:].astype(bf16).astype(f32), u32)
    return (lo >> 16) | hi


def _unpack_halves(w):
    lo = lax.bitcast_convert_type(w << 16, f32)
    hi = lax.bitcast_convert_type(w & jnp.uint32(0xFFFF0000), f32)
    return lo, hi


def _rows_to_tiles(w):
    n = w.shape[1] // LANE
    return pltpu.einshape("stl->tsl", jnp.stack([w[:, s * LANE:(s + 1) * LANE] for s in range(n)], axis=0))


def _tiles_to_rows(t):
    y = pltpu.einshape("tsl->stl", t)
    return jnp.concatenate([y[s] for s in range(t.shape[1])], axis=1)


def _dot_tn(a, b):
    return lax.dot_general(a, b, (((0,), (0,)), ((), ())), preferred_element_type=f32)


def _ada_kernel(c_ref, w_ref, b_ref, o_ref):
    a = _silu(c_ref[...]).astype(bf16)
    o_ref[...] = _dot(a, w_ref[...].astype(bf16)) + b_ref[...]


def _ada(c8, w, b, layer, tn=768):
    m, d = c8.shape
    n = w.shape[2]
    return pl.pallas_call(
        _ada_kernel,
        grid=(n // tn,),
        in_specs=[pl.BlockSpec((m, d), lambda j: (0, 0)),
                  pl.BlockSpec((None, d, tn), lambda j: (layer, 0, j)),
                  pl.BlockSpec((None, 1, tn), lambda j: (layer, 0, j))],
        out_specs=pl.BlockSpec((m, tn), lambda j: (0, j)),
        out_shape=jax.ShapeDtypeStruct((m, n), f32),
        compiler_params=_params(("arbitrary",)),
    )(c8, w, b.reshape(b.shape[0], 1, n))


def _norm_mod(x, g, sc, sh):
    y = x * lax.rsqrt(jnp.mean(x * x, axis=-1, keepdims=True) + EPS) * g
    return y * (1.0 + sc) + sh


def _rope_slice(x, cos, sin, quarter):
    lane = lax.broadcasted_iota(i32, x.shape, 1)
    up = pltpu.roll(x, LANE - quarter, axis=1)
    dn = pltpu.roll(x, quarter, axis=1)
    partner = jnp.where(lane % (2 * quarter) < quarter, up, dn)
    return x * cos + partner * sin


def _normmm_kernel(x_ref, g_ref, sc_ref, sh_ref, w_ref, *rest, epilogue, tn):
    if epilogue is None:
        (o_ref,) = rest
    elif epilogue["head_norm"]:
        cos_ref, sin_ref, gq_ref, gk_ref, o_ref = rest
    else:
        cos_ref, sin_ref, o_ref = rest
    j = pl.program_id(0)
    h = _norm_mod(x_ref[...], g_ref[...], sc_ref[...], sh_ref[...]).astype(bf16)
    y = _dot(h, w_ref[...])
    if epilogue is None:
        o_ref[...] = y.astype(o_ref.dtype)
        return

    n_sl = tn // LANE
    for jb in range(epilogue["n_col_blocks"]):
        n_rope = epilogue["blocks"].get(jb, 0)

        @pl.when(j == jb)
        def _(jb=jb, n_rope=n_rope):
            cos = cos_ref[...]
            sin = sin_ref[...]
            outs = []
            for s in range(n_sl):
                ys = y[:, s * LANE:(s + 1) * LANE]
                if s < n_rope:
                    col = jb * n_sl + s
                    is_q = col < epilogue["q_slices"]
                    if epilogue["head_norm"]:
                        gain = gq_ref[...] if is_q else gk_ref[...]
                        ys = ys * lax.rsqrt(jnp.mean(ys * ys, axis=-1, keepdims=True) + EPS) * gain
                    ys = _rope_slice(ys, cos, sin, epilogue["quarter"])
                    if is_q:
                        ys = ys * epilogue["q_scale"]
                outs.append(ys.astype(o_ref.dtype))
            o_ref[...] = jnp.concatenate(outs, axis=1)


def _mod_row(i, n_lat_blocks, blocks_per_batch, n_batch):
    return jnp.where(i < n_lat_blocks, i // blocks_per_batch, n_batch)


def _normmm(x, g, mod, which, w, out_dtype, dims, tn, epilogue=None, tables=None, gains=None):
    t, d = x.shape
    n = w.shape[1]
    tm = ROW_BLOCK
    nlb, bpb, nb = dims
    mrow = lambda i: _mod_row(i, nlb, bpb, nb)
    in_specs = [pl.BlockSpec((tm, d), lambda j, i: (i, 0)),
                pl.BlockSpec((1, d), lambda j, i: (0, 0)),
                pl.BlockSpec((None, 1, d), lambda j, i: (mrow(i) * 6 + which + 1, 0, 0)),
                pl.BlockSpec((None, 1, d), lambda j, i: (mrow(i) * 6 + which, 0, 0)),
                pl.BlockSpec((d, tn), lambda j, i: (0, j))]
    args = [x, g.reshape(1, d), mod, mod, w]
    if epilogue is not None:
        cos, sin = tables
        tab = lambda j, i: (jnp.where(i < nlb, i % bpb, bpb), 0)
        in_specs += [pl.BlockSpec((tm, LANE), tab), pl.BlockSpec((tm, LANE), tab)]
        args += [cos, sin]
        epilogue = dict(epilogue, n_col_blocks=n // tn)
        if epilogue["head_norm"]:
            in_specs += [pl.BlockSpec((1, LANE), lambda j, i: (0, 0))] * 2
            args += [gains[0].reshape(1, LANE), gains[1].reshape(1, LANE)]
    return pl.pallas_call(
        functools.partial(_normmm_kernel, epilogue=epilogue, tn=tn),
        grid=(n // tn, t // tm),
        in_specs=in_specs,
        out_specs=pl.BlockSpec((tm, tn), lambda j, i: (i, j)),
        out_shape=jax.ShapeDtypeStruct((t, n), out_dtype),
        compiler_params=_params(("parallel", "parallel")),
    )(*args)


def _gla_chunk(q_raw, v, f_raw, lb, st_ref, a_scr, rev):
    c = HGRN_CHUNK
    sub = HGRN_SUB
    q = _silu(q_raw)
    f = lb + (1.0 - lb) * jax.nn.sigmoid(f_raw)
    kk = 1.0 - f
    g = jnp.log(f)
    row = lax.broadcasted_iota(i32, (c, c), 0)
    col = lax.broadcasted_iota(i32, (c, c), 1)
    tri = (col >= row) if rev else (col <= row)
    b = jnp.dot(tri.astype(f32), g, preferred_element_type=f32, precision=lax.Precision.HIGHEST)
    b_end = b[0:1] if rev else b[c - 1:c]
    st = st_ref[...]
    o = _dot_nt((q * jnp.exp(b)).astype(bf16), st.astype(bf16))

    vb = v.astype(bf16)
    nsub = c // sub
    lane16 = lax.broadcasted_iota(i32, (sub, sub), 1)
    row16 = lax.broadcasted_iota(i32, (sub, sub), 0)
    diag_ok = (row16 <= lane16) if rev else (row16 >= lane16)
    rsub = lax.broadcasted_iota(i32, (c, sub), 0) // sub
    for jb in range(nsub):
        js = slice(jb * sub, (jb + 1) * sub)
        m_j = b[jb * sub:jb * sub + 1] if rev else b[(jb + 1) * sub - 1:(jb + 1) * sub]
        kd = kk[js] * jnp.exp(m_j - b[js])
        qd = q * jnp.exp(jnp.minimum(b - m_j, 0.0))
        a_col = _dot_nt(qd.astype(bf16), kd.astype(bf16))
        off_ok = (rsub < jb) if rev else (rsub > jb)
        a_col = jnp.where(off_ok, a_col, 0.0)
        qi, bi, ki = q[js], b[js], kk[js]
        a_dd = jnp.zeros((sub, sub), f32)
        for s in range(sub):
            e = jnp.exp(bi - bi[s:s + 1])
            col_s = jnp.sum(qi * ki[s:s + 1] * e, axis=-1, keepdims=True)
            a_dd = jnp.where(lane16 == s, col_s, a_dd)
        a_dd = jnp.where(diag_ok, a_dd, 0.0)
        pieces = [a_dd if ib == jb else a_col[ib * sub:(ib + 1) * sub] for ib in range(nsub)]
        a_scr[:, js] = jnp.concatenate(pieces, axis=0)
    o = o + _dot(a_scr[...].astype(bf16), vb)
    kdec = kk * jnp.exp(b_end - b)
    st_ref[...] = jnp.exp(b_end) * st + _dot_tn(vb, kdec.astype(bf16))
    return o


def _hgrn_kernel(qf_ref, vf_ref, ff_ref, qb_ref, vb_ref, fb_ref, lb_ref, of_ref, ob_ref,
                 stf_ref, stb_ref, af_scr, ab_scr):
    @pl.when(pl.program_id(2) == 0)
    def _():
        stf_ref[...] = jnp.zeros_like(stf_ref)
        stb_ref[...] = jnp.zeros_like(stb_ref)

    for j in range(HGRN_HEADS_PER_STEP):
        sl = slice(j * HGRN_DK, (j + 1) * HGRN_DK)
        lb = lb_ref[:, sl]
        of_ref[:, sl] = _gla_chunk(qf_ref[:, sl], vf_ref[:, sl], ff_ref[:, sl], lb, stf_ref.at[j], af_scr.at[j], False)
        ob_ref[:, sl] = _gla_chunk(qb_ref[:, sl], vb_ref[:, sl], fb_ref[:, sl], lb, stb_ref.at[j], ab_scr.at[j], True)


def _hgrn(proj, lb, n_batch, seq, ctx_len):
    t = proj.shape[0]
    c = HGRN_CHUNK
    hp = HGRN_HEADS_PER_STEP
    wblk = hp * HGRN_DK
    ngrp = HGRN_HEADS // hp
    ncc, ncl = ctx_len // c, seq // c
    lat_chunks = n_batch * ncl

    def fwd_blk(b, s):
        return jnp.where(s < ncc, lat_chunks + b * ncc + s, b * ncl + (s - ncc))

    def bwd_blk(b, s):
        return jnp.where(s < ncc, lat_chunks + b * ncc + (ncc - 1 - s), b * ncl + (ncl - 1 - (s - ncc)))

    def spec(blk, section):
        return pl.BlockSpec((c, wblk), lambda b, h, s: (blk(b, s), section * ngrp + h))

    def ospec(blk):
        return pl.BlockSpec((c, wblk), lambda b, h, s: (blk(b, s), h))

    out = jax.ShapeDtypeStruct((t, A_K), f32)
    return pl.pallas_call(
        _hgrn_kernel,
        grid=(n_batch, ngrp, ncc + ncl),
        in_specs=[spec(fwd_blk, 0), spec(fwd_blk, 1), spec(fwd_blk, 2),
                  spec(bwd_blk, 0), spec(bwd_blk, 1), spec(bwd_blk, 3),
                  pl.BlockSpec((1, wblk), lambda b, h, s: (0, h))],
        out_specs=[ospec(fwd_blk), ospec(bwd_blk)],
        out_shape=[out, out],
        scratch_shapes=[pltpu.VMEM((hp, HGRN_DK, HGRN_DK), f32), pltpu.VMEM((hp, HGRN_DK, HGRN_DK), f32),
                        pltpu.VMEM((hp, c, c), f32), pltpu.VMEM((hp, c, c), f32)],
        compiler_params=_params(("parallel", "parallel", "arbitrary")),
    )(proj, proj, proj, proj, proj, proj, lb.reshape(1, A_K))


def _hgrn_out_kernel(of_ref, ob_ref, g_ref, gain_ref, o_ref):
    gain = gain_ref[...]
    outs = []
    for h in range(HGRN_HEADS):
        sl = slice(h * HGRN_DK, (h + 1) * HGRN_DK)
        o = of_ref[:, sl] + ob_ref[:, sl]
        o = o * lax.rsqrt(jnp.mean(o * o, axis=-1, keepdims=True) + EPS) * gain
        outs.append((o * _silu(g_ref[:, sl])).astype(bf16))
    o_ref[...] = jnp.concatenate(outs, axis=1)


def _hgrn_out(o_f, o_b, proj, gain):
    t = o_f.shape[0]
    tm = ROW_BLOCK
    return pl.pallas_call(
        _hgrn_out_kernel,
        grid=(t // tm,),
        in_specs=[pl.BlockSpec((tm, A_K), lambda i: (i, 0)),
                  pl.BlockSpec((tm, A_K), lambda i: (i, 0)),
                  pl.BlockSpec((tm, A_K), lambda i: (i, 4)),
                  pl.BlockSpec((1, HGRN_DK), lambda i: (0, 0))],
        out_specs=pl.BlockSpec((tm, A_K), lambda i: (i, 0)),
        out_shape=jax.ShapeDtypeStruct((t, A_K), bf16),
        compiler_params=_params(("parallel",)),
    )(o_f, o_b, proj, gain.reshape(1, HGRN_DK))


def _win_kernel(sink_ref, q_ref, kp_ref, kc_ref, kn_ref, vp_ref, vc_ref, vn_ref, kx_ref, vx_ref, o_ref,
                *, n_lat_blocks):
    n = pl.program_id(1)
    w = WIN_BLOCK
    dh = WIN_HEAD_DIM
    is_lat = n < n_lat_blocks
    ri = lax.broadcasted_iota(i32, (w, w), 0)
    ci = lax.broadcasted_iota(i32, (w, w), 1)
    ok_p = (ci >= ri) & is_lat & (n > 0)
    ok_c = jnp.broadcast_to(is_lat, (w, w))
    ok_n = (ci <= ri) & is_lat & (n < n_lat_blocks - 1)
    valid = jnp.concatenate([ok_p, ok_c, ok_n], axis=1)
    kwin = jnp.concatenate([kp_ref[...], kc_ref[...], kn_ref[...]], axis=0)
    vwin = jnp.concatenate([vp_ref[...], vc_ref[...], vn_ref[...]], axis=0)
    kx = kx_ref[...]
    vx = vx_ref[...]
    g = WIN_HEADS // WIN_KV_HEADS
    outs = []
    for h in range(WIN_HEADS):
        kv = h // g
        ks = slice(kv * dh, (kv + 1) * dh)
        qh = q_ref[:, h * dh:(h + 1) * dh]
        sw = jnp.where(valid, _dot_nt(qh, kwin[:, ks]), -jnp.inf)
        sx = _dot_nt(qh, kx[:, ks])
        sk = sink_ref[h]
        m = jnp.maximum(jnp.maximum(jnp.max(sw, axis=-1, keepdims=True), jnp.max(sx, axis=-1, keepdims=True)), sk)
        pw = jnp.exp(sw - m)
        px = jnp.exp(sx - m)
        den = jnp.sum(pw, axis=-1, keepdims=True) + jnp.sum(px, axis=-1, keepdims=True) + jnp.exp(sk - m)
        o = _dot(pw.astype(bf16), vwin[:, ks]) + _dot(px.astype(bf16), vx[:, ks])
        outs.append((o / den).astype(bf16))
    o_ref[...] = jnp.concatenate(outs, axis=1)


def _win_attn(wproj, sink, n_batch, seq, ctx_len):
    t = wproj.shape[0]
    w = WIN_BLOCK
    nlb = seq // w
    ncb = ctx_len // w
    lat_blocks = n_batch * nlb
    qc = WIN_HEADS * WIN_HEAD_DIM // LANE
    kcol, vcol = qc, qc + 1

    def qrow(b, n):
        return jnp.where(n < nlb, b * nlb + n, lat_blocks + b * ncb + (n - nlb))

    def krow(off):
        def f(b, n):
            return b * nlb + jnp.clip(n + off, 0, nlb - 1)
        return f

    def kspec(off, colblk):
        return pl.BlockSpec((w, LANE), lambda b, n: (krow(off)(b, n), colblk))

    def xspec(colblk):
        return pl.BlockSpec((ctx_len, LANE), lambda b, n: ((n_batch * seq) // ctx_len + b, colblk))

    return pl.pallas_call(
        functools.partial(_win_kernel, n_lat_blocks=nlb),
        grid=(n_batch, nlb + ncb),
        in_specs=[pl.BlockSpec(memory_space=pltpu.SMEM),
                  pl.BlockSpec((w, qc * LANE), lambda b, n: (qrow(b, n), 0)),
                  kspec(-1, kcol), kspec(0, kcol), kspec(1, kcol),
                  kspec(-1, vcol), kspec(0, vcol), kspec(1, vcol),
                  xspec(kcol), xspec(vcol)],
        out_specs=pl.BlockSpec((w, qc * LANE), lambda b, n: (qrow(b, n), 0)),
        out_shape=jax.ShapeDtypeStruct((t, qc * LANE), bf16),
        compiler_params=_params(("parallel", "arbitrary")),
    )(sink.astype(f32), wproj, wproj, wproj, wproj, wproj, wproj, wproj, wproj, wproj)


GLB_KEY_CHUNK = 256


def _glb_kernel(q_ref, kx_ref, kl_ref, vx_ref, vl_ref, o_ref, m_scr, l_scr, acc_scr):
    dh = GLB_HEAD_DIM
    g = GLB_HEADS // GLB_KV_HEADS
    ck = GLB_KEY_CHUNK
    tq = q_ref.shape[0]
    q = jnp.concatenate([q_ref[:, i * dh:(i + 1) * dh] for i in range(g)], axis=0)

    def update(k, v, first):
        s = _dot_nt(q, k)
        reps = s.shape[1] // LANE
        mx = jnp.broadcast_to(jnp.max(s, axis=-1, keepdims=True), (g * tq, LANE))
        if first:
            m_new = mx
            p = jnp.exp(s - jnp.concatenate([m_new] * reps, axis=1))
            l_scr[...] = jnp.broadcast_to(jnp.sum(p, axis=-1, keepdims=True), (g * tq, LANE))
            acc_scr[...] = _dot(p.astype(bf16), v)
        else:
            m_old = m_scr[...]
            m_new = jnp.maximum(m_old, mx)
            alpha = jnp.exp(m_old - m_new)
            p = jnp.exp(s - jnp.concatenate([m_new] * reps, axis=1))
            l_scr[...] = alpha * l_scr[...] + jnp.broadcast_to(jnp.sum(p, axis=-1, keepdims=True), (g * tq, LANE))
            acc_scr[...] = alpha * acc_scr[...] + _dot(p.astype(bf16), v)
        m_scr[...] = m_new

    chunks = [(kx_ref, vx_ref, c, min(ck, kx_ref.shape[0] - c)) for c in range(0, kx_ref.shape[0], ck)]
    chunks += [(kl_ref, vl_ref, c, min(ck, kl_ref.shape[0] - c)) for c in range(0, kl_ref.shape[0], ck)]
    for n, (kr, vr, c0, sz) in enumerate(chunks):
        update(kr[c0:c0 + sz, :], vr[c0:c0 + sz, :], n == 0)
    o = acc_scr[...] / l_scr[...]
    o_ref[...] = jnp.concatenate([o[i * tq:(i + 1) * tq] for i in range(g)], axis=1).astype(bf16)


def _glb_attn(qkv, n_batch, seq, ctx_len):
    tq = Q_BLOCK
    dh = GLB_HEAD_DIM
    g = GLB_HEADS // GLB_KV_HEADS
    nq = seq // tq
    kcol0 = GLB_HEADS
    vcol0 = GLB_HEADS + GLB_KV_HEADS
    ctx_blk0 = (n_batch * seq) // ctx_len
    return pl.pallas_call(
        _glb_kernel,
        grid=(n_batch, GLB_KV_HEADS, nq),
        in_specs=[pl.BlockSpec((tq, g * dh), lambda b, k, i: (b * nq + i, k)),
                  pl.BlockSpec((ctx_len, dh), lambda b, k, i: (ctx_blk0 + b, kcol0 + k)),
                  pl.BlockSpec((seq, dh), lambda b, k, i: (b, kcol0 + k)),
                  pl.BlockSpec((ctx_len, dh), lambda b, k, i: (ctx_blk0 + b, vcol0 + k)),
                  pl.BlockSpec((seq, dh), lambda b, k, i: (b, vcol0 + k))],
        out_specs=pl.BlockSpec((tq, g * dh), lambda b, k, i: (b * nq + i, k)),
        out_shape=jax.ShapeDtypeStruct((n_batch * seq, GLB_HEADS * dh), bf16),
        scratch_shapes=[pltpu.VMEM((g * tq, LANE), f32), pltpu.VMEM((g * tq, LANE), f32), pltpu.VMEM((g * tq, dh), f32)],
        compiler_params=_params(("parallel", "parallel", "arbitrary")),
    )(qkv, qkv, qkv, qkv, qkv)


def _outproj_kernel(a1_ref, a2_ref, w1_ref, w2_ref, x_ref, gate_ref, o_ref):
    y = _dot(a1_ref[...], w1_ref[...]) + _dot(a2_ref[...], w2_ref[...])
    o_ref[...] = x_ref[...] + gate_ref[...] * y


def _outproj(a1, c1, a2, c2, w, x, mod, which, dims, n_rows, tn=1024):
    d = x.shape[1]
    kh = w.shape[0] // 2
    tm = ROW_BLOCK
    nlb, bpb, nb = dims
    mrow = lambda i: _mod_row(i, nlb, bpb, nb)
    return pl.pallas_call(
        _outproj_kernel,
        grid=(d // tn, n_rows // tm),
        in_specs=[pl.BlockSpec((tm, kh), lambda j, i: (i, c1)),
                  pl.BlockSpec((tm, kh), lambda j, i: (i, c2)),
                  pl.BlockSpec((kh, tn), lambda j, i: (0, j)),
                  pl.BlockSpec((kh, tn), lambda j, i: (1, j)),
                  pl.BlockSpec((tm, tn), lambda j, i: (i, j)),
                  pl.BlockSpec((None, 1, tn), lambda j, i: (mrow(i) * 6 + which, 0, j))],
        out_specs=pl.BlockSpec((tm, tn), lambda j, i: (i, j)),
        out_shape=jax.ShapeDtypeStruct((n_rows, d), f32),
        compiler_params=_params(("parallel", "parallel")),
    )(a1, a2, w, w, x, mod)


def _router_kernel(x_ref, g_ref, sc_ref, sh_ref, wr_ref, br_ref, u_ref,
                   h_ref, idx_ref, wt_ref, rank_ref, cnt_ref, carry_scr):
    i = pl.program_id(0)
    tm = x_ref.shape[0]
    ne, ng, pg = N_EXPERTS, N_GROUPS, N_EXPERTS // N_GROUPS

    @pl.when(i == 0)
    def _():
        carry_scr[...] = jnp.zeros_like(carry_scr)

    h = _norm_mod(x_ref[...], g_ref[...], sc_ref[...], sh_ref[...])
    h_ref[...] = _rows_to_tiles(_pack_halves(h))
    logits = lax.dot_general(wr_ref[...], h, (((1,), (1,)), ((), ())), preferred_element_type=f32,
                             precision=lax.Precision.HIGHEST)
    s = jax.nn.sigmoid(logits)
    s3 = s.reshape(ng, pg, tm)
    b3 = (s + br_ref[...]).reshape(ng, pg, tm)
    gi = lax.broadcasted_iota(i32, (ng, pg, tm), 0)
    pi = lax.broadcasted_iota(i32, (ng, pg, tm), 1)
    neg = -jnp.inf
    m1 = jnp.max(b3, axis=1, keepdims=True)
    first = jnp.min(jnp.where(b3 == m1, pi, pg), axis=1, keepdims=True)
    m2 = jnp.max(jnp.where(pi == first, neg, b3), axis=1, keepdims=True)
    gs = m1 + m2
    g2 = lax.broadcasted_iota(i32, (ng, 1, tm), 0)
    gmask = jnp.zeros((ng, 1, tm), jnp.bool_)
    for _ in range(TOPK_GROUPS):
        m = jnp.max(gs, axis=0, keepdims=True)
        fi = jnp.min(jnp.where(gs == m, g2, ng), axis=0, keepdims=True)
        pick = g2 == fi
        gmask = gmask | pick
        gs = jnp.where(pick, neg, gs)
    cand = jnp.where(gmask, b3, neg)
    eid = gi * pg + pi
    sel = jnp.zeros((ng, pg, tm), jnp.bool_)
    picks, idxs, wts = [], [], []
    for _ in range(TOP_K):
        m = jnp.max(jnp.max(cand, axis=0, keepdims=True), axis=1, keepdims=True)
        fi = jnp.min(jnp.min(jnp.where(cand == m, eid, ne), axis=0, keepdims=True), axis=1, keepdims=True)
        pick = eid == fi
        picks.append(pick)
        idxs.append(fi.reshape(1, tm))
        wts.append(jnp.sum(jnp.sum(jnp.where(pick, s3, 0.0), axis=0, keepdims=True), axis=1,
                           keepdims=True).reshape(1, tm))
        sel = sel | pick
        cand = jnp.where(pick, neg, cand)
    wsum = wts[0]
    for k in range(1, TOP_K):
        wsum = wsum + wts[k]
    idx_ref[...] = jnp.concatenate(idxs, axis=0)
    wt_ref[...] = jnp.concatenate([wk / wsum * ROUTED_SCALE for wk in wts], axis=0)
    sel2 = jnp.where(sel, 1.0, 0.0).reshape(ne, tm)
    prefix = (_dot(sel2.astype(bf16), u_ref[...]) + carry_scr[...]).reshape(ng, pg, tm)
    ranks = [jnp.sum(jnp.sum(jnp.where(pk, prefix, 0.0), axis=0, keepdims=True), axis=1,
                     keepdims=True).reshape(1, tm) for pk in picks]
    rank_ref[...] = jnp.concatenate(ranks, axis=0).astype(i32)
    total = carry_scr[...] + jnp.sum(sel2, axis=1, keepdims=True)
    carry_scr[...] = total
    cnt_ref[...] = total


def _router(x, g, mod, which, w_router, b_router, dims, n_rows):
    d = x.shape[1]
    tm = ROW_BLOCK
    ne = N_EXPERTS
    nlb, bpb, nb = dims
    mrow = lambda i: _mod_row(i, nlb, bpb, nb)
    tri = (jnp.arange(tm)[:, None] < jnp.arange(tm)[None, :]).astype(bf16)
    return pl.pallas_call(
        _router_kernel,
        grid=(n_rows // tm,),
        in_specs=[pl.BlockSpec((tm, d), lambda i: (i, 0)),
                  pl.BlockSpec((1, d), lambda i: (0, 0)),
                  pl.BlockSpec((None, 1, d), lambda i: (mrow(i) * 6 + which + 1, 0, 0)),
                  pl.BlockSpec((None, 1, d), lambda i: (mrow(i) * 6 + which, 0, 0)),
                  pl.BlockSpec((ne, d), lambda i: (0, 0)),
                  pl.BlockSpec((ne, 1), lambda i: (0, 0)),
                  pl.BlockSpec((tm, tm), lambda i: (0, 0))],
        out_specs=[pl.BlockSpec((tm, d // 2 // LANE, LANE), lambda i: (i, 0, 0)),
                   pl.BlockSpec((TOP_K, tm), lambda i: (0, i)),
                   pl.BlockSpec((TOP_K, tm), lambda i: (0, i)),
                   pl.BlockSpec((TOP_K, tm), lambda i: (0, i)),
                   pl.BlockSpec((ne, 1), lambda i: (0, 0))],
        out_shape=[jax.ShapeDtypeStruct((n_rows, d // 2 // LANE, LANE), u32),
                   jax.ShapeDtypeStruct((TOP_K, n_rows), i32),
                   jax.ShapeDtypeStruct((TOP_K, n_rows), f32),
                   jax.ShapeDtypeStruct((TOP_K, n_rows), i32),
                   jax.ShapeDtypeStruct((ne, 1), f32)],
        scratch_shapes=[pltpu.VMEM((ne, 1), f32)],
        compiler_params=_params(("arbitrary",)),
    )(x, g.reshape(1, d), mod, mod, w_router.T, b_router.reshape(ne, 1).astype(f32), tri)


EXPERT_BURSTS = 8


def _expert_kernel(be_ref, nu_ref, tok0_ref, tokn_ref, posp_ref, h_hbm, wg_ref, wu_ref, wd_ref, y_hbm,
                   xbuf, ybuf, yrow, wgc, wuc, wdc, sem_g, sem_s):
    i = pl.program_id(0)
    nblk = pl.num_programs(0) - 1
    n_used = nu_ref[0]
    slot = i % 2
    mb = MOE_BLOCK
    de = wgc.shape[1]
    dp = yrow.shape[1]
    spare0 = y_hbm.shape[0] - 2 * mb
    per = mb // (EXPERT_BURSTS // 2)

    def gather_row(idx_ref, r, dst_slot, priority=0):
        pltpu.make_async_copy(h_hbm.at[pl.ds(idx_ref[0, 0, r], 1)], xbuf.at[dst_slot, pl.ds(r, 1)],
                              sem_g.at[dst_slot]).start(priority=priority)

    def gather_burst(b):
        for r in range(b * per, (b + 1) * per):
            gather_row(tokn_ref, r, 1 - slot, r % 2)

    def scatter_burst(b):
        for r in range(b * per, (b + 1) * per):
            p = jnp.where(i == 0, spare0 + r, posp_ref[0, 0, r])
            pltpu.make_async_copy(ybuf.at[1 - slot, pl.ds(r, 1)], y_hbm.at[pl.ds(p, 1)],
                                  sem_s.at[slot]).start(priority=r % 2)

    def wait_scatter(parity):
        pltpu.make_async_copy(ybuf.at[0], y_hbm.at[pl.ds(0, mb)], sem_s.at[parity]).wait()

    @pl.when(i == 0)
    def _():
        ybuf[1] = jnp.zeros(ybuf.shape[1:], u32)

        def body(r, carry):
            gather_row(tok0_ref, r, 0)
            pltpu.make_async_copy(ybuf.at[1, pl.ds(r, 1)], y_hbm.at[pl.ds(spare0 + mb + r, 1)], sem_s.at[1]).start()
            return carry
        lax.fori_loop(0, mb, body, 0)

    @pl.when(i <= n_used)
    def _():
        pltpu.make_async_copy(h_hbm.at[pl.ds(0, mb)], xbuf.at[slot], sem_g.at[slot]).wait()

    ib = jnp.minimum(i, nblk - 1)
    changed = (i == 0) | (be_ref[ib] != be_ref[jnp.maximum(ib - 1, 0)])

    @pl.when(changed & (i < n_used))
    def _():
        wgc[...] = wg_ref[...].astype(bf16)
        wuc[...] = wu_ref[...].astype(bf16)
        wdc[...] = wd_ref[...].astype(bf16)

    @pl.when(i < n_used)
    def _():
        lo, hi = _unpack_halves(_tiles_to_rows(xbuf[slot]))
        x = jnp.concatenate([lo.astype(bf16), hi.astype(bf16)], axis=1)
        nh = EXPERT_BURSTS // 4
        hw = de // nh
        hids = []
        for c in range(nh):
            gate = _dot(x, wgc[:, c * hw:(c + 1) * hw])
            gather_burst(2 * c)
            up = _dot(x, wuc[:, c * hw:(c + 1) * hw])
            gather_burst(2 * c + 1)
            hids.append((_silu(gate) * up).astype(bf16))
        hid = jnp.concatenate(hids, axis=1)
        nd = EXPERT_BURSTS // 2
        dw = dp // nd
        for c in range(nd):
            wd_c = jnp.concatenate([wdc[:, c * dw:(c + 1) * dw], wdc[:, dp + c * dw:dp + (c + 1) * dw]], axis=1)
            yrow[:, c * dw:(c + 1) * dw] = _pack_halves(_dot(hid, wd_c))
            scatter_burst(c)
        wait_scatter(1 - slot)
        ybuf[slot] = _rows_to_tiles(yrow[...])

    @pl.when(i == n_used)
    def _():
        wait_scatter(1 - slot)
        for b in range(EXPERT_BURSTS // 2):
            scatter_burst(b)
        wait_scatter(slot)


def _experts(h, slot_tok, slot_pos, block_e, n_used, wg, wu, wd, layer, n_rows):
    tile = h.shape[1:]
    dp = tile[0] * tile[1]
    d = wg.shape[2]
    nblk = slot_tok.shape[0]
    de = wg.shape[3]
    mb = MOE_BLOCK
    last = nblk - 1
    smem = lambda f: pl.BlockSpec((1, 1, mb), f, memory_space=pltpu.SMEM)
    wspec = lambda shape: pl.BlockSpec((None,) + shape, lambda i, be, nu: (layer, be[jnp.minimum(i, last)], 0, 0))
    grid_spec = pltpu.PrefetchScalarGridSpec(
        num_scalar_prefetch=2,
        grid=(nblk + 1,),
        in_specs=[smem(lambda i, be, nu: (0, 0, 0)),
                  smem(lambda i, be, nu: (jnp.minimum(i + 1, last), 0, 0)),
                  smem(lambda i, be, nu: (jnp.clip(i - 1, 0, last), 0, 0)),
                  pl.BlockSpec(memory_space=pl.ANY),
                  wspec((None, d, de)), wspec((None, d, de)), wspec((None, de, d))],
        out_specs=pl.BlockSpec(memory_space=pl.ANY),
        scratch_shapes=[pltpu.VMEM((2, mb) + tile, u32), pltpu.VMEM((2, mb) + tile, u32), pltpu.VMEM((mb, dp), u32),
                        pltpu.VMEM((d, de), bf16), pltpu.VMEM((d, de), bf16), pltpu.VMEM((de, d), bf16),
                        pltpu.SemaphoreType.DMA((2,)), pltpu.SemaphoreType.DMA((2,))],
    )
    st = slot_tok.reshape(nblk, 1, mb)
    sp = slot_pos.reshape(nblk, 1, mb)
    return pl.pallas_call(
        _expert_kernel,
        grid_spec=grid_spec,
        out_shape=jax.ShapeDtypeStruct((n_rows * TOP_K + 2 * mb,) + tile, u32),
        compiler_params=_params(("arbitrary",)),
    )(block_e, n_used, st, st, sp, h, wg, wu, wd)


COMBINE_ROWS = 128


def _combine_kernel(*refs, final_norm):
    y_refs = refs[:TOP_K]
    w_ref, h_ref, sg_ref, su_ref, sd_ref, x_ref, gate_ref, gf_ref, o_ref = refs[TOP_K:]
    lo, hi = _unpack_halves(_tiles_to_rows(h_ref[...]))
    hb = jnp.concatenate([lo.astype(bf16), hi.astype(bf16)], axis=1)
    hid = _silu(_dot(hb, sg_ref[...])) * _dot(hb, su_ref[...])
    acc = _dot(hid.astype(bf16), sd_ref[...])
    shape = y_refs[0].shape
    acc_lo = acc_hi = 0.0
    for k in range(TOP_K):
        lo, hi = _unpack_halves(y_refs[k][...])
        wk = jnp.broadcast_to(w_ref[:, k:k + 1, :], shape)
        acc_lo = acc_lo + lo * wk
        acc_hi = acc_hi + hi * wk
    acc = acc + jnp.concatenate([_tiles_to_rows(acc_lo), _tiles_to_rows(acc_hi)], axis=1)
    out = x_ref[...] + gate_ref[...] * acc
    if final_norm:
        out = out * lax.rsqrt(jnp.mean(out * out, axis=-1, keepdims=True) + EPS) * gf_ref[...]
    o_ref[...] = out


def _combine(y_rows, wts, h, sg, su, sd, x, mod, which, g_final, dims, n_rows, final_norm):
    d = x.shape[1]
    de = sg.shape[1]
    tile = h.shape[1:]
    tm = COMBINE_ROWS
    per = ROW_BLOCK // tm
    nsteps = n_rows // tm
    nlb, bpb, nb = dims
    mrow = lambda i: _mod_row(i // per, nlb, bpb, nb)
    yspec = lambda k: pl.BlockSpec((tm,) + tile, lambda i: (k * nsteps + i, 0, 0))
    w3 = jnp.broadcast_to(wts.T[:, :, None], (n_rows, TOP_K, LANE))
    return pl.pallas_call(
        functools.partial(_combine_kernel, final_norm=final_norm),
        grid=(nsteps,),
        in_specs=[yspec(k) for k in range(TOP_K)] + [
                  pl.BlockSpec((tm, TOP_K, LANE), lambda i: (i, 0, 0)),
                  pl.BlockSpec((tm,) + tile, lambda i: (i, 0, 0)),
                  pl.BlockSpec((d, de), lambda i: (0, 0)),
                  pl.BlockSpec((d, de), lambda i: (0, 0)),
                  pl.BlockSpec((de, d), lambda i: (0, 0)),
                  pl.BlockSpec((tm, d), lambda i: (i, 0)),
                  pl.BlockSpec((None, 1, d), lambda i: (mrow(i) * 6 + which, 0, 0)),
                  pl.BlockSpec((1, d), lambda i: (0, 0))],
        out_specs=pl.BlockSpec((tm, d), lambda i: (i, 0)),
        out_shape=jax.ShapeDtypeStruct((n_rows, d), f32),
        compiler_params=_params(("parallel",)),
    )(*([y_rows] * TOP_K), w3, h, sg, su, sd, x, mod, g_final.reshape(1, d))


def _moe(x, g, mod, which_shift, w_router, b_router, wg, wu, wd, layer, sg, su, sd, g_final, dims, n_rows,
         final_norm):
    ne = N_EXPERTS
    mb = MOE_BLOCK
    h, idx, wts, rank, cnt = _router(x, g, mod, which_shift, w_router, b_router, dims, n_rows)
    counts = cnt[:, 0].astype(i32)
    padded = (counts + mb - 1) // mb * mb
    pend = jnp.cumsum(padded)
    start = pend - padded
    onehot = idx[:, :, None] == jnp.arange(ne, dtype=i32)[None, None, :]
    dest = jnp.sum(jnp.where(onehot, start[None, None, :], 0), axis=-1) + rank
    n_assign = n_rows * TOP_K
    nblk = (n_assign + ne * (mb - 1) + mb - 1) // mb
    pos = jnp.arange(TOP_K * n_rows, dtype=i32)
    spare = n_assign + jnp.arange(nblk * mb, dtype=i32) % mb
    slot_pos = spare.at[dest.reshape(-1)].set(pos, unique_indices=True)
    slot_tok = jnp.where(slot_pos < n_assign, slot_pos % n_rows, 0)
    blk_start = jnp.arange(nblk, dtype=i32) * mb
    block_e = jnp.minimum(jnp.sum((pend[None, :] <= blk_start[:, None]).astype(i32), axis=1), ne - 1)
    n_used = (pend[-1:] // mb).astype(i32)
    y_rows = _experts(h, slot_tok.reshape(nblk, mb), slot_pos.reshape(nblk, mb), block_e, n_used, wg, wu, wd,
                      layer, n_rows)
    return _combine(y_rows, wts, h, sg, su, sd, x, mod, which_shift + 2, g_final, dims, n_rows, final_norm)


def _rope_tables(seq, head_dim):
    half = head_dim // 2
    quarter = half // 2
    t = jnp.arange(seq, dtype=i32)
    rows = (t // GRID_W).astype(f32)
    cols = (t % GRID_W).astype(f32)
    freqs = ROPE_THETA ** (-jnp.arange(0, half, 2, dtype=f32) / half)
    ar = rows[:, None] * freqs[None, :]
    ac = cols[:, None] * freqs[None, :]
    cos = jnp.concatenate([jnp.cos(ar), jnp.cos(ar), jnp.cos(ac), jnp.cos(ac)], axis=1)
    sin = jnp.concatenate([-jnp.sin(ar), jnp.sin(ar), -jnp.sin(ac), jnp.sin(ac)], axis=1)
    reps = LANE // head_dim
    cos = jnp.tile(cos, (1, reps))
    sin = jnp.tile(sin, (1, reps))
    cos = jnp.concatenate([cos, jnp.ones((ROW_BLOCK, LANE), f32)], axis=0)
    sin = jnp.concatenate([sin, jnp.zeros((ROW_BLOCK, LANE), f32)], axis=0)
    return cos, sin, quarter


def kernel(x, c, ctx, c_ctx, w_ada, b_ada, g_norm_mix, g_norm_ffn, w_in_even, w_out_even, hgrn_lb_logits,
           g_hgrn_norm, win_sink, w_in_odd, w_out_odd, g_q_norm, g_k_norm, w_router, b_router,
           w_exp_gate, w_exp_up, w_exp_down, w_sh_gate, w_sh_up, w_sh_down, g_norm_final):
    nb, seq, d = x.shape
    lc = ctx.shape[1]
    assert seq % ROW_BLOCK == 0 and lc % ROW_BLOCK == 0 and seq % lc == 0
    t_lat = nb * seq
    t_all = t_lat + nb * lc
    bpb = seq // ROW_BLOCK
    nlb = t_lat // ROW_BLOCK
    dims = (nlb, bpb, nb)

    xs = jnp.concatenate([x.reshape(t_lat, d), ctx.reshape(nb * lc, d)], axis=0)
    cc = jnp.concatenate([c, c_ctx[None, :], jnp.zeros((8 - nb - 1, d), f32)], axis=0)

    lb_w = jax.nn.softmax(hgrn_lb_logits.astype(f32), axis=0)
    lower_bounds = jnp.cumsum(lb_w, axis=0)[1:] - lb_w[0]

    mod = _ada(cc, w_ada, b_ada, 0).reshape(8 * 6, 1, d)
    w_in = w_in_even[0].astype(bf16)
    proj = _normmm(xs, g_norm_mix[0], mod, 0, w_in[:, :HGRN_COLS], f32, dims, tn=1280)
    cos, sin, quarter = _rope_tables(seq, WIN_HEAD_DIM)
    epi = dict(blocks={0: (WIN_HEADS + WIN_KV_HEADS) * WIN_HEAD_DIM // LANE}, quarter=quarter, head_norm=False,
               q_slices=WIN_HEADS * WIN_HEAD_DIM // LANE, q_scale=WIN_HEAD_DIM ** -0.5)
    wproj = _normmm(xs, g_norm_mix[0], mod, 0, w_in[:, HGRN_COLS:], bf16, dims, tn=WIN_COLS,
                    epilogue=epi, tables=(cos, sin))
    o_f, o_b = _hgrn(proj, lower_bounds[0], nb, seq, lc)
    a_mix = _hgrn_out(o_f, o_b, proj, g_hgrn_norm[0])
    b_mix = _win_attn(wproj, win_sink[0], nb, seq, lc)
    xs = _outproj(a_mix, 0, b_mix, 0, w_out_even[0].astype(bf16), xs, mod, 2, dims, t_all)
    xs = _moe(xs, g_norm_ffn[0], mod, 3, w_router[0], b_router[0],
              w_exp_gate, w_exp_up, w_exp_down, 0,
              w_sh_gate[0].astype(bf16), w_sh_up[0].astype(bf16), w_sh_down[0].astype(bf16),
              g_norm_final, dims, t_all, False)

    mod = _ada(cc, w_ada, b_ada, 1).reshape(8 * 6, 1, d)
    cos, sin, quarter = _rope_tables(seq, GLB_HEAD_DIM)
    tn = 1536
    n_sl = tn // LANE
    rope_slices = GLB_HEADS + GLB_KV_HEADS
    epi = dict(blocks={0: min(rope_slices, n_sl), 1: max(rope_slices - n_sl, 0)}, quarter=quarter, head_norm=True,
               q_slices=GLB_HEADS, q_scale=GLB_HEAD_DIM ** -0.5)
    qkv = _normmm(xs, g_norm_mix[1], mod, 0, w_in_odd[0].astype(bf16), bf16, dims, tn=tn,
                  epilogue=epi, tables=(cos, sin), gains=(g_q_norm[0], g_k_norm[0]))
    att = _glb_attn(qkv, nb, seq, lc)
    xl = _outproj(att, 0, att, 1, w_out_odd[0].astype(bf16), xs, mod, 2, dims, t_lat)
    out = _moe(xl, g_norm_ffn[1], mod, 3, w_router[1], b_router[1],
               w_exp_gate, w_exp_up, w_exp_down, 1,
               w_sh_gate[1].astype(bf16), w_sh_up[1].astype(bf16), w_sh_down[1].astype(bf16),
               g_norm_final, dims, t_lat, True)
    return out.reshape(nb, seq, d)
```

```python
import functools

import jax
import jax.numpy as jnp
from jax import lax
from jax.experimental import pallas as pl
from jax.experimental.pallas import tpu as pltpu

f32 = jnp.float32
bf16 = jnp.bfloat16
i32 = jnp.int32
u32 = jnp.uint32

EPS = 1e-6
LOG2E = 1.4426950408889634
ROPE_THETA = 10000.0
GRID_W = 64

HGRN_DK = 128
HGRN_HEADS = 8
HGRN_CHUNK = 64
HGRN_SUB = 16
HGRN_HEADS_PER_STEP = 8
WIN_HEAD_DIM = 64
WIN_HEADS = 16
WIN_KV_HEADS = 2
WIN_BLOCK = 128
GLB_HEAD_DIM = 128
GLB_HEADS = 16
GLB_KV_HEADS = 4
Q_BLOCK = 256
N_EXPERTS = 64
N_GROUPS = 8
TOPK_GROUPS = 4
TOP_K = 8
ROUTED_SCALE = 2.5
MOE_BLOCK = 256

LANE = 128
ROW_BLOCK = 256
VMEM_LIMIT = 56 * 1024 * 1024

A_K = HGRN_HEADS * HGRN_DK
HGRN_COLS = 5 * A_K
WIN_COLS = WIN_HEADS * WIN_HEAD_DIM + 2 * WIN_KV_HEADS * WIN_HEAD_DIM


def _params(sem, vmem=VMEM_LIMIT):
    return pltpu.CompilerParams(dimension_semantics=sem, vmem_limit_bytes=vmem)


def _silu(x):
    return x * jax.nn.sigmoid(x)


def _dot(a, b):
    return jnp.dot(a, b, preferred_element_type=f32)


def _dot_nt(a, b):
    return lax.dot_general(a, b, (((1,), (1,)), ((), ())), preferred_element_type=f32)


def _pack_halves(x):
    n = x.shape[1] // 2
    lo = lax.bitcast_convert_type(x[:, :n].astype(bf16).astype(f32), u32)
    hi = lax.bitcast_convert_type(x[:, n:].astype(bf16).astype(f32), u32)
    return (lo >> 16) | hi


def _unpack_halves(w):
    lo = lax.bitcast_convert_type(w << 16, f32)
    hi = lax.bitcast_convert_type(w & jnp.uint32(0xFFFF0000), f32)
    return lo, hi


def _rows_to_tiles(w):
    n = w.shape[1] // LANE
    return pltpu.einshape("stl->tsl", jnp.stack([w[:, s * LANE:(s + 1) * LANE] for s in range(n)], axis=0))


def _tiles_to_rows(t):
    y = pltpu.einshape("tsl->stl", t)
    return jnp.concatenate([y[s] for s in range(t.shape[1])], axis=1)


def _dot_tn(a, b):
    return lax.dot_general(a, b, (((0,), (0,)), ((), ())), preferred_element_type=f32)


def _ada_kernel(c_ref, w_ref, b_ref, o_ref):
    a = _silu(c_ref[...]).astype(bf16)
    o_ref[...] = _dot(a, w_ref[...].astype(bf16)) + b_ref[...]


def _ada(c8, w, b, layer, tn=768):
    m, d = c8.shape
    n = w.shape[2]
    return pl.pallas_call(
        _ada_kernel,
        grid=(n // tn,),
        in_specs=[pl.BlockSpec((m, d), lambda j: (0, 0)),
                  pl.BlockSpec((None, d, tn), lambda j: (layer, 0, j)),
                  pl.BlockSpec((None, 1, tn), lambda j: (layer, 0, j))],
        out_specs=pl.BlockSpec((m, tn), lambda j: (0, j)),
        out_shape=jax.ShapeDtypeStruct((m, n), f32),
        compiler_params=_params(("arbitrary",)),
    )(c8, w, b.reshape(b.shape[0], 1, n))


def _norm_mod(x, g, sc, sh):
    y = x * lax.rsqrt(jnp.mean(x * x, axis=-1, keepdims=True) + EPS) * g
    return y * (1.0 + sc) + sh


def _rope_slice(x, cos, sin, quarter):
    lane = lax.broadcasted_iota(i32, x.shape, 1)
    up = pltpu.roll(x, LANE - quarter, axis=1)
    dn = pltpu.roll(x, quarter, axis=1)
    partner = jnp.where(lane % (2 * quarter) < quarter, up, dn)
    return x * cos + partner * sin


def _normmm_kernel(x_ref, g_ref, sc_ref, sh_ref, w_ref, *rest, epilogue, tn):
    if epilogue is None:
        (o_ref,) = rest
    elif epilogue["head_norm"]:
        cos_ref, sin_ref, gq_ref, gk_ref, o_ref = rest
    else:
        cos_ref, sin_ref, o_ref = rest
    j = pl.program_id(0)
    h = _norm_mod(x_ref[...], g_ref[...], sc_ref[...], sh_ref[...]).astype(bf16)
    y = _dot(h, w_ref[...])
    if epilogue is None:
        o_ref[...] = y.astype(o_ref.dtype)
        return

    n_sl = tn // LANE
    for jb in range(epilogue["n_col_blocks"]):
        n_rope = epilogue["blocks"].get(jb, 0)

        @pl.when(j == jb)
        def _(jb=jb, n_rope=n_rope):
            cos = cos_ref[...]
            sin = sin_ref[...]
            outs = []
            for s in range(n_sl):
                ys = y[:, s * LANE:(s + 1) * LANE]
                if s < n_rope:
                    col = jb * n_sl + s
                    is_q = col < epilogue["q_slices"]
                    if epilogue["head_norm"]:
                        gain = gq_ref[...] if is_q else gk_ref[...]
                        ys = ys * lax.rsqrt(jnp.mean(ys * ys, axis=-1, keepdims=True) + EPS) * gain
                    ys = _rope_slice(ys, cos, sin, epilogue["quarter"])
                    if is_q:
                        ys = ys * epilogue["q_scale"]
                outs.append(ys.astype(o_ref.dtype))
            o_ref[...] = jnp.concatenate(outs, axis=1)


def _mod_row(i, n_lat_blocks, blocks_per_batch, n_batch):
    return jnp.where(i < n_lat_blocks, i // blocks_per_batch, n_batch)


def _normmm(x, g, mod, which, w, out_dtype, dims, tn, epilogue=None, tables=None, gains=None):
    t, d = x.shape
    n = w.shape[1]
    tm = ROW_BLOCK
    nlb, bpb, nb = dims
    mrow = lambda i: _mod_row(i, nlb, bpb, nb)
    in_specs = [pl.BlockSpec((tm, d), lambda j, i: (i, 0)),
                pl.BlockSpec((1, d), lambda j, i: (0, 0)),
                pl.BlockSpec((None, 1, d), lambda j, i: (mrow(i) * 6 + which + 1, 0, 0)),
                pl.BlockSpec((None, 1, d), lambda j, i: (mrow(i) * 6 + which, 0, 0)),
                pl.BlockSpec((d, tn), lambda j, i: (0, j))]
    args = [x, g.reshape(1, d), mod, mod, w]
    if epilogue is not None:
        cos, sin = tables
        tab = lambda j, i: (jnp.where(i < nlb, i % bpb, bpb), 0)
        in_specs += [pl.BlockSpec((tm, LANE), tab), pl.BlockSpec((tm, LANE), tab)]
        args += [cos, sin]
        epilogue = dict(epilogue, n_col_blocks=n // tn)
        if epilogue["head_norm"]:
            in_specs += [pl.BlockSpec((1, LANE), lambda j, i: (0, 0))] * 2
            args += [gains[0].reshape(1, LANE), gains[1].reshape(1, LANE)]
    return pl.pallas_call(
        functools.partial(_normmm_kernel, epilogue=epilogue, tn=tn),
        grid=(n // tn, t // tm),
        in_specs=in_specs,
        out_specs=pl.BlockSpec((tm, tn), lambda j, i: (i, j)),
        out_shape=jax.ShapeDtypeStruct((t, n), out_dtype),
        compiler_params=_params(("parallel", "parallel")),
    )(*args)


def _gla_chunk(q_raw, v, f_raw, lb, st_ref, a_scr, rev):
    c = HGRN_CHUNK
    sub = HGRN_SUB
    q = _silu(q_raw)
    f = lb + (1.0 - lb) * jax.nn.sigmoid(f_raw)
    kk = 1.0 - f
    g = jnp.log(f)
    row = lax.broadcasted_iota(i32, (c, c), 0)
    col = lax.broadcasted_iota(i32, (c, c), 1)
    tri = (col >= row) if rev else (col <= row)
    b = jnp.dot(tri.astype(f32), g, preferred_element_type=f32, precision=lax.Precision.HIGHEST)
    b_end = b[0:1] if rev else b[c - 1:c]
    st = st_ref[...]
    o = _dot_nt((q * jnp.exp(b)).astype(bf16), st.astype(bf16))

    vb = v.astype(bf16)
    nsub = c // sub
    lane16 = lax.broadcasted_iota(i32, (sub, sub), 1)
    row16 = lax.broadcasted_iota(i32, (sub, sub), 0)
    diag_ok = (row16 <= lane16) if rev else (row16 >= lane16)
    rsub = lax.broadcasted_iota(i32, (c, sub), 0) // sub
    for jb in range(nsub):
        js = slice(jb * sub, (jb + 1) * sub)
        m_j = b[jb * sub:jb * sub + 1] if rev else b[(jb + 1) * sub - 1:(jb + 1) * sub]
        kd = kk[js] * jnp.exp(m_j - b[js])
        qd = q * jnp.exp(jnp.minimum(b - m_j, 0.0))
        a_col = _dot_nt(qd.astype(bf16), kd.astype(bf16))
        off_ok = (rsub < jb) if rev else (rsub > jb)
        a_col = jnp.where(off_ok, a_col, 0.0)
        qi, bi, ki = q[js], b[js] * LOG2E, kk[js]
        a_dd = jnp.zeros((sub, sub), f32)
        for s in range(sub):
            e = jnp.exp2(bi - bi[s:s + 1])
            col_s = jnp.sum(qi * ki[s:s + 1] * e, axis=-1, keepdims=True)
            a_dd = jnp.where(lane16 == s, col_s, a_dd)
        a_dd = jnp.where(diag_ok, a_dd, 0.0)
        pieces = [a_dd if ib == jb else a_col[ib * sub:(ib + 1) * sub] for ib in range(nsub)]
        a_scr[:, js] = jnp.concatenate(pieces, axis=0)
    o = o + _dot(a_scr[...].astype(bf16), vb)
    kdec = kk * jnp.exp(b_end - b)
    st_ref[...] = jnp.exp(b_end) * st + _dot_tn(vb, kdec.astype(bf16))
    return o


def _hgrn_kernel(qf_ref, vf_ref, ff_ref, qb_ref, vb_ref, fb_ref, lb_ref, of_ref, ob_ref,
                 stf_ref, stb_ref, af_scr, ab_scr):
    @pl.when(pl.program_id(2) == 0)
    def _():
        stf_ref[...] = jnp.zeros_like(stf_ref)
        stb_ref[...] = jnp.zeros_like(stb_ref)

    for j in range(HGRN_HEADS_PER_STEP):
        sl = slice(j * HGRN_DK, (j + 1) * HGRN_DK)
        lb = lb_ref[:, sl]
        of_ref[:, sl] = _gla_chunk(qf_ref[:, sl], vf_ref[:, sl], ff_ref[:, sl], lb, stf_ref.at[j], af_scr.at[j], False)
        ob_ref[:, sl] = _gla_chunk(qb_ref[:, sl], vb_ref[:, sl], fb_ref[:, sl], lb, stb_ref.at[j], ab_scr.at[j], True)


def _hgrn(proj, lb, n_batch, seq, ctx_len):
    t = proj.shape[0]
    c = HGRN_CHUNK
    hp = HGRN_HEADS_PER_STEP
    wblk = hp * HGRN_DK
    ngrp = HGRN_HEADS // hp
    ncc, ncl = ctx_len // c, seq // c
    lat_chunks = n_batch * ncl

    def fwd_blk(b, s):
        return jnp.where(s < ncc, lat_chunks + b * ncc + s, b * ncl + (s - ncc))

    def bwd_blk(b, s):
        return jnp.where(s < ncc, lat_chunks + b * ncc + (ncc - 1 - s), b * ncl + (ncl - 1 - (s - ncc)))

    def spec(blk, section):
        return pl.BlockSpec((c, wblk), lambda b, h, s: (blk(b, s), section * ngrp + h))

    def ospec(blk):
        return pl.BlockSpec((c, wblk), lambda b, h, s: (blk(b, s), h))

    out = jax.ShapeDtypeStruct((t, A_K), f32)
    return pl.pallas_call(
        _hgrn_kernel,
        grid=(n_batch, ngrp, ncc + ncl),
        in_specs=[spec(fwd_blk, 0), spec(fwd_blk, 1), spec(fwd_blk, 2),
                  spec(bwd_blk, 0), spec(bwd_blk, 1), spec(bwd_blk, 3),
                  pl.BlockSpec((1, wblk), lambda b, h, s: (0, h))],
        out_specs=[ospec(fwd_blk), ospec(bwd_blk)],
        out_shape=[out, out],
        scratch_shapes=[pltpu.VMEM((hp, HGRN_DK, HGRN_DK), f32), pltpu.VMEM((hp, HGRN_DK, HGRN_DK), f32),
                        pltpu.VMEM((hp, c, c), f32), pltpu.VMEM((hp, c, c), f32)],
        compiler_params=_params(("parallel", "parallel", "arbitrary")),
    )(proj, proj, proj, proj, proj, proj, lb.reshape(1, A_K))


def _hgrn_out_kernel(of_ref, ob_ref, g_ref, gain_ref, o_ref):
    gain = gain_ref[...]
    outs = []
    for h in range(HGRN_HEADS):
        sl = slice(h * HGRN_DK, (h + 1) * HGRN_DK)
        o = of_ref[:, sl] + ob_ref[:, sl]
        o = o * lax.rsqrt(jnp.mean(o * o, axis=-1, keepdims=True) + EPS) * gain
        outs.append((o * _silu(g_ref[:, sl])).astype(bf16))
    o_ref[...] = jnp.concatenate(outs, axis=1)


def _hgrn_out(o_f, o_b, proj, gain):
    t = o_f.shape[0]
    tm = ROW_BLOCK
    return pl.pallas_call(
        _hgrn_out_kernel,
        grid=(t // tm,),
        in_specs=[pl.BlockSpec((tm, A_K), lambda i: (i, 0)),
                  pl.BlockSpec((tm, A_K), lambda i: (i, 0)),
                  pl.BlockSpec((tm, A_K), lambda i: (i, 4)),
                  pl.BlockSpec((1, HGRN_DK), lambda i: (0, 0))],
        out_specs=pl.BlockSpec((tm, A_K), lambda i: (i, 0)),
        out_shape=jax.ShapeDtypeStruct((t, A_K), bf16),
        compiler_params=_params(("parallel",)),
    )(o_f, o_b, proj, gain.reshape(1, HGRN_DK))


def _win_kernel(sink_ref, q_ref, kp_ref, kc_ref, kn_ref, vp_ref, vc_ref, vn_ref, kx_ref, vx_ref, o_ref,
                *, n_lat_blocks):
    n = pl.program_id(1)
    w = WIN_BLOCK
    dh = WIN_HEAD_DIM
    is_lat = n < n_lat_blocks
    ri = lax.broadcasted_iota(i32, (w, w), 0)
    ci = lax.broadcasted_iota(i32, (w, w), 1)
    ok_p = (ci >= ri) & is_lat & (n > 0)
    ok_c = jnp.broadcast_to(is_lat, (w, w))
    ok_n = (ci <= ri) & is_lat & (n < n_lat_blocks - 1)
    valid = jnp.concatenate([ok_p, ok_c, ok_n], axis=1)
    kwin = jnp.concatenate([kp_ref[...], kc_ref[...], kn_ref[...]], axis=0)
    vwin = jnp.concatenate([vp_ref[...], vc_ref[...], vn_ref[...]], axis=0)
    kx = kx_ref[...]
    vx = vx_ref[...]
    g = WIN_HEADS // WIN_KV_HEADS
    outs = []
    for h in range(WIN_HEADS):
        kv = h // g
        ks = slice(kv * dh, (kv + 1) * dh)
        qh = q_ref[:, h * dh:(h + 1) * dh]
        sw = jnp.where(valid, _dot_nt(qh, kwin[:, ks]), -jnp.inf)
        sx = _dot_nt(qh, kx[:, ks])
        sk = sink_ref[h]
        m = jnp.maximum(jnp.maximum(jnp.max(sw, axis=-1, keepdims=True), jnp.max(sx, axis=-1, keepdims=True)), sk)
        pw = jnp.exp(sw - m)
        px = jnp.exp(sx - m)
        den = jnp.sum(pw, axis=-1, keepdims=True) + jnp.sum(px, axis=-1, keepdims=True) + jnp.exp(sk - m)
        o = _dot(pw.astype(bf16), vwin[:, ks]) + _dot(px.astype(bf16), vx[:, ks])
        outs.append((o / den).astype(bf16))
    o_ref[...] = jnp.concatenate(outs, axis=1)


def _win_attn(wproj, sink, n_batch, seq, ctx_len):
    t = wproj.shape[0]
    w = WIN_BLOCK
    nlb = seq // w
    ncb = ctx_len // w
    lat_blocks = n_batch * nlb
    qc = WIN_HEADS * WIN_HEAD_DIM // LANE
    kcol, vcol = qc, qc + 1

    def qrow(b, n):
        return jnp.where(n < nlb, b * nlb + n, lat_blocks + b * ncb + (n - nlb))

    def krow(off):
        def f(b, n):
            return b * nlb + jnp.clip(n + off, 0, nlb - 1)
        return f

    def kspec(off, colblk):
        return pl.BlockSpec((w, LANE), lambda b, n: (krow(off)(b, n), colblk))

    def xspec(colblk):
        return pl.BlockSpec((ctx_len, LANE), lambda b, n: ((n_batch * seq) // ctx_len + b, colblk))

    return pl.pallas_call(
        functools.partial(_win_kernel, n_lat_blocks=nlb),
        grid=(n_batch, nlb + ncb),
        in_specs=[pl.BlockSpec(memory_space=pltpu.SMEM),
                  pl.BlockSpec((w, qc * LANE), lambda b, n: (qrow(b, n), 0)),
                  kspec(-1, kcol), kspec(0, kcol), kspec(1, kcol),
                  kspec(-1, vcol), kspec(0, vcol), kspec(1, vcol),
                  xspec(kcol), xspec(vcol)],
        out_specs=pl.BlockSpec((w, qc * LANE), lambda b, n: (qrow(b, n), 0)),
        out_shape=jax.ShapeDtypeStruct((t, qc * LANE), bf16),
        compiler_params=_params(("parallel", "arbitrary")),
    )(sink.astype(f32), wproj, wproj, wproj, wproj, wproj, wproj, wproj, wproj, wproj)


GLB_KEY_CHUNK = 256


def _glb_kernel(q_ref, kx_ref, kl_ref, vx_ref, vl_ref, o_ref, m_scr, l_scr, acc_scr):
    dh = GLB_HEAD_DIM
    g = GLB_HEADS // GLB_KV_HEADS
    ck = GLB_KEY_CHUNK
    tq = q_ref.shape[0]
    q = jnp.concatenate([q_ref[:, i * dh:(i + 1) * dh] for i in range(g)], axis=0)

    def update(k, v, first):
        s = _dot_nt(q, k)
        reps = s.shape[1] // LANE
        mx = jnp.broadcast_to(jnp.max(s, axis=-1, keepdims=True), (g * tq, LANE))
        if first:
            m_new = mx
            p = jnp.exp(s - jnp.concatenate([m_new] * reps, axis=1))
            l_scr[...] = jnp.broadcast_to(jnp.sum(p, axis=-1, keepdims=True), (g * tq, LANE))
            acc_scr[...] = _dot(p.astype(bf16), v)
        else:
            m_old = m_scr[...]
            m_new = jnp.maximum(m_old, mx)
            alpha = jnp.exp(m_old - m_new)
            p = jnp.exp(s - jnp.concatenate([m_new] * reps, axis=1))
            l_scr[...] = alpha * l_scr[...] + jnp.broadcast_to(jnp.sum(p, axis=-1, keepdims=True), (g * tq, LANE))
            acc_scr[...] = alpha * acc_scr[...] + _dot(p.astype(bf16), v)
        m_scr[...] = m_new

    chunks = [(kx_ref, vx_ref, c, min(ck, kx_ref.shape[0] - c)) for c in range(0, kx_ref.shape[0], ck)]
    chunks += [(kl_ref, vl_ref, c, min(ck, kl_ref.shape[0] - c)) for c in range(0, kl_ref.shape[0], ck)]
    for n, (kr, vr, c0, sz) in enumerate(chunks):
        update(kr[c0:c0 + sz, :], vr[c0:c0 + sz, :], n == 0)
    o = acc_scr[...] / l_scr[...]
    o_ref[...] = jnp.concatenate([o[i * tq:(i + 1) * tq] for i in range(g)], axis=1).astype(bf16)


def _glb_attn(qkv, n_batch, seq, ctx_len):
    tq = Q_BLOCK
    dh = GLB_HEAD_DIM
    g = GLB_HEADS // GLB_KV_HEADS
    nq = seq // tq
    kcol0 = GLB_HEADS
    vcol0 = GLB_HEADS + GLB_KV_HEADS
    ctx_blk0 = (n_batch * seq) // ctx_len
    return pl.pallas_call(
        _glb_kernel,
        grid=(n_batch, GLB_KV_HEADS, nq),
        in_specs=[pl.BlockSpec((tq, g * dh), lambda b, k, i: (b * nq + i, k)),
                  pl.BlockSpec((ctx_len, dh), lambda b, k, i: (ctx_blk0 + b, kcol0 + k)),
                  pl.BlockSpec((seq, dh), lambda b, k, i: (b, kcol0 + k)),
                  pl.BlockSpec((ctx_len, dh), lambda b, k, i: (ctx_blk0 + b, vcol0 + k)),
                  pl.BlockSpec((seq, dh), lambda b, k, i: (b, vcol0 + k))],
        out_specs=pl.BlockSpec((tq, g * dh), lambda b, k, i: (b * nq + i, k)),
        out_shape=jax.ShapeDtypeStruct((n_batch * seq, GLB_HEADS * dh), bf16),
        scratch_shapes=[pltpu.VMEM((g * tq, LANE), f32), pltpu.VMEM((g * tq, LANE), f32), pltpu.VMEM((g * tq, dh), f32)],
        compiler_params=_params(("parallel", "parallel", "arbitrary")),
    )(qkv, qkv, qkv, qkv, qkv)


def _outproj_kernel(a1_ref, a2_ref, w1_ref, w2_ref, x_ref, gate_ref, o_ref):
    y = _dot(a1_ref[...], w1_ref[...]) + _dot(a2_ref[...], w2_ref[...])
    o_ref[...] = x_ref[...] + gate_ref[...] * y


def _outproj(a1, c1, a2, c2, w, x, mod, which, dims, n_rows, tn=1024):
    d = x.shape[1]
    kh = w.shape[0] // 2
    tm = ROW_BLOCK
    nlb, bpb, nb = dims
    mrow = lambda i: _mod_row(i, nlb, bpb, nb)
    return pl.pallas_call(
        _outproj_kernel,
        grid=(d // tn, n_rows // tm),
        in_specs=[pl.BlockSpec((tm, kh), lambda j, i: (i, c1)),
                  pl.BlockSpec((tm, kh), lambda j, i: (i, c2)),
                  pl.BlockSpec((kh, tn), lambda j, i: (0, j)),
                  pl.BlockSpec((kh, tn), lambda j, i: (1, j)),
                  pl.BlockSpec((tm, tn), lambda j, i: (i, j)),
                  pl.BlockSpec((None, 1, tn), lambda j, i: (mrow(i) * 6 + which, 0, j))],
        out_specs=pl.BlockSpec((tm, tn), lambda j, i: (i, j)),
        out_shape=jax.ShapeDtypeStruct((n_rows, d), f32),
        compiler_params=_params(("parallel", "parallel")),
    )(a1, a2, w, w, x, mod)


def _router_kernel(x_ref, g_ref, sc_ref, sh_ref, wr_ref, br_ref, u_ref,
                   h_ref, idx_ref, wt_ref, rank_ref, cnt_ref, carry_scr):
    i = pl.program_id(0)
    tm = x_ref.shape[0]
    ne, ng, pg = N_EXPERTS, N_GROUPS, N_EXPERTS // N_GROUPS

    @pl.when(i == 0)
    def _():
        carry_scr[...] = jnp.zeros_like(carry_scr)

    h = _norm_mod(x_ref[...], g_ref[...], sc_ref[...], sh_ref[...])
    h_ref[...] = _rows_to_tiles(_pack_halves(h))
    logits = lax.dot_general(wr_ref[...], h, (((1,), (1,)), ((), ())), preferred_element_type=f32,
                             precision=lax.Precision.HIGHEST)
    s = jax.nn.sigmoid(logits)
    s3 = s.reshape(ng, pg, tm)
    b3 = (s + br_ref[...]).reshape(ng, pg, tm)
    gi = lax.broadcasted_iota(i32, (ng, pg, tm), 0)
    pi = lax.broadcasted_iota(i32, (ng, pg, tm), 1)
    neg = -jnp.inf
    m1 = jnp.max(b3, axis=1, keepdims=True)
    first = jnp.min(jnp.where(b3 == m1, pi, pg), axis=1, keepdims=True)
    m2 = jnp.max(jnp.where(pi == first, neg, b3), axis=1, keepdims=True)
    gs = m1 + m2
    g2 = lax.broadcasted_iota(i32, (ng, 1, tm), 0)
    gmask = jnp.zeros((ng, 1, tm), jnp.bool_)
    for _ in range(TOPK_GROUPS):
        m = jnp.max(gs, axis=0, keepdims=True)
        fi = jnp.min(jnp.where(gs == m, g2, ng), axis=0, keepdims=True)
        pick = g2 == fi
        gmask = gmask | pick
        gs = jnp.where(pick, neg, gs)
    cand = jnp.where(gmask, b3, neg)
    eid = gi * pg + pi
    sel = jnp.zeros((ng, pg, tm), jnp.bool_)
    picks, idxs, wts = [], [], []
    for _ in range(TOP_K):
        m = jnp.max(jnp.max(cand, axis=0, keepdims=True), axis=1, keepdims=True)
        fi = jnp.min(jnp.min(jnp.where(cand == m, eid, ne), axis=0, keepdims=True), axis=1, keepdims=True)
        pick = eid == fi
        picks.append(pick)
        idxs.append(fi.reshape(1, tm))
        wts.append(jnp.sum(jnp.sum(jnp.where(pick, s3, 0.0), axis=0, keepdims=True), axis=1,
                           keepdims=True).reshape(1, tm))
        sel = sel | pick
        cand = jnp.where(pick, neg, cand)
    wsum = wts[0]
    for k in range(1, TOP_K):
        wsum = wsum + wts[k]
    idx_ref[...] = jnp.concatenate(idxs, axis=0)
    wt_ref[...] = jnp.concatenate([wk / wsum * ROUTED_SCALE for wk in wts], axis=0)
    sel2 = jnp.where(sel, 1.0, 0.0).reshape(ne, tm)
    prefix = (_dot(sel2.astype(bf16), u_ref[...]) + carry_scr[...]).reshape(ng, pg, tm)
    ranks = [jnp.sum(jnp.sum(jnp.where(pk, prefix, 0.0), axis=0, keepdims=True), axis=1,
                     keepdims=True).reshape(1, tm) for pk in picks]
    rank_ref[...] = jnp.concatenate(ranks, axis=0).astype(i32)
    total = carry_scr[...] + jnp.sum(sel2, axis=1, keepdims=True)
    carry_scr[...] = total
    cnt_ref[...] = total


def _router(x, g, mod, which, w_router, b_router, dims, n_rows):
    d = x.shape[1]
    tm = ROW_BLOCK
    ne = N_EXPERTS
    nlb, bpb, nb = dims
    mrow = lambda i: _mod_row(i, nlb, bpb, nb)
    tri = (jnp.arange(tm)[:, None] < jnp.arange(tm)[None, :]).astype(bf16)
    return pl.pallas_call(
        _router_kernel,
        grid=(n_rows // tm,),
        in_specs=[pl.BlockSpec((tm, d), lambda i: (i, 0)),
                  pl.BlockSpec((1, d), lambda i: (0, 0)),
                  pl.BlockSpec((None, 1, d), lambda i: (mrow(i) * 6 + which + 1, 0, 0)),
                  pl.BlockSpec((None, 1, d), lambda i: (mrow(i) * 6 + which, 0, 0)),
                  pl.BlockSpec((ne, d), lambda i: (0, 0)),
                  pl.BlockSpec((ne, 1), lambda i: (0, 0)),
                  pl.BlockSpec((tm, tm), lambda i: (0, 0))],
        out_specs=[pl.BlockSpec((tm, d // 2 // LANE, LANE), lambda i: (i, 0, 0)),
                   pl.BlockSpec((TOP_K, tm), lambda i: (0, i)),
                   pl.BlockSpec((TOP_K, tm), lambda i: (0, i)),
                   pl.BlockSpec((TOP_K, tm), lambda i: (0, i)),
                   pl.BlockSpec((ne, 1), lambda i: (0, 0))],
        out_shape=[jax.ShapeDtypeStruct((n_rows, d // 2 // LANE, LANE), u32),
                   jax.ShapeDtypeStruct((TOP_K, n_rows), i32),
                   jax.ShapeDtypeStruct((TOP_K, n_rows), f32),
                   jax.ShapeDtypeStruct((TOP_K, n_rows), i32),
                   jax.ShapeDtypeStruct((ne, 1), f32)],
        scratch_shapes=[pltpu.VMEM((ne, 1), f32)],
        compiler_params=_params(("arbitrary",)),
    )(x, g.reshape(1, d), mod, mod, w_router.T, b_router.reshape(ne, 1).astype(f32), tri)


EXPERT_BURSTS = 8


def _expert_kernel(be_ref, nu_ref, tok0_ref, tokn_ref, posp_ref, h_hbm, wg_ref, wu_ref, wd_ref, y_hbm,
                   xbuf, ybuf, yrow, wgc, wuc, wdc, sem_g, sem_s):
    i = pl.program_id(0)
    nblk = pl.num_programs(0) - 1
    n_used = nu_ref[0]
    slot = i % 2
    mb = MOE_BLOCK
    de = wgc.shape[1]
    dp = yrow.shape[1]
    spare0 = y_hbm.shape[0] - 2 * mb
    per = mb // (EXPERT_BURSTS // 2)

    def gather_row(idx_ref, r, dst_slot, priority=0):
        pltpu.make_async_copy(h_hbm.at[pl.ds(idx_ref[0, 0, r], 1)], xbuf.at[dst_slot, pl.ds(r, 1)],
                              sem_g.at[dst_slot]).start(priority=priority)

    def gather_burst(b):
        for r in range(b * per, (b + 1) * per):
            gather_row(tokn_ref, r, 1 - slot, r % 2)

    def scatter_burst(b):
        for r in range(b * per, (b + 1) * per):
            p = jnp.where(i == 0, spare0 + r, posp_ref[0, 0, r])
            pltpu.make_async_copy(ybuf.at[1 - slot, pl.ds(r, 1)], y_hbm.at[pl.ds(p, 1)],
                                  sem_s.at[slot]).start(priority=r % 2)

    def wait_scatter(parity):
        pltpu.make_async_copy(ybuf.at[0], y_hbm.at[pl.ds(0, mb)], sem_s.at[parity]).wait()

    @pl.when(i == 0)
    def _():
        ybuf[1] = jnp.zeros(ybuf.shape[1:], u32)

        def body(r, carry):
            gather_row(tok0_ref, r, 0)
            pltpu.make_async_copy(ybuf.at[1, pl.ds(r, 1)], y_hbm.at[pl.ds(spare0 + mb + r, 1)], sem_s.at[1]).start()
            return carry
        lax.fori_loop(0, mb, body, 0)

    @pl.when(i <= n_used)
    def _():
        pltpu.make_async_copy(h_hbm.at[pl.ds(0, mb)], xbuf.at[slot], sem_g.at[slot]).wait()

    ib = jnp.minimum(i, nblk - 1)
    changed = (i == 0) | (be_ref[ib] != be_ref[jnp.maximum(ib - 1, 0)])

    @pl.when(changed & (i < n_used))
    def _():
        wgc[...] = wg_ref[...].astype(bf16)
        wuc[...] = wu_ref[...].astype(bf16)
        wdc[...] = wd_ref[...].astype(bf16)

    @pl.when(i < n_used)
    def _():
        lo, hi = _unpack_halves(_tiles_to_rows(xbuf[slot]))
        x = jnp.concatenate([lo.astype(bf16), hi.astype(bf16)], axis=1)
        nh = EXPERT_BURSTS // 4
        hw = de // nh
        hids = []
        for c in range(nh):
            gate = _dot(x, wgc[:, c * hw:(c + 1) * hw])
            gather_burst(2 * c)
            up = _dot(x, wuc[:, c * hw:(c + 1) * hw])
            gather_burst(2 * c + 1)
            hids.append((_silu(gate) * up).astype(bf16))
        hid = jnp.concatenate(hids, axis=1)
        nd = EXPERT_BURSTS // 2
        dw = dp // nd
        for c in range(nd):
            wd_c = jnp.concatenate([wdc[:, c * dw:(c + 1) * dw], wdc[:, dp + c * dw:dp + (c + 1) * dw]], axis=1)
            yrow[:, c * dw:(c + 1) * dw] = _pack_halves(_dot(hid, wd_c))
            scatter_burst(c)
        wait_scatter(1 - slot)
        ybuf[slot] = _rows_to_tiles(yrow[...])

    @pl.when(i == n_used)
    def _():
        wait_scatter(1 - slot)
        for b in range(EXPERT_BURSTS // 2):
            scatter_burst(b)
        wait_scatter(slot)


def _experts(h, slot_tok, slot_pos, block_e, n_used, wg, wu, wd, layer, n_rows):
    tile = h.shape[1:]
    dp = tile[0] * tile[1]
    d = wg.shape[2]
    nblk = slot_tok.shape[0]
    de = wg.shape[3]
    mb = MOE_BLOCK
    last = nblk - 1
    smem = lambda f: pl.BlockSpec((1, 1, mb), f, memory_space=pltpu.SMEM)
    wspec = lambda shape: pl.BlockSpec((None,) + shape, lambda i, be, nu: (layer, be[jnp.minimum(i, last)], 0, 0))
    grid_spec = pltpu.PrefetchScalarGridSpec(
        num_scalar_prefetch=2,
        grid=(nblk + 1,),
        in_specs=[smem(lambda i, be, nu: (0, 0, 0)),
                  smem(lambda i, be, nu: (jnp.minimum(i + 1, last), 0, 0)),
                  smem(lambda i, be, nu: (jnp.clip(i - 1, 0, last), 0, 0)),
                  pl.BlockSpec(memory_space=pl.ANY),
                  wspec((None, d, de)), wspec((None, d, de)), wspec((None, de, d))],
        out_specs=pl.BlockSpec(memory_space=pl.ANY),
        scratch_shapes=[pltpu.VMEM((2, mb) + tile, u32), pltpu.VMEM((2, mb) + tile, u32), pltpu.VMEM((mb, dp), u32),
                        pltpu.VMEM((d, de), bf16), pltpu.VMEM((d, de), bf16), pltpu.VMEM((de, d), bf16),
                        pltpu.SemaphoreType.DMA((2,)), pltpu.SemaphoreType.DMA((2,))],
    )
    st = slot_tok.reshape(nblk, 1, mb)
    sp = slot_pos.reshape(nblk, 1, mb)
    return pl.pallas_call(
        _expert_kernel,
        grid_spec=grid_spec,
        out_shape=jax.ShapeDtypeStruct((n_rows * TOP_K + 2 * mb,) + tile, u32),
        compiler_params=_params(("arbitrary",)),
    )(block_e, n_used, st, st, sp, h, wg, wu, wd)


COMBINE_ROWS = 128


def _combine_kernel(*refs, final_norm):
    y_refs = refs[:TOP_K]
    w_ref, h_ref, sg_ref, su_ref, sd_ref, x_ref, gate_ref, gf_ref, o_ref = refs[TOP_K:]
    lo, hi = _unpack_halves(_tiles_to_rows(h_ref[...]))
    hb = jnp.concatenate([lo.astype(bf16), hi.astype(bf16)], axis=1)
    hid = _silu(_dot(hb, sg_ref[...])) * _dot(hb, su_ref[...])
    acc = _dot(hid.astype(bf16), sd_ref[...])
    shape = y_refs[0].shape
    acc_lo = acc_hi = 0.0
    for k in range(TOP_K):
        lo, hi = _unpack_halves(y_refs[k][...])
        wk = jnp.broadcast_to(w_ref[:, k:k + 1, :], shape)
        acc_lo = acc_lo + lo * wk
        acc_hi = acc_hi + hi * wk
    acc = acc + jnp.concatenate([_tiles_to_rows(acc_lo), _tiles_to_rows(acc_hi)], axis=1)
    out = x_ref[...] + gate_ref[...] * acc
    if final_norm:
        out = out * lax.rsqrt(jnp.mean(out * out, axis=-1, keepdims=True) + EPS) * gf_ref[...]
    o_ref[...] = out


def _combine(y_rows, wts, h, sg, su, sd, x, mod, which, g_final, dims, n_rows, final_norm):
    d = x.shape[1]
    de = sg.shape[1]
    tile = h.shape[1:]
    tm = COMBINE_ROWS
    per = ROW_BLOCK // tm
    nsteps = n_rows // tm
    nlb, bpb, nb = dims
    mrow = lambda i: _mod_row(i // per, nlb, bpb, nb)
    yspec = lambda k: pl.BlockSpec((tm,) + tile, lambda i: (k * nsteps + i, 0, 0))
    w3 = jnp.broadcast_to(wts.T[:, :, None], (n_rows, TOP_K, LANE))
    return pl.pallas_call(
        functools.partial(_combine_kernel, final_norm=final_norm),
        grid=(nsteps,),
        in_specs=[yspec(k) for k in range(TOP_K)] + [
                  pl.BlockSpec((tm, TOP_K, LANE), lambda i: (i, 0, 0)),
                  pl.BlockSpec((tm,) + tile, lambda i: (i, 0, 0)),
                  pl.BlockSpec((d, de), lambda i: (0, 0)),
                  pl.BlockSpec((d, de), lambda i: (0, 0)),
                  pl.BlockSpec((de, d), lambda i: (0, 0)),
                  pl.BlockSpec((tm, d), lambda i: (i, 0)),
                  pl.BlockSpec((None, 1, d), lambda i: (mrow(i) * 6 + which, 0, 0)),
                  pl.BlockSpec((1, d), lambda i: (0, 0))],
        out_specs=pl.BlockSpec((tm, d), lambda i: (i, 0)),
        out_shape=jax.ShapeDtypeStruct((n_rows, d), f32),
        compiler_params=_params(("parallel",)),
    )(*([y_rows] * TOP_K), w3, h, sg, su, sd, x, mod, g_final.reshape(1, d))


def _moe(x, g, mod, which_shift, w_router, b_router, wg, wu, wd, layer, sg, su, sd, g_final, dims, n_rows,
         final_norm):
    ne = N_EXPERTS
    mb = MOE_BLOCK
    h, idx, wts, rank, cnt = _router(x, g, mod, which_shift, w_router, b_router, dims, n_rows)
    counts = cnt[:, 0].astype(i32)
    padded = (counts + mb - 1) // mb * mb
    pend = jnp.cumsum(padded)
    start = pend - padded
    onehot = idx[:, :, None] == jnp.arange(ne, dtype=i32)[None, None, :]
    dest = jnp.sum(jnp.where(onehot, start[None, None, :], 0), axis=-1) + rank
    n_assign = n_rows * TOP_K
    nblk = (n_assign + ne * (mb - 1) + mb - 1) // mb
    n_pad = nblk * mb - n_assign
    pcnt = padded - counts
    pcum = jnp.cumsum(pcnt)
    j = jnp.arange(n_pad, dtype=i32)
    e_j = jnp.sum((pcum[None, :] <= j[:, None]).astype(i32), axis=1)
    base = start + counts - (pcum - pcnt)
    base_j = jnp.sum(jnp.where(e_j[:, None] == jnp.arange(ne, dtype=i32)[None, :], base[None, :], 0), axis=1)
    pad_slot = jnp.where(e_j < ne, base_j + j, pend[-1] + j - pcum[-1])
    keys = jnp.concatenate([dest.reshape(-1), pad_slot])
    vals = jnp.concatenate([jnp.arange(n_assign, dtype=i32), n_assign + pad_slot % mb])
    _, slot_pos = lax.sort((keys, vals), num_keys=1)
    slot_tok = jnp.where(slot_pos < n_assign, slot_pos % n_rows, 0)
    blk_start = jnp.arange(nblk, dtype=i32) * mb
    block_e = jnp.minimum(jnp.sum((pend[None, :] <= blk_start[:, None]).astype(i32), axis=1), ne - 1)
    n_used = (pend[-1:] // mb).astype(i32)
    y_rows = _experts(h, slot_tok.reshape(nblk, mb), slot_pos.reshape(nblk, mb), block_e, n_used, wg, wu, wd,
                      layer, n_rows)
    return _combine(y_rows, wts, h, sg, su, sd, x, mod, which_shift + 2, g_final, dims, n_rows, final_norm)


def _rope_tables(seq, head_dim):
    half = head_dim // 2
    quarter = half // 2
    t = jnp.arange(seq, dtype=i32)
    rows = (t // GRID_W).astype(f32)
    cols = (t % GRID_W).astype(f32)
    freqs = ROPE_THETA ** (-jnp.arange(0, half, 2, dtype=f32) / half)
    ar = rows[:, None] * freqs[None, :]
    ac = cols[:, None] * freqs[None, :]
    cos = jnp.concatenate([jnp.cos(ar), jnp.cos(ar), jnp.cos(ac), jnp.cos(ac)], axis=1)
    sin = jnp.concatenate([-jnp.sin(ar), jnp.sin(ar), -jnp.sin(ac), jnp.sin(ac)], axis=1)
    reps = LANE // head_dim
    cos = jnp.tile(cos, (1, reps))
    sin = jnp.tile(sin, (1, reps))
    cos = jnp.concatenate([cos, jnp.ones((ROW_BLOCK, LANE), f32)], axis=0)
    sin = jnp.concatenate([sin, jnp.zeros((ROW_BLOCK, LANE), f32)], axis=0)
    return cos, sin, quarter


def kernel(x, c, ctx, c_ctx, w_ada, b_ada, g_norm_mix, g_norm_ffn, w_in_even, w_out_even, hgrn_lb_logits,
           g_hgrn_norm, win_sink, w_in_odd, w_out_odd, g_q_norm, g_k_norm, w_router, b_router,
           w_exp_gate, w_exp_up, w_exp_down, w_sh_gate, w_sh_up, w_sh_down, g_norm_final):
    nb, seq, d = x.shape
    lc = ctx.shape[1]
    assert seq % ROW_BLOCK == 0 and lc % ROW_BLOCK == 0 and seq % lc == 0
    t_lat = nb * seq
    t_all = t_lat + nb * lc
    bpb = seq // ROW_BLOCK
    nlb = t_lat // ROW_BLOCK
    dims = (nlb, bpb, nb)

    xs = jnp.concatenate([x.reshape(t_lat, d), ctx.reshape(nb * lc, d)], axis=0)
    cc = jnp.concatenate([c, c_ctx[None, :], jnp.zeros((8 - nb - 1, d), f32)], axis=0)

    lb_w = jax.nn.softmax(hgrn_lb_logits.astype(f32), axis=0)
    lower_bounds = jnp.cumsum(lb_w, axis=0)[1:] - lb_w[0]

    mod = _ada(cc, w_ada, b_ada, 0).reshape(8 * 6, 1, d)
    w_in = w_in_even[0].astype(bf16)
    proj = _normmm(xs, g_norm_mix[0], mod, 0, w_in[:, :HGRN_COLS], f32, dims, tn=1280)
    cos, sin, quarter = _rope_tables(seq, WIN_HEAD_DIM)
    epi = dict(blocks={0: (WIN_HEADS + WIN_KV_HEADS) * WIN_HEAD_DIM // LANE}, quarter=quarter, head_norm=False,
               q_slices=WIN_HEADS * WIN_HEAD_DIM // LANE, q_scale=WIN_HEAD_DIM ** -0.5)
    wproj = _normmm(xs, g_norm_mix[0], mod, 0, w_in[:, HGRN_COLS:], bf16, dims, tn=WIN_COLS,
                    epilogue=epi, tables=(cos, sin))
    o_f, o_b = _hgrn(proj, lower_bounds[0], nb, seq, lc)
    a_mix = _hgrn_out(o_f, o_b, proj, g_hgrn_norm[0])
    b_mix = _win_attn(wproj, win_sink[0], nb, seq, lc)
    xs = _outproj(a_mix, 0, b_mix, 0, w_out_even[0].astype(bf16), xs, mod, 2, dims, t_all)
    xs = _moe(xs, g_norm_ffn[0], mod, 3, w_router[0], b_router[0],
              w_exp_gate, w_exp_up, w_exp_down, 0,
              w_sh_gate[0].astype(bf16), w_sh_up[0].astype(bf16), w_sh_down[0].astype(bf16),
              g_norm_final, dims, t_all, False)

    mod = _ada(cc, w_ada, b_ada, 1).reshape(8 * 6, 1, d)
    cos, sin, quarter = _rope_tables(seq, GLB_HEAD_DIM)
    tn = 1536
    n_sl = tn // LANE
    rope_slices = GLB_HEADS + GLB_KV_HEADS
    epi = dict(blocks={0: min(rope_slices, n_sl), 1: max(rope_slices - n_sl, 0)}, quarter=quarter, head_norm=True,
               q_slices=GLB_HEADS, q_scale=GLB_HEAD_DIM ** -0.5)
    qkv = _normmm(xs, g_norm_mix[1], mod, 0, w_in_odd[0].astype(bf16), bf16, dims, tn=tn,
                  epilogue=epi, tables=(cos, sin), gains=(g_q_norm[0], g_k_norm[0]))
    att = _glb_attn(qkv, nb, seq, lc)
    xl = _outproj(att, 0, att, 1, w_out_odd[0].astype(bf16), xs, mod, 2, dims, t_lat)
    out = _moe(xl, g_norm_ffn[1], mod, 3, w_router[1], b_router[1],
               w_exp_gate, w_exp_up, w_exp_down, 1,
               w_sh_gate[1].astype(bf16), w_sh_up[1].astype(bf16), w_sh_down[1].astype(bf16),
               g_norm_final, dims, t_lat, True)
    return out.reshape(nb, seq, d)
```

```python
import functools

import jax
import jax.numpy as jnp
from jax import lax
from jax.experimental import pallas as pl
from jax.experimental.pallas import tpu as pltpu

f32 = jnp.float32
bf16 = jnp.bfloat16
i32 = jnp.int32
u32 = jnp.uint32

EPS = 1e-6
LOG2E = 1.4426950408889634
ROPE_THETA = 10000.0
GRID_W = 64

HGRN_DK = 128
HGRN_HEADS = 8
HGRN_CHUNK = 64
HGRN_SUB = 16
HGRN_HEADS_PER_STEP = 8
HGRN_SAFE_DECAY = 80.0
WIN_HEAD_DIM = 64
WIN_HEADS = 16
WIN_KV_HEADS = 2
WIN_BLOCK = 128
GLB_HEAD_DIM = 128
GLB_HEADS = 16
GLB_KV_HEADS = 4
Q_BLOCK = 256
N_EXPERTS = 64
N_GROUPS = 8
TOPK_GROUPS = 4
TOP_K = 8
ROUTED_SCALE = 2.5
MOE_BLOCK = 256

LANE = 128
ROW_BLOCK = 256
VMEM_LIMIT = 56 * 1024 * 1024

A_K = HGRN_HEADS * HGRN_DK
HGRN_COLS = 5 * A_K
WIN_COLS = WIN_HEADS * WIN_HEAD_DIM + 2 * WIN_KV_HEADS * WIN_HEAD_DIM


def _params(sem, vmem=VMEM_LIMIT):
    return pltpu.CompilerParams(dimension_semantics=sem, vmem_limit_bytes=vmem)


def _silu(x):
    return x * jax.nn.sigmoid(x)


def _dot(a, b):
    return jnp.dot(a, b, preferred_element_type=f32)


def _dot_nt(a, b):
    return lax.dot_general(a, b, (((1,), (1,)), ((), ())), preferred_element_type=f32)


def _pack_halves(x):
    n = x.shape[1] // 2
    lo = lax.bitcast_convert_type(x[:, :n].astype(bf16).astype(f32), u32)
    hi = lax.bitcast_convert_type(x[:, n:].astype(bf16).astype(f32), u32)
    return (lo >> 16) | hi


def _unpack_halves(w):
    lo = lax.bitcast_convert_type(w << 16, f32)
    hi = lax.bitcast_convert_type(w & jnp.uint32(0xFFFF0000), f32)
    return lo, hi


def _rows_to_tiles(w):
    n = w.shape[1] // LANE
    return pltpu.einshape("stl->tsl", jnp.stack([w[:, s * LANE:(s + 1) * LANE] for s in range(n)], axis=0))


def _tiles_to_rows(t):
    y = pltpu.einshape("tsl->stl", t)
    return jnp.concatenate([y[s] for s in range(t.shape[1])], axis=1)


def _dot_tn(a, b):
    return lax.dot_general(a, b, (((0,), (0,)), ((), ())), preferred_element_type=f32)


def _ada_kernel(c_ref, w_ref, b_ref, o_ref):
    a = _silu(c_ref[...]).astype(bf16)
    o_ref[...] = _dot(a, w_ref[...].astype(bf16)) + b_ref[...]


def _ada(c8, w, b, layer, tn=768):
    m, d = c8.shape
    n = w.shape[2]
    return pl.pallas_call(
        _ada_kernel,
        grid=(n // tn,),
        in_specs=[pl.BlockSpec((m, d), lambda j: (0, 0)),
                  pl.BlockSpec((None, d, tn), lambda j: (layer, 0, j)),
                  pl.BlockSpec((None, 1, tn), lambda j: (layer, 0, j))],
        out_specs=pl.BlockSpec((m, tn), lambda j: (0, j)),
        out_shape=jax.ShapeDtypeStruct((m, n), f32),
        compiler_params=_params(("arbitrary",)),
    )(c8, w, b.reshape(b.shape[0], 1, n))


def _norm_mod(x, g, sc, sh):
    y = x * lax.rsqrt(jnp.mean(x * x, axis=-1, keepdims=True) + EPS) * g
    return y * (1.0 + sc) + sh


def _rope_slice(x, cos, sin, quarter):
    lane = lax.broadcasted_iota(i32, x.shape, 1)
    up = pltpu.roll(x, LANE - quarter, axis=1)
    dn = pltpu.roll(x, quarter, axis=1)
    partner = jnp.where(lane % (2 * quarter) < quarter, up, dn)
    return x * cos + partner * sin


def _normmm_kernel(x_ref, g_ref, sc_ref, sh_ref, w_ref, *rest, epilogue, tn):
    if epilogue is None:
        (o_ref,) = rest
    elif epilogue["head_norm"]:
        cos_ref, sin_ref, gq_ref, gk_ref, o_ref = rest
    else:
        cos_ref, sin_ref, o_ref = rest
    j = pl.program_id(0)
    h = _norm_mod(x_ref[...], g_ref[...], sc_ref[...], sh_ref[...]).astype(bf16)
    y = _dot(h, w_ref[...])
    if epilogue is None:
        o_ref[...] = y.astype(o_ref.dtype)
        return

    n_sl = tn // LANE
    for jb in range(epilogue["n_col_blocks"]):
        n_rope = epilogue["blocks"].get(jb, 0)

        @pl.when(j == jb)
        def _(jb=jb, n_rope=n_rope):
            cos = cos_ref[...]
            sin = sin_ref[...]
            outs = []
            for s in range(n_sl):
                ys = y[:, s * LANE:(s + 1) * LANE]
                if s < n_rope:
                    col = jb * n_sl + s
                    is_q = col < epilogue["q_slices"]
                    if epilogue["head_norm"]:
                        gain = gq_ref[...] if is_q else gk_ref[...]
                        ys = ys * lax.rsqrt(jnp.mean(ys * ys, axis=-1, keepdims=True) + EPS) * gain
                    ys = _rope_slice(ys, cos, sin, epilogue["quarter"])
                    if is_q:
                        ys = ys * epilogue["q_scale"]
                outs.append(ys.astype(o_ref.dtype))
            o_ref[...] = jnp.concatenate(outs, axis=1)


def _mod_row(i, n_lat_blocks, blocks_per_batch, n_batch):
    return jnp.where(i < n_lat_blocks, i // blocks_per_batch, n_batch)


def _normmm(x, g, mod, which, w, out_dtype, dims, tn, epilogue=None, tables=None, gains=None):
    t, d = x.shape
    n = w.shape[1]
    tm = ROW_BLOCK
    nlb, bpb, nb = dims
    mrow = lambda i: _mod_row(i, nlb, bpb, nb)
    in_specs = [pl.BlockSpec((tm, d), lambda j, i: (i, 0)),
                pl.BlockSpec((1, d), lambda j, i: (0, 0)),
                pl.BlockSpec((None, 1, d), lambda j, i: (mrow(i) * 6 + which + 1, 0, 0)),
                pl.BlockSpec((None, 1, d), lambda j, i: (mrow(i) * 6 + which, 0, 0)),
                pl.BlockSpec((d, tn), lambda j, i: (0, j))]
    args = [x, g.reshape(1, d), mod, mod, w]
    if epilogue is not None:
        cos, sin = tables
        tab = lambda j, i: (jnp.where(i < nlb, i % bpb, bpb), 0)
        in_specs += [pl.BlockSpec((tm, LANE), tab), pl.BlockSpec((tm, LANE), tab)]
        args += [cos, sin]
        epilogue = dict(epilogue, n_col_blocks=n // tn)
        if epilogue["head_norm"]:
            in_specs += [pl.BlockSpec((1, LANE), lambda j, i: (0, 0))] * 2
            args += [gains[0].reshape(1, LANE), gains[1].reshape(1, LANE)]
    return pl.pallas_call(
        functools.partial(_normmm_kernel, epilogue=epilogue, tn=tn),
        grid=(n // tn, t // tm),
        in_specs=in_specs,
        out_specs=pl.BlockSpec((tm, tn), lambda j, i: (i, j)),
        out_shape=jax.ShapeDtypeStruct((t, n), out_dtype),
        compiler_params=_params(("parallel", "parallel")),
    )(*args)


def _gla_gates(q_raw, f_raw, lb, rev):
    c = HGRN_CHUNK
    sub = HGRN_SUB
    q = _silu(q_raw)
    f = lb + (1.0 - lb) * jax.nn.sigmoid(f_raw)
    kk = 1.0 - f
    g = jnp.log(f)
    row = lax.broadcasted_iota(i32, (c, c), 0)
    col = lax.broadcasted_iota(i32, (c, c), 1)
    tri = (col >= row) if rev else (col <= row)
    b = jnp.dot(tri.astype(f32), g, preferred_element_type=f32, precision=lax.Precision.HIGHEST)
    span = None
    for jb in range(c // sub):
        d = jnp.abs(b[jb * sub:jb * sub + 1] - b[(jb + 1) * sub - 1:(jb + 1) * sub])
        span = d if span is None else jnp.maximum(span, d)
    return q, kk, b, span


def _gla_chunks_factored(chains):
    c = HGRN_CHUNK
    sub = HGRN_SUB
    nsub = c // sub
    rrow = lax.broadcasted_iota(i32, (c, sub), 0)
    rcol = lax.broadcasted_iota(i32, (c, sub), 1)
    work = []
    for (q, kk, b), v_ref, st_ref, _, o_ref, sl, rev in chains:
        st = st_ref[...]
        o = _dot_nt((q * jnp.exp(b)).astype(bf16), st.astype(bf16))
        cols = []
        for jb in range(nsub):
            js = slice(jb * sub, (jb + 1) * sub)
            m_j = b[jb * sub:jb * sub + 1] if rev else b[(jb + 1) * sub - 1:(jb + 1) * sub]
            kd = kk[js] * jnp.exp(m_j - b[js])
            qd = q * jnp.exp(jnp.minimum(b - m_j, HGRN_SAFE_DECAY))
            cols.append(_dot_nt(qd.astype(bf16), kd.astype(bf16)))
        work.append((o, cols, st))
    outs = []
    for ((q, kk, b), v_ref, st_ref, _, o_ref, sl, rev), (o, cols, st) in zip(chains, work):
        vb = v_ref[:, sl].astype(bf16)
        for jb in range(nsub):
            keep = (rrow <= jb * sub + rcol) if rev else (rrow >= jb * sub + rcol)
            a = jnp.where(keep, cols[jb], 0.0).astype(bf16)
            o = o + _dot(a, vb[jb * sub:(jb + 1) * sub])
        b_end = b[0:1] if rev else b[c - 1:c]
        kdec = kk * jnp.exp(b_end - b)
        outs.append((o, jnp.exp(b_end) * st + _dot_tn(vb, kdec.astype(bf16))))
    for (_, v_ref, st_ref, _, o_ref, sl, rev), (o, st_new) in zip(chains, outs):
        o_ref[:, sl] = o
        st_ref[...] = st_new


def _gla_chunk(q, kk, b, v, st_ref, a_scr, rev):
    c = HGRN_CHUNK
    sub = HGRN_SUB
    b_end = b[0:1] if rev else b[c - 1:c]
    st = st_ref[...]
    o = _dot_nt((q * jnp.exp(b)).astype(bf16), st.astype(bf16))

    vb = v.astype(bf16)
    nsub = c // sub
    lane16 = lax.broadcasted_iota(i32, (sub, sub), 1)
    row16 = lax.broadcasted_iota(i32, (sub, sub), 0)
    diag_ok = (row16 <= lane16) if rev else (row16 >= lane16)
    rsub = lax.broadcasted_iota(i32, (c, sub), 0) // sub
    for jb in range(nsub):
        js = slice(jb * sub, (jb + 1) * sub)
        m_j = b[jb * sub:jb * sub + 1] if rev else b[(jb + 1) * sub - 1:(jb + 1) * sub]
        kd = kk[js] * jnp.exp(m_j - b[js])
        qd = q * jnp.exp(jnp.minimum(b - m_j, 0.0))
        a_col = _dot_nt(qd.astype(bf16), kd.astype(bf16))
        off_ok = (rsub < jb) if rev else (rsub > jb)
        a_col = jnp.where(off_ok, a_col, 0.0)
        qi, bi, ki = q[js], b[js] * LOG2E, kk[js]
        a_dd = jnp.zeros((sub, sub), f32)
        for s in range(sub):
            e = jnp.exp2(bi - bi[s:s + 1])
            col_s = jnp.sum(qi * ki[s:s + 1] * e, axis=-1, keepdims=True)
            a_dd = jnp.where(lane16 == s, col_s, a_dd)
        a_dd = jnp.where(diag_ok, a_dd, 0.0)
        pieces = [a_dd if ib == jb else a_col[ib * sub:(ib + 1) * sub] for ib in range(nsub)]
        a_scr[:, js] = jnp.concatenate(pieces, axis=0)
    o = o + _dot(a_scr[...].astype(bf16), vb)
    kdec = kk * jnp.exp(b_end - b)
    st_ref[...] = jnp.exp(b_end) * st + _dot_tn(vb, kdec.astype(bf16))
    return o


def _hgrn_kernel(qf_ref, vf_ref, ff_ref, qb_ref, vb_ref, fb_ref, lb_ref, of_ref, ob_ref,
                 stf_ref, stb_ref, af_scr, ab_scr):
    @pl.when(pl.program_id(2) == 0)
    def _():
        stf_ref[...] = jnp.zeros_like(stf_ref)
        stb_ref[...] = jnp.zeros_like(stb_ref)

    gates = []
    span = None
    for j in range(HGRN_HEADS_PER_STEP):
        sl = slice(j * HGRN_DK, (j + 1) * HGRN_DK)
        lb = lb_ref[:, sl]
        gf = _gla_gates(qf_ref[:, sl], ff_ref[:, sl], lb, False)
        gb = _gla_gates(qb_ref[:, sl], fb_ref[:, sl], lb, True)
        gates.append((gf[:3], gb[:3]))
        for d in (gf[3], gb[3]):
            span = d if span is None else jnp.maximum(span, d)
    mild = jnp.max(span) <= HGRN_SAFE_DECAY

    chains = []
    for j in range(HGRN_HEADS_PER_STEP):
        sl = slice(j * HGRN_DK, (j + 1) * HGRN_DK)
        chains.append((gates[j][0], vf_ref, stf_ref.at[j], af_scr.at[j], of_ref, sl, False))
        chains.append((gates[j][1], vb_ref, stb_ref.at[j], ab_scr.at[j], ob_ref, sl, True))

    @pl.when(mild)
    def _():
        _gla_chunks_factored(chains)

    @pl.when(jnp.logical_not(mild))
    def _():
        for (q, kk, b), v_ref, st_ref, a_scr, o_ref, sl, rev in chains:
            o_ref[:, sl] = _gla_chunk(q, kk, b, v_ref[:, sl], st_ref, a_scr, rev)


def _hgrn(proj, lb, n_batch, seq, ctx_len):
    t = proj.shape[0]
    c = HGRN_CHUNK
    hp = HGRN_HEADS_PER_STEP
    wblk = hp * HGRN_DK
    ngrp = HGRN_HEADS // hp
    ncc, ncl = ctx_len // c, seq // c
    lat_chunks = n_batch * ncl

    def fwd_blk(b, s):
        return jnp.where(s < ncc, lat_chunks + b * ncc + s, b * ncl + (s - ncc))

    def bwd_blk(b, s):
        return jnp.where(s < ncc, lat_chunks + b * ncc + (ncc - 1 - s), b * ncl + (ncl - 1 - (s - ncc)))

    def spec(blk, section):
        return pl.BlockSpec((c, wblk), lambda b, h, s: (blk(b, s), section * ngrp + h))

    def ospec(blk):
        return pl.BlockSpec((c, wblk), lambda b, h, s: (blk(b, s), h))

    out = jax.ShapeDtypeStruct((t, A_K), f32)
    return pl.pallas_call(
        _hgrn_kernel,
        grid=(n_batch, ngrp, ncc + ncl),
        in_specs=[spec(fwd_blk, 0), spec(fwd_blk, 1), spec(fwd_blk, 2),
                  spec(bwd_blk, 0), spec(bwd_blk, 1), spec(bwd_blk, 3),
                  pl.BlockSpec((1, wblk), lambda b, h, s: (0, h))],
        out_specs=[ospec(fwd_blk), ospec(bwd_blk)],
        out_shape=[out, out],
        scratch_shapes=[pltpu.VMEM((hp, HGRN_DK, HGRN_DK), f32), pltpu.VMEM((hp, HGRN_DK, HGRN_DK), f32),
                        pltpu.VMEM((hp, c, c), f32), pltpu.VMEM((hp, c, c), f32)],
        compiler_params=_params(("parallel", "parallel", "arbitrary")),
    )(proj, proj, proj, proj, proj, proj, lb.reshape(1, A_K))


def _hgrn_out_kernel(of_ref, ob_ref, g_ref, gain_ref, o_ref):
    gain = gain_ref[...]
    outs = []
    for h in range(HGRN_HEADS):
        sl = slice(h * HGRN_DK, (h + 1) * HGRN_DK)
        o = of_ref[:, sl] + ob_ref[:, sl]
        o = o * lax.rsqrt(jnp.mean(o * o, axis=-1, keepdims=True) + EPS) * gain
        outs.append((o * _silu(g_ref[:, sl])).astype(bf16))
    o_ref[...] = jnp.concatenate(outs, axis=1)


def _hgrn_out(o_f, o_b, proj, gain):
    t = o_f.shape[0]
    tm = ROW_BLOCK
    return pl.pallas_call(
        _hgrn_out_kernel,
        grid=(t // tm,),
        in_specs=[pl.BlockSpec((tm, A_K), lambda i: (i, 0)),
                  pl.BlockSpec((tm, A_K), lambda i: (i, 0)),
                  pl.BlockSpec((tm, A_K), lambda i: (i, 4)),
                  pl.BlockSpec((1, HGRN_DK), lambda i: (0, 0))],
        out_specs=pl.BlockSpec((tm, A_K), lambda i: (i, 0)),
        out_shape=jax.ShapeDtypeStruct((t, A_K), bf16),
        compiler_params=_params(("parallel",)),
    )(o_f, o_b, proj, gain.reshape(1, HGRN_DK))


def _win_kernel(sink_ref, q_ref, kp_ref, kc_ref, kn_ref, vp_ref, vc_ref, vn_ref, kx_ref, vx_ref, o_ref,
                *, n_lat_blocks):
    n = pl.program_id(1)
    w = WIN_BLOCK
    dh = WIN_HEAD_DIM
    is_lat = n < n_lat_blocks
    ri = lax.broadcasted_iota(i32, (w, w), 0)
    ci = lax.broadcasted_iota(i32, (w, w), 1)
    ok_p = (ci >= ri) & is_lat & (n > 0)
    ok_c = jnp.broadcast_to(is_lat, (w, w))
    ok_n = (ci <= ri) & is_lat & (n < n_lat_blocks - 1)
    valid = jnp.concatenate([ok_p, ok_c, ok_n], axis=1)
    kwin = jnp.concatenate([kp_ref[...], kc_ref[...], kn_ref[...]], axis=0)
    vwin = jnp.concatenate([vp_ref[...], vc_ref[...], vn_ref[...]], axis=0)
    kx = kx_ref[...]
    vx = vx_ref[...]
    g = WIN_HEADS // WIN_KV_HEADS
    outs = []
    for h in range(WIN_HEADS):
        kv = h // g
        ks = slice(kv * dh, (kv + 1) * dh)
        qh = q_ref[:, h * dh:(h + 1) * dh]
        sw = jnp.where(valid, _dot_nt(qh, kwin[:, ks]), -jnp.inf)
        sx = _dot_nt(qh, kx[:, ks])
        sk = sink_ref[h]
        m = jnp.maximum(jnp.maximum(jnp.max(sw, axis=-1, keepdims=True), jnp.max(sx, axis=-1, keepdims=True)), sk)
        pw = jnp.exp(sw - m)
        px = jnp.exp(sx - m)
        den = jnp.sum(pw, axis=-1, keepdims=True) + jnp.sum(px, axis=-1, keepdims=True) + jnp.exp(sk - m)
        o = _dot(pw.astype(bf16), vwin[:, ks]) + _dot(px.astype(bf16), vx[:, ks])
        outs.append((o / den).astype(bf16))
    o_ref[...] = jnp.concatenate(outs, axis=1)


def _win_attn(wproj, sink, n_batch, seq, ctx_len):
    t = wproj.shape[0]
    w = WIN_BLOCK
    nlb = seq // w
    ncb = ctx_len // w
    lat_blocks = n_batch * nlb
    qc = WIN_HEADS * WIN_HEAD_DIM // LANE
    kcol, vcol = qc, qc + 1

    def qrow(b, n):
        return jnp.where(n < nlb, b * nlb + n, lat_blocks + b * ncb + (n - nlb))

    def krow(off):
        def f(b, n):
            return b * nlb + jnp.clip(n + off, 0, nlb - 1)
        return f

    def kspec(off, colblk):
        return pl.BlockSpec((w, LANE), lambda b, n: (krow(off)(b, n), colblk))

    def xspec(colblk):
        return pl.BlockSpec((ctx_len, LANE), lambda b, n: ((n_batch * seq) // ctx_len + b, colblk))

    return pl.pallas_call(
        functools.partial(_win_kernel, n_lat_blocks=nlb),
        grid=(n_batch, nlb + ncb),
        in_specs=[pl.BlockSpec(memory_space=pltpu.SMEM),
                  pl.BlockSpec((w, qc * LANE), lambda b, n: (qrow(b, n), 0)),
                  kspec(-1, kcol), kspec(0, kcol), kspec(1, kcol),
                  kspec(-1, vcol), kspec(0, vcol), kspec(1, vcol),
                  xspec(kcol), xspec(vcol)],
        out_specs=pl.BlockSpec((w, qc * LANE), lambda b, n: (qrow(b, n), 0)),
        out_shape=jax.ShapeDtypeStruct((t, qc * LANE), bf16),
        compiler_params=_params(("parallel", "arbitrary")),
    )(sink.astype(f32), wproj, wproj, wproj, wproj, wproj, wproj, wproj, wproj, wproj)


GLB_KEY_CHUNK = 256


def _glb_kernel(q_ref, kx_ref, kl_ref, vx_ref, vl_ref, o_ref, m_scr, l_scr, acc_scr):
    dh = GLB_HEAD_DIM
    g = GLB_HEADS // GLB_KV_HEADS
    ck = GLB_KEY_CHUNK
    tq = q_ref.shape[0]
    q = jnp.concatenate([q_ref[:, i * dh:(i + 1) * dh] for i in range(g)], axis=0)

    def update(k, v, first):
        s = _dot_nt(q, k)
        reps = s.shape[1] // LANE
        mx = jnp.broadcast_to(jnp.max(s, axis=-1, keepdims=True), (g * tq, LANE))
        if first:
            m_new = mx
            p = jnp.exp(s - jnp.concatenate([m_new] * reps, axis=1))
            l_scr[...] = jnp.broadcast_to(jnp.sum(p, axis=-1, keepdims=True), (g * tq, LANE))
            acc_scr[...] = _dot(p.astype(bf16), v)
        else:
            m_old = m_scr[...]
            m_new = jnp.maximum(m_old, mx)
            alpha = jnp.exp(m_old - m_new)
            p = jnp.exp(s - jnp.concatenate([m_new] * reps, axis=1))
            l_scr[...] = alpha * l_scr[...] + jnp.broadcast_to(jnp.sum(p, axis=-1, keepdims=True), (g * tq, LANE))
            acc_scr[...] = alpha * acc_scr[...] + _dot(p.astype(bf16), v)
        m_scr[...] = m_new

    chunks = [(kx_ref, vx_ref, c, min(ck, kx_ref.shape[0] - c)) for c in range(0, kx_ref.shape[0], ck)]
    chunks += [(kl_ref, vl_ref, c, min(ck, kl_ref.shape[0] - c)) for c in range(0, kl_ref.shape[0], ck)]
    for n, (kr, vr, c0, sz) in enumerate(chunks):
        update(kr[c0:c0 + sz, :], vr[c0:c0 + sz, :], n == 0)
    o = acc_scr[...] / l_scr[...]
    o_ref[...] = jnp.concatenate([o[i * tq:(i + 1) * tq] for i in range(g)], axis=1).astype(bf16)


def _glb_attn(qkv, n_batch, seq, ctx_len):
    tq = Q_BLOCK
    dh = GLB_HEAD_DIM
    g = GLB_HEADS // GLB_KV_HEADS
    nq = seq // tq
    kcol0 = GLB_HEADS
    vcol0 = GLB_HEADS + GLB_KV_HEADS
    ctx_blk0 = (n_batch * seq) // ctx_len
    return pl.pallas_call(
        _glb_kernel,
        grid=(n_batch, GLB_KV_HEADS, nq),
        in_specs=[pl.BlockSpec((tq, g * dh), lambda b, k, i: (b * nq + i, k)),
                  pl.BlockSpec((ctx_len, dh), lambda b, k, i: (ctx_blk0 + b, kcol0 + k)),
                  pl.BlockSpec((seq, dh), lambda b, k, i: (b, kcol0 + k)),
                  pl.BlockSpec((ctx_len, dh), lambda b, k, i: (ctx_blk0 + b, vcol0 + k)),
                  pl.BlockSpec((seq, dh), lambda b, k, i: (b, vcol0 + k))],
        out_specs=pl.BlockSpec((tq, g * dh), lambda b, k, i: (b * nq + i, k)),
        out_shape=jax.ShapeDtypeStruct((n_batch * seq, GLB_HEADS * dh), bf16),
        scratch_shapes=[pltpu.VMEM((g * tq, LANE), f32), pltpu.VMEM((g * tq, LANE), f32), pltpu.VMEM((g * tq, dh), f32)],
        compiler_params=_params(("parallel", "parallel", "arbitrary")),
    )(qkv, qkv, qkv, qkv, qkv)


def _outproj_kernel(a1_ref, a2_ref, w1_ref, w2_ref, x_ref, gate_ref, o_ref):
    y = _dot(a1_ref[...], w1_ref[...]) + _dot(a2_ref[...], w2_ref[...])
    o_ref[...] = x_ref[...] + gate_ref[...] * y


def _outproj(a1, c1, a2, c2, w, x, mod, which, dims, n_rows, tn=1024):
    d = x.shape[1]
    kh = w.shape[0] // 2
    tm = ROW_BLOCK
    nlb, bpb, nb = dims
    mrow = lambda i: _mod_row(i, nlb, bpb, nb)
    return pl.pallas_call(
        _outproj_kernel,
        grid=(d // tn, n_rows // tm),
        in_specs=[pl.BlockSpec((tm, kh), lambda j, i: (i, c1)),
                  pl.BlockSpec((tm, kh), lambda j, i: (i, c2)),
                  pl.BlockSpec((kh, tn), lambda j, i: (0, j)),
                  pl.BlockSpec((kh, tn), lambda j, i: (1, j)),
                  pl.BlockSpec((tm, tn), lambda j, i: (i, j)),
                  pl.BlockSpec((None, 1, tn), lambda j, i: (mrow(i) * 6 + which, 0, j))],
        out_specs=pl.BlockSpec((tm, tn), lambda j, i: (i, j)),
        out_shape=jax.ShapeDtypeStruct((n_rows, d), f32),
        compiler_params=_params(("parallel", "parallel")),
    )(a1, a2, w, w, x, mod)


def _router_kernel(x_ref, g_ref, sc_ref, sh_ref, wr_ref, br_ref, u_ref,
                   h_ref, idx_ref, wt_ref, rank_ref, cnt_ref, carry_scr):
    i = pl.program_id(0)
    tm = x_ref.shape[0]
    ne, ng, pg = N_EXPERTS, N_GROUPS, N_EXPERTS // N_GROUPS

    @pl.when(i == 0)
    def _():
        carry_scr[...] = jnp.zeros_like(carry_scr)

    h = _norm_mod(x_ref[...], g_ref[...], sc_ref[...], sh_ref[...])
    h_ref[...] = _rows_to_tiles(_pack_halves(h))
    logits = lax.dot_general(wr_ref[...], h, (((1,), (1,)), ((), ())), preferred_element_type=f32,
                             precision=lax.Precision.HIGHEST)
    s = jax.nn.sigmoid(logits)
    s3 = s.reshape(ng, pg, tm)
    b3 = (s + br_ref[...]).reshape(ng, pg, tm)
    gi = lax.broadcasted_iota(i32, (ng, pg, tm), 0)
    pi = lax.broadcasted_iota(i32, (ng, pg, tm), 1)
    neg = -jnp.inf
    m1 = jnp.max(b3, axis=1, keepdims=True)
    first = jnp.min(jnp.where(b3 == m1, pi, pg), axis=1, keepdims=True)
    m2 = jnp.max(jnp.where(pi == first, neg, b3), axis=1, keepdims=True)
    gs = m1 + m2
    g2 = lax.broadcasted_iota(i32, (ng, 1, tm), 0)
    gmask = jnp.zeros((ng, 1, tm), jnp.bool_)
    for _ in range(TOPK_GROUPS):
        m = jnp.max(gs, axis=0, keepdims=True)
        fi = jnp.min(jnp.where(gs == m, g2, ng), axis=0, keepdims=True)
        pick = g2 == fi
        gmask = gmask | pick
        gs = jnp.where(pick, neg, gs)
    cand = jnp.where(gmask, b3, neg)
    eid = gi * pg + pi
    sel = jnp.zeros((ng, pg, tm), jnp.bool_)
    picks, idxs, wts = [], [], []
    for _ in range(TOP_K):
        m = jnp.max(jnp.max(cand, axis=0, keepdims=True), axis=1, keepdims=True)
        fi = jnp.min(jnp.min(jnp.where(cand == m, eid, ne), axis=0, keepdims=True), axis=1, keepdims=True)
        pick = eid == fi
        picks.append(pick)
        idxs.append(fi.reshape(1, tm))
        wts.append(jnp.sum(jnp.sum(jnp.where(pick, s3, 0.0), axis=0, keepdims=True), axis=1,
                           keepdims=True).reshape(1, tm))
        sel = sel | pick
        cand = jnp.where(pick, neg, cand)
    wsum = wts[0]
    for k in range(1, TOP_K):
        wsum = wsum + wts[k]
    idx_ref[...] = jnp.concatenate(idxs, axis=0)
    wt_ref[...] = jnp.concatenate([wk / wsum * ROUTED_SCALE for wk in wts], axis=0)
    sel2 = jnp.where(sel, 1.0, 0.0).reshape(ne, tm)
    prefix = (_dot(sel2.astype(bf16), u_ref[...]) + carry_scr[...]).reshape(ng, pg, tm)
    ranks = [jnp.sum(jnp.sum(jnp.where(pk, prefix, 0.0), axis=0, keepdims=True), axis=1,
                     keepdims=True).reshape(1, tm) for pk in picks]
    rank_ref[...] = jnp.concatenate(ranks, axis=0).astype(i32)
    total = carry_scr[...] + jnp.sum(sel2, axis=1, keepdims=True)
    carry_scr[...] = total
    cnt_ref[...] = total


def _router(x, g, mod, which, w_router, b_router, dims, n_rows):
    d = x.shape[1]
    tm = ROW_BLOCK
    ne = N_EXPERTS
    nlb, bpb, nb = dims
    mrow = lambda i: _mod_row(i, nlb, bpb, nb)
    tri = (jnp.arange(tm)[:, None] < jnp.arange(tm)[None, :]).astype(bf16)
    return pl.pallas_call(
        _router_kernel,
        grid=(n_rows // tm,),
        in_specs=[pl.BlockSpec((tm, d), lambda i: (i, 0)),
                  pl.BlockSpec((1, d), lambda i: (0, 0)),
                  pl.BlockSpec((None, 1, d), lambda i: (mrow(i) * 6 + which + 1, 0, 0)),
                  pl.BlockSpec((None, 1, d), lambda i: (mrow(i) * 6 + which, 0, 0)),
                  pl.BlockSpec((ne, d), lambda i: (0, 0)),
                  pl.BlockSpec((ne, 1), lambda i: (0, 0)),
                  pl.BlockSpec((tm, tm), lambda i: (0, 0))],
        out_specs=[pl.BlockSpec((tm, d // 2 // LANE, LANE), lambda i: (i, 0, 0)),
                   pl.BlockSpec((TOP_K, tm), lambda i: (0, i)),
                   pl.BlockSpec((TOP_K, tm), lambda i: (0, i)),
                   pl.BlockSpec((TOP_K, tm), lambda i: (0, i)),
                   pl.BlockSpec((ne, 1), lambda i: (0, 0))],
        out_shape=[jax.ShapeDtypeStruct((n_rows, d // 2 // LANE, LANE), u32),
                   jax.ShapeDtypeStruct((TOP_K, n_rows), i32),
                   jax.ShapeDtypeStruct((TOP_K, n_rows), f32),
                   jax.ShapeDtypeStruct((TOP_K, n_rows), i32),
                   jax.ShapeDtypeStruct((ne, 1), f32)],
        scratch_shapes=[pltpu.VMEM((ne, 1), f32)],
        compiler_params=_params(("arbitrary",)),
    )(x, g.reshape(1, d), mod, mod, w_router.T, b_router.reshape(ne, 1).astype(f32), tri)


EXPERT_BURSTS = 8


def _expert_kernel(be_ref, nu_ref, tok0_ref, tokn_ref, posp_ref, h_hbm, wg_ref, wu_ref, wd_ref, y_hbm,
                   xbuf, ybuf, yrow, wgc, wuc, wdc, sem_g, sem_s):
    i = pl.program_id(0)
    nblk = pl.num_programs(0) - 1
    n_used = nu_ref[0]
    slot = i % 2
    mb = MOE_BLOCK
    de = wgc.shape[1]
    dp = yrow.shape[1]
    spare0 = y_hbm.shape[0] - 2 * mb
    per = mb // (EXPERT_BURSTS // 2)

    def gather_row(idx_ref, r, dst_slot, priority=0):
        pltpu.make_async_copy(h_hbm.at[pl.ds(idx_ref[0, 0, r], 1)], xbuf.at[dst_slot, pl.ds(r, 1)],
                              sem_g.at[dst_slot]).start(priority=priority)

    def gather_burst(b):
        for r in range(b * per, (b + 1) * per):
            gather_row(tokn_ref, r, 1 - slot, r % 2)

    def scatter_burst(b):
        for r in range(b * per, (b + 1) * per):
            p = jnp.where(i == 0, spare0 + r, posp_ref[0, 0, r])
            pltpu.make_async_copy(ybuf.at[1 - slot, pl.ds(r, 1)], y_hbm.at[pl.ds(p, 1)],
                                  sem_s.at[slot]).start(priority=r % 2)

    def wait_scatter(parity):
        pltpu.make_async_copy(ybuf.at[0], y_hbm.at[pl.ds(0, mb)], sem_s.at[parity]).wait()

    @pl.when(i == 0)
    def _():
        ybuf[1] = jnp.zeros(ybuf.shape[1:], u32)

        def body(r, carry):
            gather_row(tok0_ref, r, 0)
            pltpu.make_async_copy(ybuf.at[1, pl.ds(r, 1)], y_hbm.at[pl.ds(spare0 + mb + r, 1)], sem_s.at[1]).start()
            return carry
        lax.fori_loop(0, mb, body, 0)

    @pl.when(i <= n_used)
    def _():
        pltpu.make_async_copy(h_hbm.at[pl.ds(0, mb)], xbuf.at[slot], sem_g.at[slot]).wait()

    ib = jnp.minimum(i, nblk - 1)
    changed = (i == 0) | (be_ref[ib] != be_ref[jnp.maximum(ib - 1, 0)])

    @pl.when(changed & (i < n_used))
    def _():
        wgc[...] = wg_ref[...].astype(bf16)
        wuc[...] = wu_ref[...].astype(bf16)
        wdc[...] = wd_ref[...].astype(bf16)

    @pl.when(i < n_used)
    def _():
        lo, hi = _unpack_halves(_tiles_to_rows(xbuf[slot]))
        x = jnp.concatenate([lo.astype(bf16), hi.astype(bf16)], axis=1)
        nh = EXPERT_BURSTS // 4
        hw = de // nh
        hids = []
        for c in range(nh):
            gate = _dot(x, wgc[:, c * hw:(c + 1) * hw])
            gather_burst(2 * c)
            up = _dot(x, wuc[:, c * hw:(c + 1) * hw])
            gather_burst(2 * c + 1)
            hids.append((_silu(gate) * up).astype(bf16))
        hid = jnp.concatenate(hids, axis=1)
        nd = EXPERT_BURSTS // 2
        dw = dp // nd
        for c in range(nd):
            wd_c = jnp.concatenate([wdc[:, c * dw:(c + 1) * dw], wdc[:, dp + c * dw:dp + (c + 1) * dw]], axis=1)
            yrow[:, c * dw:(c + 1) * dw] = _pack_halves(_dot(hid, wd_c))
            scatter_burst(c)
        wait_scatter(1 - slot)
        ybuf[slot] = _rows_to_tiles(yrow[...])

    @pl.when(i == n_used)
    def _():
        wait_scatter(1 - slot)
        for b in range(EXPERT_BURSTS // 2):
            scatter_burst(b)
        wait_scatter(slot)


def _experts(h, slot_tok, slot_pos, block_e, n_used, wg, wu, wd, layer, n_rows):
    tile = h.shape[1:]
    dp = tile[0] * tile[1]
    d = wg.shape[2]
    nblk = slot_tok.shape[0]
    de = wg.shape[3]
    mb = MOE_BLOCK
    last = nblk - 1
    smem = lambda f: pl.BlockSpec((1, 1, mb), f, memory_space=pltpu.SMEM)
    wspec = lambda shape: pl.BlockSpec((None,) + shape, lambda i, be, nu: (layer, be[jnp.minimum(i, last)], 0, 0))
    grid_spec = pltpu.PrefetchScalarGridSpec(
        num_scalar_prefetch=2,
        grid=(nblk + 1,),
        in_specs=[smem(lambda i, be, nu: (0, 0, 0)),
                  smem(lambda i, be, nu: (jnp.minimum(i + 1, last), 0, 0)),
                  smem(lambda i, be, nu: (jnp.clip(i - 1, 0, last), 0, 0)),
                  pl.BlockSpec(memory_space=pl.ANY),
                  wspec((None, d, de)), wspec((None, d, de)), wspec((None, de, d))],
        out_specs=pl.BlockSpec(memory_space=pl.ANY),
        scratch_shapes=[pltpu.VMEM((2, mb) + tile, u32), pltpu.VMEM((2, mb) + tile, u32), pltpu.VMEM((mb, dp), u32),
                        pltpu.VMEM((d, de), bf16), pltpu.VMEM((d, de), bf16), pltpu.VMEM((de, d), bf16),
                        pltpu.SemaphoreType.DMA((2,)), pltpu.SemaphoreType.DMA((2,))],
    )
    st = slot_tok.reshape(nblk, 1, mb)
    sp = slot_pos.reshape(nblk, 1, mb)
    return pl.pallas_call(
        _expert_kernel,
        grid_spec=grid_spec,
        out_shape=jax.ShapeDtypeStruct((n_rows * TOP_K + 2 * mb,) + tile, u32),
        compiler_params=_params(("arbitrary",)),
    )(block_e, n_used, st, st, sp, h, wg, wu, wd)


COMBINE_ROWS = 128


def _combine_kernel(*refs, final_norm):
    y_refs = refs[:TOP_K]
    w_ref, h_ref, sg_ref, su_ref, sd_ref, x_ref, gate_ref, gf_ref, o_ref = refs[TOP_K:]
    lo, hi = _unpack_halves(_tiles_to_rows(h_ref[...]))
    hb = jnp.concatenate([lo.astype(bf16), hi.astype(bf16)], axis=1)
    hid = _silu(_dot(hb, sg_ref[...])) * _dot(hb, su_ref[...])
    acc = _dot(hid.astype(bf16), sd_ref[...])
    shape = y_refs[0].shape
    acc_lo = acc_hi = 0.0
    for k in range(TOP_K):
        lo, hi = _unpack_halves(y_refs[k][...])
        wk = jnp.broadcast_to(w_ref[:, k:k + 1, :], shape)
        acc_lo = acc_lo + lo * wk
        acc_hi = acc_hi + hi * wk
    acc = acc + jnp.concatenate([_tiles_to_rows(acc_lo), _tiles_to_rows(acc_hi)], axis=1)
    out = x_ref[...] + gate_ref[...] * acc
    if final_norm:
        out = out * lax.rsqrt(jnp.mean(out * out, axis=-1, keepdims=True) + EPS) * gf_ref[...]
    o_ref[...] = out


def _combine(y_rows, wts, h, sg, su, sd, x, mod, which, g_final, dims, n_rows, final_norm):
    d = x.shape[1]
    de = sg.shape[1]
    tile = h.shape[1:]
    tm = COMBINE_ROWS
    per = ROW_BLOCK // tm
    nsteps = n_rows // tm
    nlb, bpb, nb = dims
    mrow = lambda i: _mod_row(i // per, nlb, bpb, nb)
    yspec = lambda k: pl.BlockSpec((tm,) + tile, lambda i: (k * nsteps + i, 0, 0))
    w3 = jnp.broadcast_to(wts.T[:, :, None], (n_rows, TOP_K, LANE))
    return pl.pallas_call(
        functools.partial(_combine_kernel, final_norm=final_norm),
        grid=(nsteps,),
        in_specs=[yspec(k) for k in range(TOP_K)] + [
                  pl.BlockSpec((tm, TOP_K, LANE), lambda i: (i, 0, 0)),
                  pl.BlockSpec((tm,) + tile, lambda i: (i, 0, 0)),
                  pl.BlockSpec((d, de), lambda i: (0, 0)),
                  pl.BlockSpec((d, de), lambda i: (0, 0)),
                  pl.BlockSpec((de, d), lambda i: (0, 0)),
                  pl.BlockSpec((tm, d), lambda i: (i, 0)),
                  pl.BlockSpec((None, 1, d), lambda i: (mrow(i) * 6 + which, 0, 0)),
                  pl.BlockSpec((1, d), lambda i: (0, 0))],
        out_specs=pl.BlockSpec((tm, d), lambda i: (i, 0)),
        out_shape=jax.ShapeDtypeStruct((n_rows, d), f32),
        compiler_params=_params(("parallel",)),
    )(*([y_rows] * TOP_K), w3, h, sg, su, sd, x, mod, g_final.reshape(1, d))


def _moe(x, g, mod, which_shift, w_router, b_router, wg, wu, wd, layer, sg, su, sd, g_final, dims, n_rows,
         final_norm):
    ne = N_EXPERTS
    mb = MOE_BLOCK
    h, idx, wts, rank, cnt = _router(x, g, mod, which_shift, w_router, b_router, dims, n_rows)
    counts = cnt[:, 0].astype(i32)
    padded = (counts + mb - 1) // mb * mb
    pend = jnp.cumsum(padded)
    start = pend - padded
    onehot = idx[:, :, None] == jnp.arange(ne, dtype=i32)[None, None, :]
    dest = jnp.sum(jnp.where(onehot, start[None, None, :], 0), axis=-1) + rank
    n_assign = n_rows * TOP_K
    nblk = (n_assign + ne * (mb - 1) + mb - 1) // mb
    n_pad = nblk * mb - n_assign
    pcnt = padded - counts
    pcum = jnp.cumsum(pcnt)
    j = jnp.arange(n_pad, dtype=i32)
    e_j = jnp.sum((pcum[None, :] <= j[:, None]).astype(i32), axis=1)
    base = start + counts - (pcum - pcnt)
    base_j = jnp.sum(jnp.where(e_j[:, None] == jnp.arange(ne, dtype=i32)[None, :], base[None, :], 0), axis=1)
    pad_slot = jnp.where(e_j < ne, base_j + j, pend[-1] + j - pcum[-1])
    keys = jnp.concatenate([dest.reshape(-1), pad_slot])
    vals = jnp.concatenate([jnp.arange(n_assign, dtype=i32), n_assign + pad_slot % mb])
    _, slot_pos = lax.sort((keys, vals), num_keys=1)
    slot_tok = jnp.where(slot_pos < n_assign, slot_pos % n_rows, 0)
    blk_start = jnp.arange(nblk, dtype=i32) * mb
    block_e = jnp.minimum(jnp.sum((pend[None, :] <= blk_start[:, None]).astype(i32), axis=1), ne - 1)
    n_used = (pend[-1:] // mb).astype(i32)
    y_rows = _experts(h, slot_tok.reshape(nblk, mb), slot_pos.reshape(nblk, mb), block_e, n_used, wg, wu, wd,
                      layer, n_rows)
    return _combine(y_rows, wts, h, sg, su, sd, x, mod, which_shift + 2, g_final, dims, n_rows, final_norm)


def _rope_tables(seq, head_dim):
    half = head_dim // 2
    quarter = half // 2
    t = jnp.arange(seq, dtype=i32)
    rows = (t // GRID_W).astype(f32)
    cols = (t % GRID_W).astype(f32)
    freqs = ROPE_THETA ** (-jnp.arange(0, half, 2, dtype=f32) / half)
    ar = rows[:, None] * freqs[None, :]
    ac = cols[:, None] * freqs[None, :]
    cos = jnp.concatenate([jnp.cos(ar), jnp.cos(ar), jnp.cos(ac), jnp.cos(ac)], axis=1)
    sin = jnp.concatenate([-jnp.sin(ar), jnp.sin(ar), -jnp.sin(ac), jnp.sin(ac)], axis=1)
    reps = LANE // head_dim
    cos = jnp.tile(cos, (1, reps))
    sin = jnp.tile(sin, (1, reps))
    cos = jnp.concatenate([cos, jnp.ones((ROW_BLOCK, LANE), f32)], axis=0)
    sin = jnp.concatenate([sin, jnp.zeros((ROW_BLOCK, LANE), f32)], axis=0)
    return cos, sin, quarter


def kernel(x, c, ctx, c_ctx, w_ada, b_ada, g_norm_mix, g_norm_ffn, w_in_even, w_out_even, hgrn_lb_logits,
           g_hgrn_norm, win_sink, w_in_odd, w_out_odd, g_q_norm, g_k_norm, w_router, b_router,
           w_exp_gate, w_exp_up, w_exp_down, w_sh_gate, w_sh_up, w_sh_down, g_norm_final):
    nb, seq, d = x.shape
    lc = ctx.shape[1]
    assert seq % ROW_BLOCK == 0 and lc % ROW_BLOCK == 0 and seq % lc == 0
    t_lat = nb * seq
    t_all = t_lat + nb * lc
    bpb = seq // ROW_BLOCK
    nlb = t_lat // ROW_BLOCK
    dims = (nlb, bpb, nb)

    xs = jnp.concatenate([x.reshape(t_lat, d), ctx.reshape(nb * lc, d)], axis=0)
    cc = jnp.concatenate([c, c_ctx[None, :], jnp.zeros((8 - nb - 1, d), f32)], axis=0)

    lb_w = jax.nn.softmax(hgrn_lb_logits.astype(f32), axis=0)
    lower_bounds = jnp.cumsum(lb_w, axis=0)[1:] - lb_w[0]

    mod = _ada(cc, w_ada, b_ada, 0).reshape(8 * 6, 1, d)
    w_in = w_in_even[0].astype(bf16)
    proj = _normmm(xs, g_norm_mix[0], mod, 0, w_in[:, :HGRN_COLS], f32, dims, tn=1280)
    cos, sin, quarter = _rope_tables(seq, WIN_HEAD_DIM)
    epi = dict(blocks={0: (WIN_HEADS + WIN_KV_HEADS) * WIN_HEAD_DIM // LANE}, quarter=quarter, head_norm=False,
               q_slices=WIN_HEADS * WIN_HEAD_DIM // LANE, q_scale=WIN_HEAD_DIM ** -0.5)
    wproj = _normmm(xs, g_norm_mix[0], mod, 0, w_in[:, HGRN_COLS:], bf16, dims, tn=WIN_COLS,
                    epilogue=epi, tables=(cos, sin))
    o_f, o_b = _hgrn(proj, lower_bounds[0], nb, seq, lc)
    a_mix = _hgrn_out(o_f, o_b, proj, g_hgrn_norm[0])
    b_mix = _win_attn(wproj, win_sink[0], nb, seq, lc)
    xs = _outproj(a_mix, 0, b_mix, 0, w_out_even[0].astype(bf16), xs, mod, 2, dims, t_all)
    xs = _moe(xs, g_norm_ffn[0], mod, 3, w_router[0], b_router[0],
              w_exp_gate, w_exp_up, w_exp_down, 0,
              w_sh_gate[0].astype(bf16), w_sh_up[0].astype(bf16), w_sh_down[0].astype(bf16),
              g_norm_final, dims, t_all, False)

    mod = _ada(cc, w_ada, b_ada, 1).reshape(8 * 6, 1, d)
    cos, sin, quarter = _rope_tables(seq, GLB_HEAD_DIM)
    tn = 1536
    n_sl = tn // LANE
    rope_slices = GLB_HEADS + GLB_KV_HEADS
    epi = dict(blocks={0: min(rope_slices, n_sl), 1: max(rope_slices - n_sl, 0)}, quarter=quarter, head_norm=True,
               q_slices=GLB_HEADS, q_scale=GLB_HEAD_DIM ** -0.5)
    qkv = _normmm(xs, g_norm_mix[1], mod, 0, w_in_odd[0].astype(bf16), bf16, dims, tn=tn,
                  epilogue=epi, tables=(cos, sin), gains=(g_q_norm[0], g_k_norm[0]))
    att = _glb_attn(qkv, nb, seq, lc)
    xl = _outproj(att, 0, att, 1, w_out_odd[0].astype(bf16), xs, mod, 2, dims, t_lat)
    out = _moe(xl, g_norm_ffn[1], mod, 3, w_router[1], b_router[1],
               w_exp_gate, w_exp_up, w_exp_down, 1,
               w_sh_gate[1].astype(bf16), w_sh_up[1].astype(bf16), w_sh_down[1].astype(bf16),
               g_norm_final, dims, t_lat, True)
    return out.reshape(nb, seq, d)
```

```python
import functools

import jax
import jax.numpy as jnp
from jax import lax
from jax.experimental import pallas as pl
from jax.experimental.pallas import tpu as pltpu

f32 = jnp.float32
bf16 = jnp.bfloat16
i32 = jnp.int32
u32 = jnp.uint32

EPS = 1e-6
LOG2E = 1.4426950408889634
ROPE_THETA = 10000.0
GRID_W = 64

HGRN_DK = 128
HGRN_HEADS = 8
HGRN_CHUNK = 64
HGRN_SUB = 16
HGRN_HEADS_PER_STEP = 8
HGRN_SAFE_DECAY = 80.0
WIN_HEAD_DIM = 64
WIN_HEADS = 16
WIN_KV_HEADS = 2
WIN_BLOCK = 128
GLB_HEAD_DIM = 128
GLB_HEADS = 16
GLB_KV_HEADS = 4
Q_BLOCK = 256
N_EXPERTS = 64
N_GROUPS = 8
TOPK_GROUPS = 4
TOP_K = 8
ROUTED_SCALE = 2.5
MOE_BLOCK = 256

LANE = 128
ROW_BLOCK = 256
VMEM_LIMIT = 56 * 1024 * 1024

A_K = HGRN_HEADS * HGRN_DK
HGRN_COLS = 5 * A_K
WIN_COLS = WIN_HEADS * WIN_HEAD_DIM + 2 * WIN_KV_HEADS * WIN_HEAD_DIM


def _params(sem, vmem=VMEM_LIMIT):
    return pltpu.CompilerParams(dimension_semantics=sem, vmem_limit_bytes=vmem)


def _silu(x):
    return x * jax.nn.sigmoid(x)


def _dot(a, b):
    return jnp.dot(a, b, preferred_element_type=f32)


def _dot_nt(a, b):
    return lax.dot_general(a, b, (((1,), (1,)), ((), ())), preferred_element_type=f32)


def _pack_halves(x):
    n = x.shape[1] // 2
    lo = lax.bitcast_convert_type(x[:, :n].astype(bf16).astype(f32), u32)
    hi = lax.bitcast_convert_type(x[:, n:].astype(bf16).astype(f32), u32)
    return (lo >> 16) | hi


def _unpack_halves(w):
    lo = lax.bitcast_convert_type(w << 16, f32)
    hi = lax.bitcast_convert_type(w & jnp.uint32(0xFFFF0000), f32)
    return lo, hi


def _rows_to_tiles(w):
    n = w.shape[1] // LANE
    return pltpu.einshape("stl->tsl", jnp.stack([w[:, s * LANE:(s + 1) * LANE] for s in range(n)], axis=0))


def _tiles_to_rows(t):
    y = pltpu.einshape("tsl->stl", t)
    return jnp.concatenate([y[s] for s in range(t.shape[1])], axis=1)


def _dot_tn(a, b):
    return lax.dot_general(a, b, (((0,), (0,)), ((), ())), preferred_element_type=f32)


def _ada_kernel(c_ref, w_ref, b_ref, o_ref):
    a = _silu(c_ref[...]).astype(bf16)
    o_ref[...] = _dot(a, w_ref[...].astype(bf16)) + b_ref[...]


def _ada(c8, w, b, layer, tn=768):
    m, d = c8.shape
    n = w.shape[2]
    return pl.pallas_call(
        _ada_kernel,
        grid=(n // tn,),
        in_specs=[pl.BlockSpec((m, d), lambda j: (0, 0)),
                  pl.BlockSpec((None, d, tn), lambda j: (layer, 0, j)),
                  pl.BlockSpec((None, 1, tn), lambda j: (layer, 0, j))],
        out_specs=pl.BlockSpec((m, tn), lambda j: (0, j)),
        out_shape=jax.ShapeDtypeStruct((m, n), f32),
        compiler_params=_params(("arbitrary",)),
    )(c8, w, b.reshape(b.shape[0], 1, n))


def _norm_mod(x, g, sc, sh):
    y = x * lax.rsqrt(jnp.mean(x * x, axis=-1, keepdims=True) + EPS) * g
    return y * (1.0 + sc) + sh


def _rope_slice(x, cos, sin, quarter):
    lane = lax.broadcasted_iota(i32, x.shape, 1)
    up = pltpu.roll(x, LANE - quarter, axis=1)
    dn = pltpu.roll(x, quarter, axis=1)
    partner = jnp.where(lane % (2 * quarter) < quarter, up, dn)
    return x * cos + partner * sin


def _normmm_kernel(x_ref, g_ref, sc_ref, sh_ref, w_ref, *rest, epilogue, tn):
    if epilogue is None:
        (o_ref,) = rest
    elif epilogue["head_norm"]:
        cos_ref, sin_ref, gq_ref, gk_ref, o_ref = rest
    else:
        cos_ref, sin_ref, o_ref = rest
    j = pl.program_id(0)
    h = _norm_mod(x_ref[...], g_ref[...], sc_ref[...], sh_ref[...]).astype(bf16)
    y = _dot(h, w_ref[...])
    if epilogue is None:
        o_ref[...] = y.astype(o_ref.dtype)
        return

    n_sl = tn // LANE
    for jb in range(epilogue["n_col_blocks"]):
        n_rope = epilogue["blocks"].get(jb, 0)

        @pl.when(j == jb)
        def _(jb=jb, n_rope=n_rope):
            cos = cos_ref[...]
            sin = sin_ref[...]
            outs = []
            for s in range(n_sl):
                ys = y[:, s * LANE:(s + 1) * LANE]
                if s < n_rope:
                    col = jb * n_sl + s
                    is_q = col < epilogue["q_slices"]
                    if epilogue["head_norm"]:
                        gain = gq_ref[...] if is_q else gk_ref[...]
                        ys = ys * lax.rsqrt(jnp.mean(ys * ys, axis=-1, keepdims=True) + EPS) * gain
                    ys = _rope_slice(ys, cos, sin, epilogue["quarter"])
                    if is_q:
                        ys = ys * epilogue["q_scale"]
                outs.append(ys.astype(o_ref.dtype))
            o_ref[...] = jnp.concatenate(outs, axis=1)


def _mod_row(i, n_lat_blocks, blocks_per_batch, n_batch):
    return jnp.where(i < n_lat_blocks, i // blocks_per_batch, n_batch)


def _normmm(x, g, mod, which, w, out_dtype, dims, tn, epilogue=None, tables=None, gains=None):
    t, d = x.shape
    n = w.shape[1]
    tm = ROW_BLOCK
    nlb, bpb, nb = dims
    mrow = lambda i: _mod_row(i, nlb, bpb, nb)
    in_specs = [pl.BlockSpec((tm, d), lambda j, i: (i, 0)),
                pl.BlockSpec((1, d), lambda j, i: (0, 0)),
                pl.BlockSpec((None, 1, d), lambda j, i: (mrow(i) * 6 + which + 1, 0, 0)),
                pl.BlockSpec((None, 1, d), lambda j, i: (mrow(i) * 6 + which, 0, 0)),
                pl.BlockSpec((d, tn), lambda j, i: (0, j))]
    args = [x, g.reshape(1, d), mod, mod, w]
    if epilogue is not None:
        cos, sin = tables
        tab = lambda j, i: (jnp.where(i < nlb, i % bpb, bpb), 0)
        in_specs += [pl.BlockSpec((tm, LANE), tab), pl.BlockSpec((tm, LANE), tab)]
        args += [cos, sin]
        epilogue = dict(epilogue, n_col_blocks=n // tn)
        if epilogue["head_norm"]:
            in_specs += [pl.BlockSpec((1, LANE), lambda j, i: (0, 0))] * 2
            args += [gains[0].reshape(1, LANE), gains[1].reshape(1, LANE)]
    return pl.pallas_call(
        functools.partial(_normmm_kernel, epilogue=epilogue, tn=tn),
        grid=(n // tn, t // tm),
        in_specs=in_specs,
        out_specs=pl.BlockSpec((tm, tn), lambda j, i: (i, j)),
        out_shape=jax.ShapeDtypeStruct((t, n), out_dtype),
        compiler_params=_params(("parallel", "parallel")),
    )(*args)


def _gla_gates(q_raw, f_raw, lb, rev):
    c = HGRN_CHUNK
    sub = HGRN_SUB
    q = _silu(q_raw)
    f = lb + (1.0 - lb) * jax.nn.sigmoid(f_raw)
    kk = 1.0 - f
    g = jnp.log(f)
    row = lax.broadcasted_iota(i32, (c, c), 0)
    col = lax.broadcasted_iota(i32, (c, c), 1)
    tri = (col >= row) if rev else (col <= row)
    b = jnp.dot(tri.astype(f32), g, preferred_element_type=f32, precision=lax.Precision.HIGHEST)
    span = None
    for jb in range(c // sub):
        d = jnp.abs(b[jb * sub:jb * sub + 1] - b[(jb + 1) * sub - 1:(jb + 1) * sub])
        span = d if span is None else jnp.maximum(span, d)
    return q, kk, b, span


def _gla_chunks_factored(chains):
    c = HGRN_CHUNK
    sub = HGRN_SUB
    nsub = c // sub
    rrow = lax.broadcasted_iota(i32, (c, sub), 0)
    rcol = lax.broadcasted_iota(i32, (c, sub), 1)
    work = []
    for (q, kk, b), v_ref, st_ref, _, o_ref, sl, rev in chains:
        st = st_ref[...]
        o = _dot_nt((q * jnp.exp(b)).astype(bf16), st.astype(bf16))
        cols = []
        for jb in range(nsub):
            js = slice(jb * sub, (jb + 1) * sub)
            m_j = b[jb * sub:jb * sub + 1] if rev else b[(jb + 1) * sub - 1:(jb + 1) * sub]
            kd = kk[js] * jnp.exp(m_j - b[js])
            qd = q * jnp.exp(jnp.minimum(b - m_j, HGRN_SAFE_DECAY))
            cols.append(_dot_nt(qd.astype(bf16), kd.astype(bf16)))
        work.append((o, cols, st))
    outs = []
    for ((q, kk, b), v_ref, st_ref, _, o_ref, sl, rev), (o, cols, st) in zip(chains, work):
        vb = v_ref[:, sl].astype(bf16)
        for jb in range(nsub):
            keep = (rrow <= jb * sub + rcol) if rev else (rrow >= jb * sub + rcol)
            a = jnp.where(keep, cols[jb], 0.0).astype(bf16)
            o = o + _dot(a, vb[jb * sub:(jb + 1) * sub])
        b_end = b[0:1] if rev else b[c - 1:c]
        kdec = kk * jnp.exp(b_end - b)
        outs.append((o, jnp.exp(b_end) * st + _dot_tn(vb, kdec.astype(bf16))))
    for (_, v_ref, st_ref, _, o_ref, sl, rev), (o, st_new) in zip(chains, outs):
        o_ref[:, sl] = o
        st_ref[...] = st_new


def _gla_chunk(q, kk, b, v, st_ref, a_scr, rev):
    c = HGRN_CHUNK
    sub = HGRN_SUB
    b_end = b[0:1] if rev else b[c - 1:c]
    st = st_ref[...]
    o = _dot_nt((q * jnp.exp(b)).astype(bf16), st.astype(bf16))

    vb = v.astype(bf16)
    nsub = c // sub
    lane16 = lax.broadcasted_iota(i32, (sub, sub), 1)
    row16 = lax.broadcasted_iota(i32, (sub, sub), 0)
    diag_ok = (row16 <= lane16) if rev else (row16 >= lane16)
    rsub = lax.broadcasted_iota(i32, (c, sub), 0) // sub
    for jb in range(nsub):
        js = slice(jb * sub, (jb + 1) * sub)
        m_j = b[jb * sub:jb * sub + 1] if rev else b[(jb + 1) * sub - 1:(jb + 1) * sub]
        kd = kk[js] * jnp.exp(m_j - b[js])
        qd = q * jnp.exp(jnp.minimum(b - m_j, 0.0))
        a_col = _dot_nt(qd.astype(bf16), kd.astype(bf16))
        off_ok = (rsub < jb) if rev else (rsub > jb)
        a_col = jnp.where(off_ok, a_col, 0.0)
        qi, bi, ki = q[js], b[js] * LOG2E, kk[js]
        a_dd = jnp.zeros((sub, sub), f32)
        for s in range(sub):
            e = jnp.exp2(bi - bi[s:s + 1])
            col_s = jnp.sum(qi * ki[s:s + 1] * e, axis=-1, keepdims=True)
            a_dd = jnp.where(lane16 == s, col_s, a_dd)
        a_dd = jnp.where(diag_ok, a_dd, 0.0)
        pieces = [a_dd if ib == jb else a_col[ib * sub:(ib + 1) * sub] for ib in range(nsub)]
        a_scr[:, js] = jnp.concatenate(pieces, axis=0)
    o = o + _dot(a_scr[...].astype(bf16), vb)
    kdec = kk * jnp.exp(b_end - b)
    st_ref[...] = jnp.exp(b_end) * st + _dot_tn(vb, kdec.astype(bf16))
    return o


def _hgrn_kernel(qf_ref, vf_ref, ff_ref, qb_ref, vb_ref, fb_ref, lb_ref, of_ref, ob_ref,
                 stf_ref, stb_ref, af_scr, ab_scr):
    @pl.when(pl.program_id(2) == 0)
    def _():
        stf_ref[...] = jnp.zeros_like(stf_ref)
        stb_ref[...] = jnp.zeros_like(stb_ref)

    gates = []
    span = None
    for j in range(HGRN_HEADS_PER_STEP):
        sl = slice(j * HGRN_DK, (j + 1) * HGRN_DK)
        lb = lb_ref[:, sl]
        gf = _gla_gates(qf_ref[:, sl], ff_ref[:, sl], lb, False)
        gb = _gla_gates(qb_ref[:, sl], fb_ref[:, sl], lb, True)
        gates.append((gf[:3], gb[:3]))
        for d in (gf[3], gb[3]):
            span = d if span is None else jnp.maximum(span, d)
    mild = jnp.max(span) <= HGRN_SAFE_DECAY

    chains = []
    for j in range(HGRN_HEADS_PER_STEP):
        sl = slice(j * HGRN_DK, (j + 1) * HGRN_DK)
        chains.append((gates[j][0], vf_ref, stf_ref.at[j], af_scr.at[j], of_ref, sl, False))
        chains.append((gates[j][1], vb_ref, stb_ref.at[j], ab_scr.at[j], ob_ref, sl, True))

    @pl.when(mild)
    def _():
        _gla_chunks_factored(chains)

    @pl.when(jnp.logical_not(mild))
    def _():
        for (q, kk, b), v_ref, st_ref, a_scr, o_ref, sl, rev in chains:
            o_ref[:, sl] = _gla_chunk(q, kk, b, v_ref[:, sl], st_ref, a_scr, rev)


def _hgrn(proj, lb, n_batch, seq, ctx_len):
    t = proj.shape[0]
    c = HGRN_CHUNK
    hp = HGRN_HEADS_PER_STEP
    wblk = hp * HGRN_DK
    ngrp = HGRN_HEADS // hp
    ncc, ncl = ctx_len // c, seq // c
    lat_chunks = n_batch * ncl

    def fwd_blk(b, s):
        return jnp.where(s < ncc, lat_chunks + b * ncc + s, b * ncl + (s - ncc))

    def bwd_blk(b, s):
        return jnp.where(s < ncc, lat_chunks + b * ncc + (ncc - 1 - s), b * ncl + (ncl - 1 - (s - ncc)))

    def spec(blk, section):
        return pl.BlockSpec((c, wblk), lambda b, h, s: (blk(b, s), section * ngrp + h))

    def ospec(blk):
        return pl.BlockSpec((c, wblk), lambda b, h, s: (blk(b, s), h))

    out = jax.ShapeDtypeStruct((t, A_K), f32)
    return pl.pallas_call(
        _hgrn_kernel,
        grid=(n_batch, ngrp, ncc + ncl),
        in_specs=[spec(fwd_blk, 0), spec(fwd_blk, 1), spec(fwd_blk, 2),
                  spec(bwd_blk, 0), spec(bwd_blk, 1), spec(bwd_blk, 3),
                  pl.BlockSpec((1, wblk), lambda b, h, s: (0, h))],
        out_specs=[ospec(fwd_blk), ospec(bwd_blk)],
        out_shape=[out, out],
        scratch_shapes=[pltpu.VMEM((hp, HGRN_DK, HGRN_DK), f32), pltpu.VMEM((hp, HGRN_DK, HGRN_DK), f32),
                        pltpu.VMEM((hp, c, c), f32), pltpu.VMEM((hp, c, c), f32)],
        compiler_params=_params(("parallel", "parallel", "arbitrary")),
    )(proj, proj, proj, proj, proj, proj, lb.reshape(1, A_K))


def _hgrn_out_kernel(of_ref, ob_ref, g_ref, gain_ref, o_ref):
    gain = gain_ref[...]
    outs = []
    for h in range(HGRN_HEADS):
        sl = slice(h * HGRN_DK, (h + 1) * HGRN_DK)
        o = of_ref[:, sl] + ob_ref[:, sl]
        o = o * lax.rsqrt(jnp.mean(o * o, axis=-1, keepdims=True) + EPS) * gain
        outs.append((o * _silu(g_ref[:, sl])).astype(bf16))
    o_ref[...] = jnp.concatenate(outs, axis=1)


def _hgrn_out(o_f, o_b, proj, gain):
    t = o_f.shape[0]
    tm = ROW_BLOCK
    return pl.pallas_call(
        _hgrn_out_kernel,
        grid=(t // tm,),
        in_specs=[pl.BlockSpec((tm, A_K), lambda i: (i, 0)),
                  pl.BlockSpec((tm, A_K), lambda i: (i, 0)),
                  pl.BlockSpec((tm, A_K), lambda i: (i, 4)),
                  pl.BlockSpec((1, HGRN_DK), lambda i: (0, 0))],
        out_specs=pl.BlockSpec((tm, A_K), lambda i: (i, 0)),
        out_shape=jax.ShapeDtypeStruct((t, A_K), bf16),
        compiler_params=_params(("parallel",)),
    )(o_f, o_b, proj, gain.reshape(1, HGRN_DK))


def _win_kernel(sink_ref, q_ref, kp_ref, kc_ref, kn_ref, vp_ref, vc_ref, vn_ref, kx_ref, vx_ref, o_ref,
                *, n_lat_blocks):
    n = pl.program_id(1)
    w = WIN_BLOCK
    dh = WIN_HEAD_DIM
    is_lat = n < n_lat_blocks
    ri = lax.broadcasted_iota(i32, (w, w), 0)
    ci = lax.broadcasted_iota(i32, (w, w), 1)
    ok_p = (ci >= ri) & is_lat & (n > 0)
    ok_c = jnp.broadcast_to(is_lat, (w, w))
    ok_n = (ci <= ri) & is_lat & (n < n_lat_blocks - 1)
    valid = jnp.concatenate([ok_p, ok_c, ok_n], axis=1)
    kwin = jnp.concatenate([kp_ref[...], kc_ref[...], kn_ref[...]], axis=0)
    vwin = jnp.concatenate([vp_ref[...], vc_ref[...], vn_ref[...]], axis=0)
    kx = kx_ref[...]
    vx = vx_ref[...]
    g = WIN_HEADS // WIN_KV_HEADS
    scores = []
    for h in range(WIN_HEADS):
        ks = slice((h // g) * dh, (h // g + 1) * dh)
        qh = q_ref[:, h * dh:(h + 1) * dh]
        scores.append((jnp.where(valid, _dot_nt(qh, kwin[:, ks]), -jnp.inf), _dot_nt(qh, kx[:, ks])))
    probs = []
    for h, (sw, sx) in enumerate(scores):
        sk = sink_ref[h]
        m = jnp.maximum(jnp.maximum(jnp.max(sw, axis=-1, keepdims=True), jnp.max(sx, axis=-1, keepdims=True)), sk)
        pw = jnp.exp(sw - m)
        px = jnp.exp(sx - m)
        den = jnp.sum(pw, axis=-1, keepdims=True) + jnp.sum(px, axis=-1, keepdims=True) + jnp.exp(sk - m)
        probs.append((pw.astype(bf16), px.astype(bf16), den))
    outs = []
    for h, (pw, px, den) in enumerate(probs):
        ks = slice((h // g) * dh, (h // g + 1) * dh)
        o = _dot(pw, vwin[:, ks]) + _dot(px, vx[:, ks])
        outs.append((o / den).astype(bf16))
    o_ref[...] = jnp.concatenate(outs, axis=1)


def _win_attn(wproj, sink, n_batch, seq, ctx_len):
    t = wproj.shape[0]
    w = WIN_BLOCK
    nlb = seq // w
    ncb = ctx_len // w
    lat_blocks = n_batch * nlb
    qc = WIN_HEADS * WIN_HEAD_DIM // LANE
    kcol, vcol = qc, qc + 1

    def qrow(b, n):
        return jnp.where(n < nlb, b * nlb + n, lat_blocks + b * ncb + (n - nlb))

    def krow(off):
        def f(b, n):
            return b * nlb + jnp.clip(n + off, 0, nlb - 1)
        return f

    def kspec(off, colblk):
        return pl.BlockSpec((w, LANE), lambda b, n: (krow(off)(b, n), colblk))

    def xspec(colblk):
        return pl.BlockSpec((ctx_len, LANE), lambda b, n: ((n_batch * seq) // ctx_len + b, colblk))

    return pl.pallas_call(
        functools.partial(_win_kernel, n_lat_blocks=nlb),
        grid=(n_batch, nlb + ncb),
        in_specs=[pl.BlockSpec(memory_space=pltpu.SMEM),
                  pl.BlockSpec((w, qc * LANE), lambda b, n: (qrow(b, n), 0)),
                  kspec(-1, kcol), kspec(0, kcol), kspec(1, kcol),
                  kspec(-1, vcol), kspec(0, vcol), kspec(1, vcol),
                  xspec(kcol), xspec(vcol)],
        out_specs=pl.BlockSpec((w, qc * LANE), lambda b, n: (qrow(b, n), 0)),
        out_shape=jax.ShapeDtypeStruct((t, qc * LANE), bf16),
        compiler_params=_params(("parallel", "arbitrary")),
    )(sink.astype(f32), wproj, wproj, wproj, wproj, wproj, wproj, wproj, wproj, wproj)


GLB_KEY_CHUNK = 256


def _glb_kernel(q_ref, kx_ref, kl_ref, vx_ref, vl_ref, o_ref, m_scr, l_scr, acc_scr):
    dh = GLB_HEAD_DIM
    g = GLB_HEADS // GLB_KV_HEADS
    ck = GLB_KEY_CHUNK
    tq = q_ref.shape[0]
    q = jnp.concatenate([q_ref[:, i * dh:(i + 1) * dh] for i in range(g)], axis=0)

    def update(k, v, first):
        s = _dot_nt(q, k)
        reps = s.shape[1] // LANE
        mx = jnp.broadcast_to(jnp.max(s, axis=-1, keepdims=True), (g * tq, LANE))
        if first:
            m_new = mx
            p = jnp.exp(s - jnp.concatenate([m_new] * reps, axis=1))
            l_scr[...] = jnp.broadcast_to(jnp.sum(p, axis=-1, keepdims=True), (g * tq, LANE))
            acc_scr[...] = _dot(p.astype(bf16), v)
        else:
            m_old = m_scr[...]
            m_new = jnp.maximum(m_old, mx)
            alpha = jnp.exp(m_old - m_new)
            p = jnp.exp(s - jnp.concatenate([m_new] * reps, axis=1))
            l_scr[...] = alpha * l_scr[...] + jnp.broadcast_to(jnp.sum(p, axis=-1, keepdims=True), (g * tq, LANE))
            acc_scr[...] = alpha * acc_scr[...] + _dot(p.astype(bf16), v)
        m_scr[...] = m_new

    chunks = [(kx_ref, vx_ref, c, min(ck, kx_ref.shape[0] - c)) for c in range(0, kx_ref.shape[0], ck)]
    chunks += [(kl_ref, vl_ref, c, min(ck, kl_ref.shape[0] - c)) for c in range(0, kl_ref.shape[0], ck)]
    for n, (kr, vr, c0, sz) in enumerate(chunks):
        update(kr[c0:c0 + sz, :], vr[c0:c0 + sz, :], n == 0)
    o = acc_scr[...] / l_scr[...]
    o_ref[...] = jnp.concatenate([o[i * tq:(i + 1) * tq] for i in range(g)], axis=1).astype(bf16)


def _glb_attn(qkv, n_batch, seq, ctx_len):
    tq = Q_BLOCK
    dh = GLB_HEAD_DIM
    g = GLB_HEADS // GLB_KV_HEADS
    nq = seq // tq
    kcol0 = GLB_HEADS
    vcol0 = GLB_HEADS + GLB_KV_HEADS
    ctx_blk0 = (n_batch * seq) // ctx_len
    return pl.pallas_call(
        _glb_kernel,
        grid=(n_batch, GLB_KV_HEADS, nq),
        in_specs=[pl.BlockSpec((tq, g * dh), lambda b, k, i: (b * nq + i, k)),
                  pl.BlockSpec((ctx_len, dh), lambda b, k, i: (ctx_blk0 + b, kcol0 + k)),
                  pl.BlockSpec((seq, dh), lambda b, k, i: (b, kcol0 + k)),
                  pl.BlockSpec((ctx_len, dh), lambda b, k, i: (ctx_blk0 + b, vcol0 + k)),
                  pl.BlockSpec((seq, dh), lambda b, k, i: (b, vcol0 + k))],
        out_specs=pl.BlockSpec((tq, g * dh), lambda b, k, i: (b * nq + i, k)),
        out_shape=jax.ShapeDtypeStruct((n_batch * seq, GLB_HEADS * dh), bf16),
        scratch_shapes=[pltpu.VMEM((g * tq, LANE), f32), pltpu.VMEM((g * tq, LANE), f32), pltpu.VMEM((g * tq, dh), f32)],
        compiler_params=_params(("parallel", "parallel", "arbitrary")),
    )(qkv, qkv, qkv, qkv, qkv)


def _outproj_kernel(a1_ref, a2_ref, w1_ref, w2_ref, x_ref, gate_ref, o_ref):
    y = _dot(a1_ref[...], w1_ref[...]) + _dot(a2_ref[...], w2_ref[...])
    o_ref[...] = x_ref[...] + gate_ref[...] * y


def _outproj(a1, c1, a2, c2, w, x, mod, which, dims, n_rows, tn=1024):
    d = x.shape[1]
    kh = w.shape[0] // 2
    tm = ROW_BLOCK
    nlb, bpb, nb = dims
    mrow = lambda i: _mod_row(i, nlb, bpb, nb)
    return pl.pallas_call(
        _outproj_kernel,
        grid=(d // tn, n_rows // tm),
        in_specs=[pl.BlockSpec((tm, kh), lambda j, i: (i, c1)),
                  pl.BlockSpec((tm, kh), lambda j, i: (i, c2)),
                  pl.BlockSpec((kh, tn), lambda j, i: (0, j)),
                  pl.BlockSpec((kh, tn), lambda j, i: (1, j)),
                  pl.BlockSpec((tm, tn), lambda j, i: (i, j)),
                  pl.BlockSpec((None, 1, tn), lambda j, i: (mrow(i) * 6 + which, 0, j))],
        out_specs=pl.BlockSpec((tm, tn), lambda j, i: (i, j)),
        out_shape=jax.ShapeDtypeStruct((n_rows, d), f32),
        compiler_params=_params(("parallel", "parallel")),
    )(a1, a2, w, w, x, mod)


def _router_kernel(x_ref, g_ref, sc_ref, sh_ref, wr_ref, br_ref, u_ref,
                   h_ref, idx_ref, wt_ref, rank_ref, cnt_ref, carry_scr):
    i = pl.program_id(0)
    tm = x_ref.shape[0]
    ne, ng, pg = N_EXPERTS, N_GROUPS, N_EXPERTS // N_GROUPS

    @pl.when(i == 0)
    def _():
        carry_scr[...] = jnp.zeros_like(carry_scr)

    h = _norm_mod(x_ref[...], g_ref[...], sc_ref[...], sh_ref[...])
    h_ref[...] = _rows_to_tiles(_pack_halves(h))
    logits = lax.dot_general(wr_ref[...], h, (((1,), (1,)), ((), ())), preferred_element_type=f32,
                             precision=lax.Precision.HIGHEST)
    s = jax.nn.sigmoid(logits)
    s3 = s.reshape(ng, pg, tm)
    b3 = (s + br_ref[...]).reshape(ng, pg, tm)
    gi = lax.broadcasted_iota(i32, (ng, pg, tm), 0)
    pi = lax.broadcasted_iota(i32, (ng, pg, tm), 1)
    neg = -jnp.inf
    m1 = jnp.max(b3, axis=1, keepdims=True)
    first = jnp.min(jnp.where(b3 == m1, pi, pg), axis=1, keepdims=True)
    m2 = jnp.max(jnp.where(pi == first, neg, b3), axis=1, keepdims=True)
    gs = m1 + m2
    g2 = lax.broadcasted_iota(i32, (ng, 1, tm), 0)
    gmask = jnp.zeros((ng, 1, tm), jnp.bool_)
    for _ in range(TOPK_GROUPS):
        m = jnp.max(gs, axis=0, keepdims=True)
        fi = jnp.min(jnp.where(gs == m, g2, ng), axis=0, keepdims=True)
        pick = g2 == fi
        gmask = gmask | pick
        gs = jnp.where(pick, neg, gs)
    cand = jnp.where(gmask, b3, neg)
    eid = gi * pg + pi
    sel = jnp.zeros((ng, pg, tm), jnp.bool_)
    picks, idxs, wts = [], [], []
    for _ in range(TOP_K):
        m = jnp.max(jnp.max(cand, axis=0, keepdims=True), axis=1, keepdims=True)
        fi = jnp.min(jnp.min(jnp.where(cand == m, eid, ne), axis=0, keepdims=True), axis=1, keepdims=True)
        pick = eid == fi
        picks.append(pick)
        idxs.append(fi.reshape(1, tm))
        wts.append(jnp.sum(jnp.sum(jnp.where(pick, s3, 0.0), axis=0, keepdims=True), axis=1,
                           keepdims=True).reshape(1, tm))
        sel = sel | pick
        cand = jnp.where(pick, neg, cand)
    wsum = wts[0]
    for k in range(1, TOP_K):
        wsum = wsum + wts[k]
    idx_ref[...] = jnp.concatenate(idxs, axis=0)
    wt_ref[...] = jnp.concatenate([wk / wsum * ROUTED_SCALE for wk in wts], axis=0)
    sel2 = jnp.where(sel, 1.0, 0.0).reshape(ne, tm)
    prefix = (_dot(sel2.astype(bf16), u_ref[...]) + carry_scr[...]).reshape(ng, pg, tm)
    ranks = [jnp.sum(jnp.sum(jnp.where(pk, prefix, 0.0), axis=0, keepdims=True), axis=1,
                     keepdims=True).reshape(1, tm) for pk in picks]
    rank_ref[...] = jnp.concatenate(ranks, axis=0).astype(i32)
    total = carry_scr[...] + jnp.sum(sel2, axis=1, keepdims=True)
    carry_scr[...] = total
    cnt_ref[...] = total


def _router(x, g, mod, which, w_router, b_router, dims, n_rows):
    d = x.shape[1]
    tm = ROW_BLOCK
    ne = N_EXPERTS
    nlb, bpb, nb = dims
    mrow = lambda i: _mod_row(i, nlb, bpb, nb)
    tri = (jnp.arange(tm)[:, None] < jnp.arange(tm)[None, :]).astype(bf16)
    return pl.pallas_call(
        _router_kernel,
        grid=(n_rows // tm,),
        in_specs=[pl.BlockSpec((tm, d), lambda i: (i, 0)),
                  pl.BlockSpec((1, d), lambda i: (0, 0)),
                  pl.BlockSpec((None, 1, d), lambda i: (mrow(i) * 6 + which + 1, 0, 0)),
                  pl.BlockSpec((None, 1, d), lambda i: (mrow(i) * 6 + which, 0, 0)),
                  pl.BlockSpec((ne, d), lambda i: (0, 0)),
                  pl.BlockSpec((ne, 1), lambda i: (0, 0)),
                  pl.BlockSpec((tm, tm), lambda i: (0, 0))],
        out_specs=[pl.BlockSpec((tm, d // 2 // LANE, LANE), lambda i: (i, 0, 0)),
                   pl.BlockSpec((TOP_K, tm), lambda i: (0, i)),
                   pl.BlockSpec((TOP_K, tm), lambda i: (0, i)),
                   pl.BlockSpec((TOP_K, tm), lambda i: (0, i)),
                   pl.BlockSpec((ne, 1), lambda i: (0, 0))],
        out_shape=[jax.ShapeDtypeStruct((n_rows, d // 2 // LANE, LANE), u32),
                   jax.ShapeDtypeStruct((TOP_K, n_rows), i32),
                   jax.ShapeDtypeStruct((TOP_K, n_rows), f32),
                   jax.ShapeDtypeStruct((TOP_K, n_rows), i32),
                   jax.ShapeDtypeStruct((ne, 1), f32)],
        scratch_shapes=[pltpu.VMEM((ne, 1), f32)],
        compiler_params=_params(("arbitrary",)),
    )(x, g.reshape(1, d), mod, mod, w_router.T, b_router.reshape(ne, 1).astype(f32), tri)


EXPERT_BURSTS = 8


def _expert_kernel(be_ref, nu_ref, tok0_ref, tokn_ref, posp_ref, h_hbm, wg_ref, wu_ref, wd_ref, y_hbm,
                   xbuf, ybuf, yrow, wgc, wuc, wdc, sem_g, sem_s):
    i = pl.program_id(0)
    nblk = pl.num_programs(0) - 1
    n_used = nu_ref[0]
    slot = i % 2
    mb = MOE_BLOCK
    de = wgc.shape[1]
    dp = yrow.shape[1]
    spare0 = y_hbm.shape[0] - 2 * mb
    gather_sites = EXPERT_BURSTS - 2

    def gather_row(idx_ref, r, dst_slot, priority=0):
        pltpu.make_async_copy(h_hbm.at[pl.ds(idx_ref[0, 0, r], 1)], xbuf.at[dst_slot, pl.ds(r, 1)],
                              sem_g.at[dst_slot]).start(priority=priority)

    def scatter_row(r):
        p = jnp.where(i == 0, spare0 + r, posp_ref[0, 0, r])
        pltpu.make_async_copy(ybuf.at[1 - slot, pl.ds(r, 1)], y_hbm.at[pl.ds(p, 1)],
                              sem_s.at[slot]).start(priority=r % 2)

    def burst(b, gather=True):
        s_rows = list(range(mb * b // EXPERT_BURSTS, mb * (b + 1) // EXPERT_BURSTS))
        g_rows = []
        if gather and b < gather_sites:
            g_rows = list(range(mb * b // gather_sites, mb * (b + 1) // gather_sites))
        for n in range(max(len(s_rows), len(g_rows))):
            if n < len(g_rows):
                gather_row(tokn_ref, g_rows[n], 1 - slot, g_rows[n] % 2)
            if n < len(s_rows):
                scatter_row(s_rows[n])

    def wait_scatter(parity):
        pltpu.make_async_copy(ybuf.at[0], y_hbm.at[pl.ds(0, mb)], sem_s.at[parity]).wait()

    @pl.when(i == 0)
    def _():
        ybuf[1] = jnp.zeros(ybuf.shape[1:], u32)

        def body(r, carry):
            gather_row(tok0_ref, r, 0)
            pltpu.make_async_copy(ybuf.at[1, pl.ds(r, 1)], y_hbm.at[pl.ds(spare0 + mb + r, 1)], sem_s.at[1]).start()
            return carry
        lax.fori_loop(0, mb, body, 0)

    @pl.when(i <= n_used)
    def _():
        pltpu.make_async_copy(h_hbm.at[pl.ds(0, mb)], xbuf.at[slot], sem_g.at[slot]).wait()

    ib = jnp.minimum(i, nblk - 1)
    changed = (i == 0) | (be_ref[ib] != be_ref[jnp.maximum(ib - 1, 0)])

    @pl.when(changed & (i < n_used))
    def _():
        wgc[...] = wg_ref[...].astype(bf16)
        wuc[...] = wu_ref[...].astype(bf16)
        wdc[...] = wd_ref[...].astype(bf16)

    @pl.when(i < n_used)
    def _():
        lo, hi = _unpack_halves(_tiles_to_rows(xbuf[slot]))
        x = jnp.concatenate([lo.astype(bf16), hi.astype(bf16)], axis=1)
        nh = EXPERT_BURSTS // 4
        hw = de // nh
        hids = []
        for c in range(nh):
            gate = _dot(x, wgc[:, c * hw:(c + 1) * hw])
            burst(2 * c)
            up = _dot(x, wuc[:, c * hw:(c + 1) * hw])
            burst(2 * c + 1)
            hids.append((_silu(gate) * up).astype(bf16))
        hid = jnp.concatenate(hids, axis=1)
        nd = EXPERT_BURSTS // 2
        dw = dp // nd
        for c in range(nd):
            wd_c = jnp.concatenate([wdc[:, c * dw:(c + 1) * dw], wdc[:, dp + c * dw:dp + (c + 1) * dw]], axis=1)
            yrow[:, c * dw:(c + 1) * dw] = _pack_halves(_dot(hid, wd_c))
            burst(2 * nh + c)
        wait_scatter(1 - slot)
        ybuf[slot] = _rows_to_tiles(yrow[...])

    @pl.when(i == n_used)
    def _():
        wait_scatter(1 - slot)
        for b in range(EXPERT_BURSTS):
            burst(b, gather=False)
        wait_scatter(slot)


def _experts(h, slot_tok, slot_pos, block_e, n_used, wg, wu, wd, layer, n_rows):
    tile = h.shape[1:]
    dp = tile[0] * tile[1]
    d = wg.shape[2]
    nblk = slot_tok.shape[0]
    de = wg.shape[3]
    mb = MOE_BLOCK
    last = nblk - 1
    smem = lambda f: pl.BlockSpec((1, 1, mb), f, memory_space=pltpu.SMEM)
    wspec = lambda shape: pl.BlockSpec((None,) + shape, lambda i, be, nu: (layer, be[jnp.minimum(i, last)], 0, 0))
    grid_spec = pltpu.PrefetchScalarGridSpec(
        num_scalar_prefetch=2,
        grid=(nblk + 1,),
        in_specs=[smem(lambda i, be, nu: (0, 0, 0)),
                  smem(lambda i, be, nu: (jnp.minimum(i + 1, last), 0, 0)),
                  smem(lambda i, be, nu: (jnp.clip(i - 1, 0, last), 0, 0)),
                  pl.BlockSpec(memory_space=pl.ANY),
                  wspec((None, d, de)), wspec((None, d, de)), wspec((None, de, d))],
        out_specs=pl.BlockSpec(memory_space=pl.ANY),
        scratch_shapes=[pltpu.VMEM((2, mb) + tile, u32), pltpu.VMEM((2, mb) + tile, u32), pltpu.VMEM((mb, dp), u32),
                        pltpu.VMEM((d, de), bf16), pltpu.VMEM((d, de), bf16), pltpu.VMEM((de, d), bf16),
                        pltpu.SemaphoreType.DMA((2,)), pltpu.SemaphoreType.DMA((2,))],
    )
    st = slot_tok.reshape(nblk, 1, mb)
    sp = slot_pos.reshape(nblk, 1, mb)
    return pl.pallas_call(
        _expert_kernel,
        grid_spec=grid_spec,
        out_shape=jax.ShapeDtypeStruct((n_rows * TOP_K + 2 * mb,) + tile, u32),
        compiler_params=_params(("arbitrary",)),
    )(block_e, n_used, st, st, sp, h, wg, wu, wd)


COMBINE_ROWS = 128


def _combine_kernel(*refs, final_norm):
    y_refs = refs[:TOP_K]
    w_ref, h_ref, sg_ref, su_ref, sd_ref, x_ref, gate_ref, gf_ref, o_ref = refs[TOP_K:]
    lo, hi = _unpack_halves(_tiles_to_rows(h_ref[...]))
    hb = jnp.concatenate([lo.astype(bf16), hi.astype(bf16)], axis=1)
    hid = _silu(_dot(hb, sg_ref[...])) * _dot(hb, su_ref[...])
    acc = _dot(hid.astype(bf16), sd_ref[...])
    shape = y_refs[0].shape
    acc_lo = acc_hi = 0.0
    for k in range(TOP_K):
        lo, hi = _unpack_halves(y_refs[k][...])
        wk = jnp.broadcast_to(w_ref[:, k:k + 1, :], shape)
        acc_lo = acc_lo + lo * wk
        acc_hi = acc_hi + hi * wk
    acc = acc + jnp.concatenate([_tiles_to_rows(acc_lo), _tiles_to_rows(acc_hi)], axis=1)
    out = x_ref[...] + gate_ref[...] * acc
    if final_norm:
        out = out * lax.rsqrt(jnp.mean(out * out, axis=-1, keepdims=True) + EPS) * gf_ref[...]
    o_ref[...] = out


def _combine(y_rows, wts, h, sg, su, sd, x, mod, which, g_final, dims, n_rows, final_norm):
    d = x.shape[1]
    de = sg.shape[1]
    tile = h.shape[1:]
    tm = COMBINE_ROWS
    per = ROW_BLOCK // tm
    nsteps = n_rows // tm
    nlb, bpb, nb = dims
    mrow = lambda i: _mod_row(i // per, nlb, bpb, nb)
    yspec = lambda k: pl.BlockSpec((tm,) + tile, lambda i: (k * nsteps + i, 0, 0))
    w3 = jnp.broadcast_to(wts.T[:, :, None], (n_rows, TOP_K, LANE))
    return pl.pallas_call(
        functools.partial(_combine_kernel, final_norm=final_norm),
        grid=(nsteps,),
        in_specs=[yspec(k) for k in range(TOP_K)] + [
                  pl.BlockSpec((tm, TOP_K, LANE), lambda i: (i, 0, 0)),
                  pl.BlockSpec((tm,) + tile, lambda i: (i, 0, 0)),
                  pl.BlockSpec((d, de), lambda i: (0, 0)),
                  pl.BlockSpec((d, de), lambda i: (0, 0)),
                  pl.BlockSpec((de, d), lambda i: (0, 0)),
                  pl.BlockSpec((tm, d), lambda i: (i, 0)),
                  pl.BlockSpec((None, 1, d), lambda i: (mrow(i) * 6 + which, 0, 0)),
                  pl.BlockSpec((1, d), lambda i: (0, 0))],
        out_specs=pl.BlockSpec((tm, d), lambda i: (i, 0)),
        out_shape=jax.ShapeDtypeStruct((n_rows, d), f32),
        compiler_params=_params(("parallel",)),
    )(*([y_rows] * TOP_K), w3, h, sg, su, sd, x, mod, g_final.reshape(1, d))


def _moe(x, g, mod, which_shift, w_router, b_router, wg, wu, wd, layer, sg, su, sd, g_final, dims, n_rows,
         final_norm):
    ne = N_EXPERTS
    mb = MOE_BLOCK
    h, idx, wts, rank, cnt = _router(x, g, mod, which_shift, w_router, b_router, dims, n_rows)
    counts = cnt[:, 0].astype(i32)
    padded = (counts + mb - 1) // mb * mb
    pend = jnp.cumsum(padded)
    start = pend - padded
    onehot = idx[:, :, None] == jnp.arange(ne, dtype=i32)[None, None, :]
    dest = jnp.sum(jnp.where(onehot, start[None, None, :], 0), axis=-1) + rank
    n_assign = n_rows * TOP_K
    nblk = (n_assign + ne * (mb - 1) + mb - 1) // mb
    n_pad = nblk * mb - n_assign
    pcnt = padded - counts
    pcum = jnp.cumsum(pcnt)
    j = jnp.arange(n_pad, dtype=i32)
    e_j = jnp.sum((pcum[None, :] <= j[:, None]).astype(i32), axis=1)
    base = start + counts - (pcum - pcnt)
    base_j = jnp.sum(jnp.where(e_j[:, None] == jnp.arange(ne, dtype=i32)[None, :], base[None, :], 0), axis=1)
    pad_slot = jnp.where(e_j < ne, base_j + j, pend[-1] + j - pcum[-1])
    keys = jnp.concatenate([dest.reshape(-1), pad_slot])
    vals = jnp.concatenate([jnp.arange(n_assign, dtype=i32), n_assign + pad_slot % mb])
    _, slot_pos = lax.sort((keys, vals), num_keys=1)
    slot_tok = jnp.where(slot_pos < n_assign, slot_pos % n_rows, 0)
    blk_start = jnp.arange(nblk, dtype=i32) * mb
    block_e = jnp.minimum(jnp.sum((pend[None, :] <= blk_start[:, None]).astype(i32), axis=1), ne - 1)
    n_used = (pend[-1:] // mb).astype(i32)
    y_rows = _experts(h, slot_tok.reshape(nblk, mb), slot_pos.reshape(nblk, mb), block_e, n_used, wg, wu, wd,
                      layer, n_rows)
    return _combine(y_rows, wts, h, sg, su, sd, x, mod, which_shift + 2, g_final, dims, n_rows, final_norm)


def _rope_tables(seq, head_dim):
    half = head_dim // 2
    quarter = half // 2
    t = jnp.arange(seq, dtype=i32)
    rows = (t // GRID_W).astype(f32)
    cols = (t % GRID_W).astype(f32)
    freqs = ROPE_THETA ** (-jnp.arange(0, half, 2, dtype=f32) / half)
    ar = rows[:, None] * freqs[None, :]
    ac = cols[:, None] * freqs[None, :]
    cos = jnp.concatenate([jnp.cos(ar), jnp.cos(ar), jnp.cos(ac), jnp.cos(ac)], axis=1)
    sin = jnp.concatenate([-jnp.sin(ar), jnp.sin(ar), -jnp.sin(ac), jnp.sin(ac)], axis=1)
    reps = LANE // head_dim
    cos = jnp.tile(cos, (1, reps))
    sin = jnp.tile(sin, (1, reps))
    cos = jnp.concatenate([cos, jnp.ones((ROW_BLOCK, LANE), f32)], axis=0)
    sin = jnp.concatenate([sin, jnp.zeros((ROW_BLOCK, LANE), f32)], axis=0)
    return cos, sin, quarter


def kernel(x, c, ctx, c_ctx, w_ada, b_ada, g_norm_mix, g_norm_ffn, w_in_even, w_out_even, hgrn_lb_logits,
           g_hgrn_norm, win_sink, w_in_odd, w_out_odd, g_q_norm, g_k_norm, w_router, b_router,
           w_exp_gate, w_exp_up, w_exp_down, w_sh_gate, w_sh_up, w_sh_down, g_norm_final):
    nb, seq, d = x.shape
    lc = ctx.shape[1]
    assert seq % ROW_BLOCK == 0 and lc % ROW_BLOCK == 0 and seq % lc == 0
    t_lat = nb * seq
    t_all = t_lat + nb * lc
    bpb = seq // ROW_BLOCK
    nlb = t_lat // ROW_BLOCK
    dims = (nlb, bpb, nb)

    xs = jnp.concatenate([x.reshape(t_lat, d), ctx.reshape(nb * lc, d)], axis=0)
    cc = jnp.concatenate([c, c_ctx[None, :], jnp.zeros((8 - nb - 1, d), f32)], axis=0)

    lb_w = jax.nn.softmax(hgrn_lb_logits.astype(f32), axis=0)
    lower_bounds = jnp.cumsum(lb_w, axis=0)[1:] - lb_w[0]

    mod = _ada(cc, w_ada, b_ada, 0).reshape(8 * 6, 1, d)
    w_in = w_in_even[0].astype(bf16)
    proj = _normmm(xs, g_norm_mix[0], mod, 0, w_in[:, :HGRN_COLS], f32, dims, tn=1280)
    cos, sin, quarter = _rope_tables(seq, WIN_HEAD_DIM)
    epi = dict(blocks={0: (WIN_HEADS + WIN_KV_HEADS) * WIN_HEAD_DIM // LANE}, quarter=quarter, head_norm=False,
               q_slices=WIN_HEADS * WIN_HEAD_DIM // LANE, q_scale=WIN_HEAD_DIM ** -0.5)
    wproj = _normmm(xs, g_norm_mix[0], mod, 0, w_in[:, HGRN_COLS:], bf16, dims, tn=WIN_COLS,
                    epilogue=epi, tables=(cos, sin))
    o_f, o_b = _hgrn(proj, lower_bounds[0], nb, seq, lc)
    a_mix = _hgrn_out(o_f, o_b, proj, g_hgrn_norm[0])
    b_mix = _win_attn(wproj, win_sink[0], nb, seq, lc)
    xs = _outproj(a_mix, 0, b_mix, 0, w_out_even[0].astype(bf16), xs, mod, 2, dims, t_all)
    xs = _moe(xs, g_norm_ffn[0], mod, 3, w_router[0], b_router[0],
              w_exp_gate, w_exp_up, w_exp_down, 0,
              w_sh_gate[0].astype(bf16), w_sh_up[0].astype(bf16), w_sh_down[0].astype(bf16),
              g_norm_final, dims, t_all, False)

    mod = _ada(cc, w_ada, b_ada, 1).reshape(8 * 6, 1, d)
    cos, sin, quarter = _rope_tables(seq, GLB_HEAD_DIM)
    tn = 1536
    n_sl = tn // LANE
    rope_slices = GLB_HEADS + GLB_KV_HEADS
    epi = dict(blocks={0: min(rope_slices, n_sl), 1: max(rope_slices - n_sl, 0)}, quarter=quarter, head_norm=True,
               q_slices=GLB_HEADS, q_scale=GLB_HEAD_DIM ** -0.5)
    qkv = _normmm(xs, g_norm_mix[1], mod, 0, w_in_odd[0].astype(bf16), bf16, dims, tn=tn,
                  epilogue=epi, tables=(cos, sin), gains=(g_q_norm[0], g_k_norm[0]))
    att = _glb_attn(qkv, nb, seq, lc)
    xl = _outproj(att, 0, att, 1, w_out_odd[0].astype(bf16), xs, mod, 2, dims, t_lat)
    out = _moe(xl, g_norm_ffn[1], mod, 3, w_router[1], b_router[1],
               w_exp_gate, w_exp_up, w_exp_down, 1,
               w_sh_gate[1].astype(bf16), w_sh_up[1].astype(bf16), w_sh_down[1].astype(bf16),
               g_norm_final, dims, t_lat, True)
    return out.reshape(nb, seq, d)
```

```python
import functools

import jax
import jax.numpy as jnp
from jax import lax
from jax.experimental import pallas as pl
from jax.experimental.pallas import tpu as pltpu

f32 = jnp.float32
bf16 = jnp.bfloat16
i32 = jnp.int32
u32 = jnp.uint32

EPS = 1e-6
LOG2E = 1.4426950408889634
ROPE_THETA = 10000.0
GRID_W = 64

HGRN_DK = 128
HGRN_HEADS = 8
HGRN_CHUNK = 64
HGRN_SUB = 16
HGRN_HEADS_PER_STEP = 8
HGRN_SAFE_DECAY = 80.0
WIN_HEAD_DIM = 64
WIN_HEADS = 16
WIN_KV_HEADS = 2
WIN_BLOCK = 128
GLB_HEAD_DIM = 128
GLB_HEADS = 16
GLB_KV_HEADS = 4
Q_BLOCK = 256
N_EXPERTS = 64
N_GROUPS = 8
TOPK_GROUPS = 4
TOP_K = 8
ROUTED_SCALE = 2.5
MOE_BLOCK = 256

LANE = 128
ROW_BLOCK = 256
VMEM_LIMIT = 56 * 1024 * 1024

A_K = HGRN_HEADS * HGRN_DK
HGRN_COLS = 5 * A_K
WIN_COLS = WIN_HEADS * WIN_HEAD_DIM + 2 * WIN_KV_HEADS * WIN_HEAD_DIM


def _params(sem, vmem=VMEM_LIMIT):
    return pltpu.CompilerParams(dimension_semantics=sem, vmem_limit_bytes=vmem)


def _silu(x):
    return x * jax.nn.sigmoid(x)


def _dot(a, b):
    return jnp.dot(a, b, preferred_element_type=f32)


def _dot_nt(a, b):
    return lax.dot_general(a, b, (((1,), (1,)), ((), ())), preferred_element_type=f32)


def _pack_halves(x):
    n = x.shape[1] // 2
    lo = lax.bitcast_convert_type(x[:, :n].astype(bf16).astype(f32), u32)
    hi = lax.bitcast_convert_type(x[:, n:].astype(bf16).astype(f32), u32)
    return (lo >> 16) | hi


def _unpack_halves(w):
    lo = lax.bitcast_convert_type(w << 16, f32)
    hi = lax.bitcast_convert_type(w & jnp.uint32(0xFFFF0000), f32)
    return lo, hi


def _rows_to_tiles(w):
    n = w.shape[1] // LANE
    return pltpu.einshape("stl->tsl", jnp.stack([w[:, s * LANE:(s + 1) * LANE] for s in range(n)], axis=0))


def _tiles_to_rows(t):
    y = pltpu.einshape("tsl->stl", t)
    return jnp.concatenate([y[s] for s in range(t.shape[1])], axis=1)


def _dot_tn(a, b):
    return lax.dot_general(a, b, (((0,), (0,)), ((), ())), preferred_element_type=f32)


def _ada_kernel(c_ref, w_ref, b_ref, o_ref):
    a = _silu(c_ref[...]).astype(bf16)
    o_ref[...] = _dot(a, w_ref[...].astype(bf16)) + b_ref[...]


def _ada(c8, w, b, layer, tn=768):
    m, d = c8.shape
    n = w.shape[2]
    return pl.pallas_call(
        _ada_kernel,
        grid=(n // tn,),
        in_specs=[pl.BlockSpec((m, d), lambda j: (0, 0)),
                  pl.BlockSpec((None, d, tn), lambda j: (layer, 0, j)),
                  pl.BlockSpec((None, 1, tn), lambda j: (layer, 0, j))],
        out_specs=pl.BlockSpec((m, tn), lambda j: (0, j)),
        out_shape=jax.ShapeDtypeStruct((m, n), f32),
        compiler_params=_params(("arbitrary",)),
    )(c8, w, b.reshape(b.shape[0], 1, n))


def _norm_mod(x, g, sc, sh):
    y = x * lax.rsqrt(jnp.mean(x * x, axis=-1, keepdims=True) + EPS) * g
    return y * (1.0 + sc) + sh


def _normmm_kernel(x_ref, g_ref, sc_ref, sh_ref, w_ref, *rest, epilogue, tn):
    if epilogue is None:
        (o_ref,) = rest
    elif epilogue["head_norm"]:
        cos_ref, sin_ref, gq_ref, gk_ref, o_ref = rest
    else:
        cos_ref, sin_ref, o_ref = rest
    j = pl.program_id(0)
    h = _norm_mod(x_ref[...], g_ref[...], sc_ref[...], sh_ref[...]).astype(bf16)
    y = _dot(h, w_ref[...])
    if epilogue is None:
        o_ref[...] = y.astype(o_ref.dtype)
        return

    n_sl = tn // LANE
    for jb in range(epilogue["n_col_blocks"]):
        n_rope = epilogue["blocks"].get(jb, 0)

        @pl.when(j == jb)
        def _(jb=jb, n_rope=n_rope):
            cos = cos_ref[...]
            sin = sin_ref[...]
            quarter = epilogue["quarter"]
            lane = lax.broadcasted_iota(i32, (y.shape[0], LANE), 1)
            first = lane % (2 * quarter) < quarter
            ys = [y[:, s * LANE:(s + 1) * LANE] for s in range(n_sl)]
            is_q = [jb * n_sl + s < epilogue["q_slices"] for s in range(n_sl)]
            if epilogue["head_norm"]:
                ms = [jnp.mean(ys[s] * ys[s], axis=-1, keepdims=True) for s in range(n_rope)]
                for s in range(n_rope):
                    gain = gq_ref[...] if is_q[s] else gk_ref[...]
                    ys[s] = ys[s] * lax.rsqrt(ms[s] + EPS) * gain
            ups = [pltpu.roll(ys[s], LANE - quarter, axis=1) for s in range(n_rope)]
            dns = [pltpu.roll(ys[s], quarter, axis=1) for s in range(n_rope)]
            for s in range(n_rope):
                ys[s] = ys[s] * cos + jnp.where(first, ups[s], dns[s]) * sin
                if is_q[s]:
                    ys[s] = ys[s] * epilogue["q_scale"]
            o_ref[...] = jnp.concatenate([v.astype(o_ref.dtype) for v in ys], axis=1)


def _mod_row(i, n_lat_blocks, blocks_per_batch, n_batch):
    return jnp.where(i < n_lat_blocks, i // blocks_per_batch, n_batch)


def _normmm(x, g, mod, which, w, out_dtype, dims, tn, epilogue=None, tables=None, gains=None):
    t, d = x.shape
    n = w.shape[1]
    tm = ROW_BLOCK
    nlb, bpb, nb = dims
    mrow = lambda i: _mod_row(i, nlb, bpb, nb)
    in_specs = [pl.BlockSpec((tm, d), lambda j, i: (i, 0)),
                pl.BlockSpec((1, d), lambda j, i: (0, 0)),
                pl.BlockSpec((None, 1, d), lambda j, i: (mrow(i) * 6 + which + 1, 0, 0)),
                pl.BlockSpec((None, 1, d), lambda j, i: (mrow(i) * 6 + which, 0, 0)),
                pl.BlockSpec((d, tn), lambda j, i: (0, j))]
    args = [x, g.reshape(1, d), mod, mod, w]
    if epilogue is not None:
        cos, sin = tables
        tab = lambda j, i: (jnp.where(i < nlb, i % bpb, bpb), 0)
        in_specs += [pl.BlockSpec((tm, LANE), tab), pl.BlockSpec((tm, LANE), tab)]
        args += [cos, sin]
        epilogue = dict(epilogue, n_col_blocks=n // tn)
        if epilogue["head_norm"]:
            in_specs += [pl.BlockSpec((1, LANE), lambda j, i: (0, 0))] * 2
            args += [gains[0].reshape(1, LANE), gains[1].reshape(1, LANE)]
    return pl.pallas_call(
        functools.partial(_normmm_kernel, epilogue=epilogue, tn=tn),
        grid=(n // tn, t // tm),
        in_specs=in_specs,
        out_specs=pl.BlockSpec((tm, tn), lambda j, i: (i, j)),
        out_shape=jax.ShapeDtypeStruct((t, n), out_dtype),
        compiler_params=_params(("parallel", "parallel")),
    )(*args)


def _gla_gates(q_raw, f_raw, lb, rev):
    c = HGRN_CHUNK
    sub = HGRN_SUB
    q = _silu(q_raw)
    f = lb + (1.0 - lb) * jax.nn.sigmoid(f_raw)
    kk = 1.0 - f
    g = jnp.log(f)
    row = lax.broadcasted_iota(i32, (c, c), 0)
    col = lax.broadcasted_iota(i32, (c, c), 1)
    tri = (col >= row) if rev else (col <= row)
    b = jnp.dot(tri.astype(f32), g, preferred_element_type=f32, precision=lax.Precision.HIGHEST)
    span = None
    for jb in range(c // sub):
        d = jnp.abs(b[jb * sub:jb * sub + 1] - b[(jb + 1) * sub - 1:(jb + 1) * sub])
        span = d if span is None else jnp.maximum(span, d)
    return q, kk, b, span


def _gla_chunks_factored(chains):
    c = HGRN_CHUNK
    sub = HGRN_SUB
    nsub = c // sub
    rrow = lax.broadcasted_iota(i32, (c, sub), 0)
    rcol = lax.broadcasted_iota(i32, (c, sub), 1)
    work = []
    for (q, kk, b), v_ref, st_ref, _, o_ref, sl, rev in chains:
        st = st_ref[...]
        o = _dot_nt((q * jnp.exp(b)).astype(bf16), st.astype(bf16))
        cols = []
        for jb in range(nsub):
            js = slice(jb * sub, (jb + 1) * sub)
            m_j = b[jb * sub:jb * sub + 1] if rev else b[(jb + 1) * sub - 1:(jb + 1) * sub]
            kd = kk[js] * jnp.exp(m_j - b[js])
            qd = q * jnp.exp(jnp.minimum(b - m_j, HGRN_SAFE_DECAY))
            cols.append(_dot_nt(qd.astype(bf16), kd.astype(bf16)))
        work.append((o, cols, st))
    outs = []
    for ((q, kk, b), v_ref, st_ref, _, o_ref, sl, rev), (o, cols, st) in zip(chains, work):
        vb = v_ref[:, sl].astype(bf16)
        for jb in range(nsub):
            keep = (rrow <= jb * sub + rcol) if rev else (rrow >= jb * sub + rcol)
            a = jnp.where(keep, cols[jb], 0.0).astype(bf16)
            o = o + _dot(a, vb[jb * sub:(jb + 1) * sub])
        b_end = b[0:1] if rev else b[c - 1:c]
        kdec = kk * jnp.exp(b_end - b)
        outs.append((o, jnp.exp(b_end) * st + _dot_tn(vb, kdec.astype(bf16))))
    for (_, v_ref, st_ref, _, o_ref, sl, rev), (o, st_new) in zip(chains, outs):
        o_ref[:, sl] = o
        st_ref[...] = st_new


def _gla_chunk(q, kk, b, v, st_ref, a_scr, rev):
    c = HGRN_CHUNK
    sub = HGRN_SUB
    b_end = b[0:1] if rev else b[c - 1:c]
    st = st_ref[...]
    o = _dot_nt((q * jnp.exp(b)).astype(bf16), st.astype(bf16))

    vb = v.astype(bf16)
    nsub = c // sub
    lane16 = lax.broadcasted_iota(i32, (sub, sub), 1)
    row16 = lax.broadcasted_iota(i32, (sub, sub), 0)
    diag_ok = (row16 <= lane16) if rev else (row16 >= lane16)
    rsub = lax.broadcasted_iota(i32, (c, sub), 0) // sub
    for jb in range(nsub):
        js = slice(jb * sub, (jb + 1) * sub)
        m_j = b[jb * sub:jb * sub + 1] if rev else b[(jb + 1) * sub - 1:(jb + 1) * sub]
        kd = kk[js] * jnp.exp(m_j - b[js])
        qd = q * jnp.exp(jnp.minimum(b - m_j, 0.0))
        a_col = _dot_nt(qd.astype(bf16), kd.astype(bf16))
        off_ok = (rsub < jb) if rev else (rsub > jb)
        a_col = jnp.where(off_ok, a_col, 0.0)
        qi, bi, ki = q[js], b[js] * LOG2E, kk[js]
        a_dd = jnp.zeros((sub, sub), f32)
        for s in range(sub):
            e = jnp.exp2(bi - bi[s:s + 1])
            col_s = jnp.sum(qi * ki[s:s + 1] * e, axis=-1, keepdims=True)
            a_dd = jnp.where(lane16 == s, col_s, a_dd)
        a_dd = jnp.where(diag_ok, a_dd, 0.0)
        pieces = [a_dd if ib == jb else a_col[ib * sub:(ib + 1) * sub] for ib in range(nsub)]
        a_scr[:, js] = jnp.concatenate(pieces, axis=0)
    o = o + _dot(a_scr[...].astype(bf16), vb)
    kdec = kk * jnp.exp(b_end - b)
    st_ref[...] = jnp.exp(b_end) * st + _dot_tn(vb, kdec.astype(bf16))
    return o


def _hgrn_kernel(qf_ref, vf_ref, ff_ref, qb_ref, vb_ref, fb_ref, lb_ref, of_ref, ob_ref,
                 stf_ref, stb_ref, af_scr, ab_scr):
    @pl.when(pl.program_id(2) == 0)
    def _():
        stf_ref[...] = jnp.zeros_like(stf_ref)
        stb_ref[...] = jnp.zeros_like(stb_ref)

    gates = []
    span = None
    for j in range(HGRN_HEADS_PER_STEP):
        sl = slice(j * HGRN_DK, (j + 1) * HGRN_DK)
        lb = lb_ref[:, sl]
        gf = _gla_gates(qf_ref[:, sl], ff_ref[:, sl], lb, False)
        gb = _gla_gates(qb_ref[:, sl], fb_ref[:, sl], lb, True)
        gates.append((gf[:3], gb[:3]))
        for d in (gf[3], gb[3]):
            span = d if span is None else jnp.maximum(span, d)
    mild = jnp.max(span) <= HGRN_SAFE_DECAY

    chains = []
    for j in range(HGRN_HEADS_PER_STEP):
        sl = slice(j * HGRN_DK, (j + 1) * HGRN_DK)
        chains.append((gates[j][0], vf_ref, stf_ref.at[j], af_scr.at[j], of_ref, sl, False))
        chains.append((gates[j][1], vb_ref, stb_ref.at[j], ab_scr.at[j], ob_ref, sl, True))

    @pl.when(mild)
    def _():
        _gla_chunks_factored(chains)

    @pl.when(jnp.logical_not(mild))
    def _():
        for (q, kk, b), v_ref, st_ref, a_scr, o_ref, sl, rev in chains:
            o_ref[:, sl] = _gla_chunk(q, kk, b, v_ref[:, sl], st_ref, a_scr, rev)


def _hgrn(proj, lb, n_batch, seq, ctx_len):
    t = proj.shape[0]
    c = HGRN_CHUNK
    hp = HGRN_HEADS_PER_STEP
    wblk = hp * HGRN_DK
    ngrp = HGRN_HEADS // hp
    ncc, ncl = ctx_len // c, seq // c
    lat_chunks = n_batch * ncl

    def fwd_blk(b, s):
        return jnp.where(s < ncc, lat_chunks + b * ncc + s, b * ncl + (s - ncc))

    def bwd_blk(b, s):
        return jnp.where(s < ncc, lat_chunks + b * ncc + (ncc - 1 - s), b * ncl + (ncl - 1 - (s - ncc)))

    def spec(blk, section):
        return pl.BlockSpec((c, wblk), lambda b, h, s: (blk(b, s), section * ngrp + h))

    def ospec(blk):
        return pl.BlockSpec((c, wblk), lambda b, h, s: (blk(b, s), h))

    out = jax.ShapeDtypeStruct((t, A_K), f32)
    return pl.pallas_call(
        _hgrn_kernel,
        grid=(n_batch, ngrp, ncc + ncl),
        in_specs=[spec(fwd_blk, 0), spec(fwd_blk, 1), spec(fwd_blk, 2),
                  spec(bwd_blk, 0), spec(bwd_blk, 1), spec(bwd_blk, 3),
                  pl.BlockSpec((1, wblk), lambda b, h, s: (0, h))],
        out_specs=[ospec(fwd_blk), ospec(bwd_blk)],
        out_shape=[out, out],
        scratch_shapes=[pltpu.VMEM((hp, HGRN_DK, HGRN_DK), f32), pltpu.VMEM((hp, HGRN_DK, HGRN_DK), f32),
                        pltpu.VMEM((hp, c, c), f32), pltpu.VMEM((hp, c, c), f32)],
        compiler_params=_params(("parallel", "parallel", "arbitrary")),
    )(proj, proj, proj, proj, proj, proj, lb.reshape(1, A_K))


def _hgrn_out_kernel(of_ref, ob_ref, g_ref, gain_ref, o_ref):
    gain = gain_ref[...]
    outs = []
    for h in range(HGRN_HEADS):
        sl = slice(h * HGRN_DK, (h + 1) * HGRN_DK)
        o = of_ref[:, sl] + ob_ref[:, sl]
        o = o * lax.rsqrt(jnp.mean(o * o, axis=-1, keepdims=True) + EPS) * gain
        outs.append((o * _silu(g_ref[:, sl])).astype(bf16))
    o_ref[...] = jnp.concatenate(outs, axis=1)


def _hgrn_out(o_f, o_b, proj, gain):
    t = o_f.shape[0]
    tm = ROW_BLOCK
    return pl.pallas_call(
        _hgrn_out_kernel,
        grid=(t // tm,),
        in_specs=[pl.BlockSpec((tm, A_K), lambda i: (i, 0)),
                  pl.BlockSpec((tm, A_K), lambda i: (i, 0)),
                  pl.BlockSpec((tm, A_K), lambda i: (i, 4)),
                  pl.BlockSpec((1, HGRN_DK), lambda i: (0, 0))],
        out_specs=pl.BlockSpec((tm, A_K), lambda i: (i, 0)),
        out_shape=jax.ShapeDtypeStruct((t, A_K), bf16),
        compiler_params=_params(("parallel",)),
    )(o_f, o_b, proj, gain.reshape(1, HGRN_DK))


def _win_kernel(sink_ref, q_ref, kp_ref, kc_ref, kn_ref, vp_ref, vc_ref, vn_ref, kx_ref, vx_ref, o_ref,
                *, n_lat_blocks):
    n = pl.program_id(1)
    w = WIN_BLOCK
    dh = WIN_HEAD_DIM
    is_lat = n < n_lat_blocks
    ri = lax.broadcasted_iota(i32, (w, w), 0)
    ci = lax.broadcasted_iota(i32, (w, w), 1)
    ok_p = (ci >= ri) & is_lat & (n > 0)
    ok_c = jnp.broadcast_to(is_lat, (w, w))
    ok_n = (ci <= ri) & is_lat & (n < n_lat_blocks - 1)
    valid = jnp.concatenate([ok_p, ok_c, ok_n], axis=1)
    kwin = jnp.concatenate([kp_ref[...], kc_ref[...], kn_ref[...]], axis=0)
    vwin = jnp.concatenate([vp_ref[...], vc_ref[...], vn_ref[...]], axis=0)
    kx = kx_ref[...]
    vx = vx_ref[...]
    g = WIN_HEADS // WIN_KV_HEADS
    scores = []
    for h in range(WIN_HEADS):
        ks = slice((h // g) * dh, (h // g + 1) * dh)
        qh = q_ref[:, h * dh:(h + 1) * dh]
        scores.append((jnp.where(valid, _dot_nt(qh, kwin[:, ks]), -jnp.inf), _dot_nt(qh, kx[:, ks])))
    probs = []
    for h, (sw, sx) in enumerate(scores):
        sk = sink_ref[h]
        m = jnp.maximum(jnp.maximum(jnp.max(sw, axis=-1, keepdims=True), jnp.max(sx, axis=-1, keepdims=True)), sk)
        pw = jnp.exp(sw - m)
        px = jnp.exp(sx - m)
        den = jnp.sum(pw, axis=-1, keepdims=True) + jnp.sum(px, axis=-1, keepdims=True) + jnp.exp(sk - m)
        probs.append((pw.astype(bf16), px.astype(bf16), den))
    outs = []
    for h, (pw, px, den) in enumerate(probs):
        ks = slice((h // g) * dh, (h // g + 1) * dh)
        o = _dot(pw, vwin[:, ks]) + _dot(px, vx[:, ks])
        outs.append((o / den).astype(bf16))
    o_ref[...] = jnp.concatenate(outs, axis=1)


def _win_attn(wproj, sink, n_batch, seq, ctx_len):
    t = wproj.shape[0]
    w = WIN_BLOCK
    nlb = seq // w
    ncb = ctx_len // w
    lat_blocks = n_batch * nlb
    qc = WIN_HEADS * WIN_HEAD_DIM // LANE
    kcol, vcol = qc, qc + 1

    def qrow(b, n):
        return jnp.where(n < nlb, b * nlb + n, lat_blocks + b * ncb + (n - nlb))

    def krow(off):
        def f(b, n):
            return b * nlb + jnp.clip(n + off, 0, nlb - 1)
        return f

    def kspec(off, colblk):
        return pl.BlockSpec((w, LANE), lambda b, n: (krow(off)(b, n), colblk))

    def xspec(colblk):
        return pl.BlockSpec((ctx_len, LANE), lambda b, n: ((n_batch * seq) // ctx_len + b, colblk))

    return pl.pallas_call(
        functools.partial(_win_kernel, n_lat_blocks=nlb),
        grid=(n_batch, nlb + ncb),
        in_specs=[pl.BlockSpec(memory_space=pltpu.SMEM),
                  pl.BlockSpec((w, qc * LANE), lambda b, n: (qrow(b, n), 0)),
                  kspec(-1, kcol), kspec(0, kcol), kspec(1, kcol),
                  kspec(-1, vcol), kspec(0, vcol), kspec(1, vcol),
                  xspec(kcol), xspec(vcol)],
        out_specs=pl.BlockSpec((w, qc * LANE), lambda b, n: (qrow(b, n), 0)),
        out_shape=jax.ShapeDtypeStruct((t, qc * LANE), bf16),
        compiler_params=_params(("parallel", "arbitrary")),
    )(sink.astype(f32), wproj, wproj, wproj, wproj, wproj, wproj, wproj, wproj, wproj)


GLB_KEY_CHUNK = 256


def _glb_kernel(q_ref, kx_ref, kl_ref, vx_ref, vl_ref, o_ref, m_scr, l_scr, acc_scr):
    dh = GLB_HEAD_DIM
    g = GLB_HEADS // GLB_KV_HEADS
    ck = GLB_KEY_CHUNK
    tq = q_ref.shape[0]
    q = jnp.concatenate([q_ref[:, i * dh:(i + 1) * dh] for i in range(g)], axis=0)

    def update(s, v, first):
        reps = s.shape[1] // LANE
        mx = jnp.broadcast_to(jnp.max(s, axis=-1, keepdims=True), (g * tq, LANE))
        if first:
            m_new = mx
            p = jnp.exp(s - jnp.concatenate([m_new] * reps, axis=1))
            l_scr[...] = jnp.broadcast_to(jnp.sum(p, axis=-1, keepdims=True), (g * tq, LANE))
            acc_scr[...] = _dot(p.astype(bf16), v)
        else:
            m_old = m_scr[...]
            m_new = jnp.maximum(m_old, mx)
            alpha = jnp.exp(m_old - m_new)
            p = jnp.exp(s - jnp.concatenate([m_new] * reps, axis=1))
            l_scr[...] = alpha * l_scr[...] + jnp.broadcast_to(jnp.sum(p, axis=-1, keepdims=True), (g * tq, LANE))
            acc_scr[...] = alpha * acc_scr[...] + _dot(p.astype(bf16), v)
        m_scr[...] = m_new

    chunks = [(kx_ref, vx_ref, c, min(ck, kx_ref.shape[0] - c)) for c in range(0, kx_ref.shape[0], ck)]
    chunks += [(kl_ref, vl_ref, c, min(ck, kl_ref.shape[0] - c)) for c in range(0, kl_ref.shape[0], ck)]
    for n, (kr, vr, c0, sz) in enumerate(chunks):
        update(_dot_nt(q, kr[c0:c0 + sz, :]), vr[c0:c0 + sz, :], n == 0)
    o = acc_scr[...] / l_scr[...]
    o_ref[...] = jnp.concatenate([o[i * tq:(i + 1) * tq] for i in range(g)], axis=1).astype(bf16)


def _glb_attn(qkv, n_batch, seq, ctx_len):
    tq = Q_BLOCK
    dh = GLB_HEAD_DIM
    g = GLB_HEADS // GLB_KV_HEADS
    nq = seq // tq
    kcol0 = GLB_HEADS
    vcol0 = GLB_HEADS + GLB_KV_HEADS
    ctx_blk0 = (n_batch * seq) // ctx_len
    return pl.pallas_call(
        _glb_kernel,
        grid=(n_batch, GLB_KV_HEADS, nq),
        in_specs=[pl.BlockSpec((tq, g * dh), lambda b, k, i: (b * nq + i, k)),
                  pl.BlockSpec((ctx_len, dh), lambda b, k, i: (ctx_blk0 + b, kcol0 + k)),
                  pl.BlockSpec((seq, dh), lambda b, k, i: (b, kcol0 + k)),
                  pl.BlockSpec((ctx_len, dh), lambda b, k, i: (ctx_blk0 + b, vcol0 + k)),
                  pl.BlockSpec((seq, dh), lambda b, k, i: (b, vcol0 + k))],
        out_specs=pl.BlockSpec((tq, g * dh), lambda b, k, i: (b * nq + i, k)),
        out_shape=jax.ShapeDtypeStruct((n_batch * seq, GLB_HEADS * dh), bf16),
        scratch_shapes=[pltpu.VMEM((g * tq, LANE), f32), pltpu.VMEM((g * tq, LANE), f32), pltpu.VMEM((g * tq, dh), f32)],
        compiler_params=_params(("parallel", "parallel", "arbitrary")),
    )(qkv, qkv, qkv, qkv, qkv)


def _outproj_kernel(a1_ref, a2_ref, w1_ref, w2_ref, x_ref, gate_ref, o_ref):
    y = _dot(a1_ref[...], w1_ref[...]) + _dot(a2_ref[...], w2_ref[...])
    o_ref[...] = x_ref[...] + gate_ref[...] * y


def _outproj(a1, c1, a2, c2, w, x, mod, which, dims, n_rows, tn=1024):
    d = x.shape[1]
    kh = w.shape[0] // 2
    tm = ROW_BLOCK
    nlb, bpb, nb = dims
    mrow = lambda i: _mod_row(i, nlb, bpb, nb)
    return pl.pallas_call(
        _outproj_kernel,
        grid=(d // tn, n_rows // tm),
        in_specs=[pl.BlockSpec((tm, kh), lambda j, i: (i, c1)),
                  pl.BlockSpec((tm, kh), lambda j, i: (i, c2)),
                  pl.BlockSpec((kh, tn), lambda j, i: (0, j)),
                  pl.BlockSpec((kh, tn), lambda j, i: (1, j)),
                  pl.BlockSpec((tm, tn), lambda j, i: (i, j)),
                  pl.BlockSpec((None, 1, tn), lambda j, i: (mrow(i) * 6 + which, 0, j))],
        out_specs=pl.BlockSpec((tm, tn), lambda j, i: (i, j)),
        out_shape=jax.ShapeDtypeStruct((n_rows, d), f32),
        compiler_params=_params(("parallel", "parallel")),
    )(a1, a2, w, w, x, mod)


def _router_kernel(x_ref, g_ref, sc_ref, sh_ref, wr_ref, br_ref, u_ref,
                   h_ref, idx_ref, wt_ref, rank_ref, cnt_ref, carry_scr):
    i = pl.program_id(0)
    tm = x_ref.shape[0]
    ne, ng, pg = N_EXPERTS, N_GROUPS, N_EXPERTS // N_GROUPS

    @pl.when(i == 0)
    def _():
        carry_scr[...] = jnp.zeros_like(carry_scr)

    h = _norm_mod(x_ref[...], g_ref[...], sc_ref[...], sh_ref[...])
    h_ref[...] = _rows_to_tiles(_pack_halves(h))
    logits = lax.dot_general(wr_ref[...], h, (((1,), (1,)), ((), ())), preferred_element_type=f32,
                             precision=lax.Precision.HIGHEST)
    s = jax.nn.sigmoid(logits)
    s3 = s.reshape(ng, pg, tm)
    b3 = (s + br_ref[...]).reshape(ng, pg, tm)
    gi = lax.broadcasted_iota(i32, (ng, pg, tm), 0)
    pi = lax.broadcasted_iota(i32, (ng, pg, tm), 1)
    neg = -jnp.inf
    m1 = jnp.max(b3, axis=1, keepdims=True)
    first = jnp.min(jnp.where(b3 == m1, pi, pg), axis=1, keepdims=True)
    m2 = jnp.max(jnp.where(pi == first, neg, b3), axis=1, keepdims=True)
    gs = m1 + m2
    g2 = lax.broadcasted_iota(i32, (ng, 1, tm), 0)
    gmask = jnp.zeros((ng, 1, tm), jnp.bool_)
    for _ in range(TOPK_GROUPS):
        m = jnp.max(gs, axis=0, keepdims=True)
        fi = jnp.min(jnp.where(gs == m, g2, ng), axis=0, keepdims=True)
        pick = g2 == fi
        gmask = gmask | pick
        gs = jnp.where(pick, neg, gs)
    cand = jnp.where(gmask, b3, neg)
    eid = gi * pg + pi
    sel = jnp.zeros((ng, pg, tm), jnp.bool_)
    picks, idxs, wts = [], [], []
    for _ in range(TOP_K):
        m = jnp.max(jnp.max(cand, axis=0, keepdims=True), axis=1, keepdims=True)
        fi = jnp.min(jnp.min(jnp.where(cand == m, eid, ne), axis=0, keepdims=True), axis=1, keepdims=True)
        pick = eid == fi
        picks.append(pick)
        idxs.append(fi.reshape(1, tm))
        wts.append(jnp.sum(jnp.sum(jnp.where(pick, s3, 0.0), axis=0, keepdims=True), axis=1,
                           keepdims=True).reshape(1, tm))
        sel = sel | pick
        cand = jnp.where(pick, neg, cand)
    wsum = wts[0]
    for k in range(1, TOP_K):
        wsum = wsum + wts[k]
    idx_ref[...] = jnp.concatenate(idxs, axis=0)
    wt_ref[...] = jnp.concatenate([wk / wsum * ROUTED_SCALE for wk in wts], axis=0)
    sel2 = jnp.where(sel, 1.0, 0.0).reshape(ne, tm)
    prefix = (_dot(sel2.astype(bf16), u_ref[...]) + carry_scr[...]).reshape(ng, pg, tm)
    ranks = [jnp.sum(jnp.sum(jnp.where(pk, prefix, 0.0), axis=0, keepdims=True), axis=1,
                     keepdims=True).reshape(1, tm) for pk in picks]
    rank_ref[...] = jnp.concatenate(ranks, axis=0).astype(i32)
    total = carry_scr[...] + jnp.sum(sel2, axis=1, keepdims=True)
    carry_scr[...] = total
    cnt_ref[...] = total


def _router(x, g, mod, which, w_router, b_router, dims, n_rows):
    d = x.shape[1]
    tm = ROW_BLOCK
    ne = N_EXPERTS
    nlb, bpb, nb = dims
    mrow = lambda i: _mod_row(i, nlb, bpb, nb)
    tri = (jnp.arange(tm)[:, None] < jnp.arange(tm)[None, :]).astype(bf16)
    return pl.pallas_call(
        _router_kernel,
        grid=(n_rows // tm,),
        in_specs=[pl.BlockSpec((tm, d), lambda i: (i, 0)),
                  pl.BlockSpec((1, d), lambda i: (0, 0)),
                  pl.BlockSpec((None, 1, d), lambda i: (mrow(i) * 6 + which + 1, 0, 0)),
                  pl.BlockSpec((None, 1, d), lambda i: (mrow(i) * 6 + which, 0, 0)),
                  pl.BlockSpec((ne, d), lambda i: (0, 0)),
                  pl.BlockSpec((ne, 1), lambda i: (0, 0)),
                  pl.BlockSpec((tm, tm), lambda i: (0, 0))],
        out_specs=[pl.BlockSpec((tm, d // 2 // LANE, LANE), lambda i: (i, 0, 0)),
                   pl.BlockSpec((TOP_K, tm), lambda i: (0, i)),
                   pl.BlockSpec((TOP_K, tm), lambda i: (0, i)),
                   pl.BlockSpec((TOP_K, tm), lambda i: (0, i)),
                   pl.BlockSpec((ne, 1), lambda i: (0, 0))],
        out_shape=[jax.ShapeDtypeStruct((n_rows, d // 2 // LANE, LANE), u32),
                   jax.ShapeDtypeStruct((TOP_K, n_rows), i32),
                   jax.ShapeDtypeStruct((TOP_K, n_rows), f32),
                   jax.ShapeDtypeStruct((TOP_K, n_rows), i32),
                   jax.ShapeDtypeStruct((ne, 1), f32)],
        scratch_shapes=[pltpu.VMEM((ne, 1), f32)],
        compiler_params=_params(("arbitrary",)),
    )(x, g.reshape(1, d), mod, mod, w_router.T, b_router.reshape(ne, 1).astype(f32), tri)


EXPERT_BURSTS = 8


def _expert_kernel(be_ref, nu_ref, tok0_ref, tokn_ref, posp_ref, h_hbm, wg_ref, wu_ref, wd_ref, y_hbm,
                   xbuf, ybuf, yrow, wgc, wuc, wdc, sem_g, sem_s):
    i = pl.program_id(0)
    nblk = pl.num_programs(0) - 1
    n_used = nu_ref[0]
    slot = i % 2
    mb = MOE_BLOCK
    de = wgc.shape[1]
    dp = yrow.shape[1]
    spare0 = y_hbm.shape[0] - 2 * mb
    gather_sites = EXPERT_BURSTS - 2

    def gather_row(idx_ref, r, dst_slot, priority=0):
        pltpu.make_async_copy(h_hbm.at[pl.ds(idx_ref[0, 0, r], 1)], xbuf.at[dst_slot, pl.ds(r, 1)],
                              sem_g.at[dst_slot]).start(priority=priority)

    def scatter_row(r):
        p = jnp.where(i == 0, spare0 + r, posp_ref[0, 0, r])
        pltpu.make_async_copy(ybuf.at[1 - slot, pl.ds(r, 1)], y_hbm.at[pl.ds(p, 1)],
                              sem_s.at[slot]).start(priority=r % 2)

    def burst(b, gather=True):
        s_rows = list(range(mb * b // EXPERT_BURSTS, mb * (b + 1) // EXPERT_BURSTS))
        g_rows = []
        if gather and b < gather_sites:
            g_rows = list(range(mb * b // gather_sites, mb * (b + 1) // gather_sites))
        for n in range(max(len(s_rows), len(g_rows))):
            if n < len(g_rows):
                gather_row(tokn_ref, g_rows[n], 1 - slot, g_rows[n] % 2)
            if n < len(s_rows):
                scatter_row(s_rows[n])

    def wait_scatter(parity):
        pltpu.make_async_copy(ybuf.at[0], y_hbm.at[pl.ds(0, mb)], sem_s.at[parity]).wait()

    @pl.when(i == 0)
    def _():
        ybuf[1] = jnp.zeros(ybuf.shape[1:], u32)

        def body(r, carry):
            gather_row(tok0_ref, r, 0)
            pltpu.make_async_copy(ybuf.at[1, pl.ds(r, 1)], y_hbm.at[pl.ds(spare0 + mb + r, 1)], sem_s.at[1]).start()
            return carry
        lax.fori_loop(0, mb, body, 0)

    @pl.when(i <= n_used)
    def _():
        pltpu.make_async_copy(h_hbm.at[pl.ds(0, mb)], xbuf.at[slot], sem_g.at[slot]).wait()

    ib = jnp.minimum(i, nblk - 1)
    changed = (i == 0) | (be_ref[ib] != be_ref[jnp.maximum(ib - 1, 0)])

    @pl.when(changed & (i < n_used))
    def _():
        wgc[...] = wg_ref[...].astype(bf16)
        wuc[...] = wu_ref[...].astype(bf16)
        wdc[...] = wd_ref[...].astype(bf16)

    @pl.when(i < n_used)
    def _():
        lo, hi = _unpack_halves(_tiles_to_rows(xbuf[slot]))
        x = jnp.concatenate([lo.astype(bf16), hi.astype(bf16)], axis=1)
        nh = EXPERT_BURSTS // 4
        hw = de // nh
        hids = []
        for c in range(nh):
            gate = _dot(x, wgc[:, c * hw:(c + 1) * hw])
            burst(2 * c)
            up = _dot(x, wuc[:, c * hw:(c + 1) * hw])
            burst(2 * c + 1)
            hids.append((_silu(gate) * up).astype(bf16))
        hid = jnp.concatenate(hids, axis=1)
        nd = EXPERT_BURSTS // 2
        dw = dp // nd
        for c in range(nd):
            wd_c = jnp.concatenate([wdc[:, c * dw:(c + 1) * dw], wdc[:, dp + c * dw:dp + (c + 1) * dw]], axis=1)
            yrow[:, c * dw:(c + 1) * dw] = _pack_halves(_dot(hid, wd_c))
            burst(2 * nh + c)
        wait_scatter(1 - slot)
        ybuf[slot] = _rows_to_tiles(yrow[...])

    @pl.when(i == n_used)
    def _():
        wait_scatter(1 - slot)
        for b in range(EXPERT_BURSTS):
            burst(b, gather=False)
        wait_scatter(slot)


def _experts(h, slot_tok, slot_pos, block_e, n_used, wg, wu, wd, layer, n_rows):
    tile = h.shape[1:]
    dp = tile[0] * tile[1]
    d = wg.shape[2]
    nblk = slot_tok.shape[0]
    de = wg.shape[3]
    mb = MOE_BLOCK
    last = nblk - 1
    smem = lambda f: pl.BlockSpec((1, 1, mb), f, memory_space=pltpu.SMEM)
    wspec = lambda shape: pl.BlockSpec((None,) + shape, lambda i, be, nu: (layer, be[jnp.minimum(i, last)], 0, 0))
    grid_spec = pltpu.PrefetchScalarGridSpec(
        num_scalar_prefetch=2,
        grid=(nblk + 1,),
        in_specs=[smem(lambda i, be, nu: (0, 0, 0)),
                  smem(lambda i, be, nu: (jnp.minimum(i + 1, last), 0, 0)),
                  smem(lambda i, be, nu: (jnp.clip(i - 1, 0, last), 0, 0)),
                  pl.BlockSpec(memory_space=pl.ANY),
                  wspec((None, d, de)), wspec((None, d, de)), wspec((None, de, d))],
        out_specs=pl.BlockSpec(memory_space=pl.ANY),
        scratch_shapes=[pltpu.VMEM((2, mb) + tile, u32), pltpu.VMEM((2, mb) + tile, u32), pltpu.VMEM((mb, dp), u32),
                        pltpu.VMEM((d, de), bf16), pltpu.VMEM((d, de), bf16), pltpu.VMEM((de, d), bf16),
                        pltpu.SemaphoreType.DMA((2,)), pltpu.SemaphoreType.DMA((2,))],
    )
    st = slot_tok.reshape(nblk, 1, mb)
    sp = slot_pos.reshape(nblk, 1, mb)
    return pl.pallas_call(
        _expert_kernel,
        grid_spec=grid_spec,
        out_shape=jax.ShapeDtypeStruct((n_rows * TOP_K + 2 * mb,) + tile, u32),
        compiler_params=_params(("arbitrary",)),
    )(block_e, n_used, st, st, sp, h, wg, wu, wd)


COMBINE_ROWS = 128


def _combine_kernel(*refs, final_norm):
    y_refs = refs[:TOP_K]
    w_ref, h_ref, sg_ref, su_ref, sd_ref, x_ref, gate_ref, gf_ref, o_ref = refs[TOP_K:]
    lo, hi = _unpack_halves(_tiles_to_rows(h_ref[...]))
    hb = jnp.concatenate([lo.astype(bf16), hi.astype(bf16)], axis=1)
    hid = _silu(_dot(hb, sg_ref[...])) * _dot(hb, su_ref[...])
    acc = _dot(hid.astype(bf16), sd_ref[...])
    shape = y_refs[0].shape
    acc_lo = acc_hi = 0.0
    for k in range(TOP_K):
        lo, hi = _unpack_halves(y_refs[k][...])
        wk = jnp.broadcast_to(w_ref[:, k:k + 1, :], shape)
        acc_lo = acc_lo + lo * wk
        acc_hi = acc_hi + hi * wk
    acc = acc + jnp.concatenate([_tiles_to_rows(acc_lo), _tiles_to_rows(acc_hi)], axis=1)
    out = x_ref[...] + gate_ref[...] * acc
    if final_norm:
        out = out * lax.rsqrt(jnp.mean(out * out, axis=-1, keepdims=True) + EPS) * gf_ref[...]
    o_ref[...] = out


def _combine(y_rows, wts, h, sg, su, sd, x, mod, which, g_final, dims, n_rows, final_norm):
    d = x.shape[1]
    de = sg.shape[1]
    tile = h.shape[1:]
    tm = COMBINE_ROWS
    per = ROW_BLOCK // tm
    nsteps = n_rows // tm
    nlb, bpb, nb = dims
    mrow = lambda i: _mod_row(i // per, nlb, bpb, nb)
    yspec = lambda k: pl.BlockSpec((tm,) + tile, lambda i: (k * nsteps + i, 0, 0))
    w3 = jnp.broadcast_to(wts.T[:, :, None], (n_rows, TOP_K, LANE))
    return pl.pallas_call(
        functools.partial(_combine_kernel, final_norm=final_norm),
        grid=(nsteps,),
        in_specs=[yspec(k) for k in range(TOP_K)] + [
                  pl.BlockSpec((tm, TOP_K, LANE), lambda i: (i, 0, 0)),
                  pl.BlockSpec((tm,) + tile, lambda i: (i, 0, 0)),
                  pl.BlockSpec((d, de), lambda i: (0, 0)),
                  pl.BlockSpec((d, de), lambda i: (0, 0)),
                  pl.BlockSpec((de, d), lambda i: (0, 0)),
                  pl.BlockSpec((tm, d), lambda i: (i, 0)),
                  pl.BlockSpec((None, 1, d), lambda i: (mrow(i) * 6 + which, 0, 0)),
                  pl.BlockSpec((1, d), lambda i: (0, 0))],
        out_specs=pl.BlockSpec((tm, d), lambda i: (i, 0)),
        out_shape=jax.ShapeDtypeStruct((n_rows, d), f32),
        compiler_params=_params(("parallel",)),
    )(*([y_rows] * TOP_K), w3, h, sg, su, sd, x, mod, g_final.reshape(1, d))


def _moe(x, g, mod, which_shift, w_router, b_router, wg, wu, wd, layer, sg, su, sd, g_final, dims, n_rows,
         final_norm):
    ne = N_EXPERTS
    mb = MOE_BLOCK
    h, idx, wts, rank, cnt = _router(x, g, mod, which_shift, w_router, b_router, dims, n_rows)
    counts = cnt[:, 0].astype(i32)
    padded = (counts + mb - 1) // mb * mb
    pend = jnp.cumsum(padded)
    start = pend - padded
    onehot = idx[:, :, None] == jnp.arange(ne, dtype=i32)[None, None, :]
    dest = jnp.sum(jnp.where(onehot, start[None, None, :], 0), axis=-1) + rank
    n_assign = n_rows * TOP_K
    nblk = (n_assign + ne * (mb - 1) + mb - 1) // mb
    n_pad = nblk * mb - n_assign
    pcnt = padded - counts
    pcum = jnp.cumsum(pcnt)
    j = jnp.arange(n_pad, dtype=i32)
    e_j = jnp.sum((pcum[None, :] <= j[:, None]).astype(i32), axis=1)
    base = start + counts - (pcum - pcnt)
    base_j = jnp.sum(jnp.where(e_j[:, None] == jnp.arange(ne, dtype=i32)[None, :], base[None, :], 0), axis=1)
    pad_slot = jnp.where(e_j < ne, base_j + j, pend[-1] + j - pcum[-1])
    keys = jnp.concatenate([dest.reshape(-1), pad_slot])
    vals = jnp.concatenate([jnp.arange(n_assign, dtype=i32), n_assign + pad_slot % mb])
    _, slot_pos = lax.sort((keys, vals), num_keys=1)
    slot_tok = jnp.where(slot_pos < n_assign, slot_pos % n_rows, 0)
    blk_start = jnp.arange(nblk, dtype=i32) * mb
    block_e = jnp.minimum(jnp.sum((pend[None, :] <= blk_start[:, None]).astype(i32), axis=1), ne - 1)
    n_used = (pend[-1:] // mb).astype(i32)
    y_rows = _experts(h, slot_tok.reshape(nblk, mb), slot_pos.reshape(nblk, mb), block_e, n_used, wg, wu, wd,
                      layer, n_rows)
    return _combine(y_rows, wts, h, sg, su, sd, x, mod, which_shift + 2, g_final, dims, n_rows, final_norm)


def _rope_tables(seq, head_dim):
    half = head_dim // 2
    quarter = half // 2
    t = jnp.arange(seq, dtype=i32)
    rows = (t // GRID_W).astype(f32)
    cols = (t % GRID_W).astype(f32)
    freqs = ROPE_THETA ** (-jnp.arange(0, half, 2, dtype=f32) / half)
    ar = rows[:, None] * freqs[None, :]
    ac = cols[:, None] * freqs[None, :]
    cos = jnp.concatenate([jnp.cos(ar), jnp.cos(ar), jnp.cos(ac), jnp.cos(ac)], axis=1)
    sin = jnp.concatenate([-jnp.sin(ar), jnp.sin(ar), -jnp.sin(ac), jnp.sin(ac)], axis=1)
    reps = LANE // head_dim
    cos = jnp.tile(cos, (1, reps))
    sin = jnp.tile(sin, (1, reps))
    cos = jnp.concatenate([cos, jnp.ones((ROW_BLOCK, LANE), f32)], axis=0)
    sin = jnp.concatenate([sin, jnp.zeros((ROW_BLOCK, LANE), f32)], axis=0)
    return cos, sin, quarter


def kernel(x, c, ctx, c_ctx, w_ada, b_ada, g_norm_mix, g_norm_ffn, w_in_even, w_out_even, hgrn_lb_logits,
           g_hgrn_norm, win_sink, w_in_odd, w_out_odd, g_q_norm, g_k_norm, w_router, b_router,
           w_exp_gate, w_exp_up, w_exp_down, w_sh_gate, w_sh_up, w_sh_down, g_norm_final):
    nb, seq, d = x.shape
    lc = ctx.shape[1]
    assert seq % ROW_BLOCK == 0 and lc % ROW_BLOCK == 0 and seq % lc == 0
    t_lat = nb * seq
    t_all = t_lat + nb * lc
    bpb = seq // ROW_BLOCK
    nlb = t_lat // ROW_BLOCK
    dims = (nlb, bpb, nb)

    xs = jnp.concatenate([x.reshape(t_lat, d), ctx.reshape(nb * lc, d)], axis=0)
    cc = jnp.concatenate([c, c_ctx[None, :], jnp.zeros((8 - nb - 1, d), f32)], axis=0)

    lb_w = jax.nn.softmax(hgrn_lb_logits.astype(f32), axis=0)
    lower_bounds = jnp.cumsum(lb_w, axis=0)[1:] - lb_w[0]

    mod = _ada(cc, w_ada, b_ada, 0).reshape(8 * 6, 1, d)
    w_in = w_in_even[0].astype(bf16)
    proj = _normmm(xs, g_norm_mix[0], mod, 0, w_in[:, :HGRN_COLS], f32, dims, tn=1280)
    cos, sin, quarter = _rope_tables(seq, WIN_HEAD_DIM)
    epi = dict(blocks={0: (WIN_HEADS + WIN_KV_HEADS) * WIN_HEAD_DIM // LANE}, quarter=quarter, head_norm=False,
               q_slices=WIN_HEADS * WIN_HEAD_DIM // LANE, q_scale=WIN_HEAD_DIM ** -0.5)
    wproj = _normmm(xs, g_norm_mix[0], mod, 0, w_in[:, HGRN_COLS:], bf16, dims, tn=WIN_COLS,
                    epilogue=epi, tables=(cos, sin))
    o_f, o_b = _hgrn(proj, lower_bounds[0], nb, seq, lc)
    a_mix = _hgrn_out(o_f, o_b, proj, g_hgrn_norm[0])
    b_mix = _win_attn(wproj, win_sink[0], nb, seq, lc)
    xs = _outproj(a_mix, 0, b_mix, 0, w_out_even[0].astype(bf16), xs, mod, 2, dims, t_all)
    xs = _moe(xs, g_norm_ffn[0], mod, 3, w_router[0], b_router[0],
              w_exp_gate, w_exp_up, w_exp_down, 0,
              w_sh_gate[0].astype(bf16), w_sh_up[0].astype(bf16), w_sh_down[0].astype(bf16),
              g_norm_final, dims, t_all, False)

    mod = _ada(cc, w_ada, b_ada, 1).reshape(8 * 6, 1, d)
    cos, sin, quarter = _rope_tables(seq, GLB_HEAD_DIM)
    tn = 1536
    n_sl = tn // LANE
    rope_slices = GLB_HEADS + GLB_KV_HEADS
    epi = dict(blocks={0: min(rope_slices, n_sl), 1: max(rope_slices - n_sl, 0)}, quarter=quarter, head_norm=True,
               q_slices=GLB_HEADS, q_scale=GLB_HEAD_DIM ** -0.5)
    qkv = _normmm(xs, g_norm_mix[1], mod, 0, w_in_odd[0].astype(bf16), bf16, dims, tn=tn,
                  epilogue=epi, tables=(cos, sin), gains=(g_q_norm[0], g_k_norm[0]))
    att = _glb_attn(qkv, nb, seq, lc)
    xl = _outproj(att, 0, att, 1, w_out_odd[0].astype(bf16), xs, mod, 2, dims, t_lat)
    out = _moe(xl, g_norm_ffn[1], mod, 3, w_router[1], b_router[1],
               w_exp_gate, w_exp_up, w_exp_down, 1,
               w_sh_gate[1].astype(bf16), w_sh_up[1].astype(bf16), w_sh_down[1].astype(bf16),
               g_norm_final, dims, t_lat, True)
    return out.reshape(nb, seq, d)
```
